```python
import math
import jax
import jax.numpy as jnp
from jax import lax
import numpy as np

D_MODEL = 1024
BATCH = 4
SEQ = 8192
DEPTH = 2

SSD_HEADS = 16
SSD_HEAD_DIM = 64
SSD_INNER = SSD_HEADS * SSD_HEAD_DIM
SSD_GROUPS = 4
SSD_STATE = 128
SSD_CONV = 4
SSD_CHUNK = 128
SSD_CONV_DIM = SSD_INNER + 2 * SSD_GROUPS * SSD_STATE
SSD_COLS = SSD_INNER + SSD_CONV_DIM + SSD_HEADS

RWKV_HEADS = 16
RWKV_HEAD_DIM = 64
RWKV_DIM = RWKV_HEADS * RWKV_HEAD_DIM
DECAY_LORA = 64
AAA_LORA = 64
GATE_LORA = 128
RWKV_COLS = 3 * RWKV_DIM + DECAY_LORA + AAA_LORA + GATE_LORA
RWKV_GN_EPS = 64e-5

IN_EVEN_COLS = SSD_COLS + RWKV_COLS
MIX_EVEN = SSD_INNER + RWKV_DIM

ATT_HEADS = 16
KV_HEADS = 4
Q_PER_KV = ATT_HEADS // KV_HEADS
HEAD_DIM = 64
Q_DIM = ATT_HEADS * HEAD_DIM
KV_DIM = KV_HEADS * HEAD_DIM
IN_ODD_COLS = Q_DIM + 2 * KV_DIM
WINDOW = 128
ATT_BLOCK = 128

EXPERT_GROUPS = 4
EXPERTS_PER_GROUP = 8
N_EXPERTS = EXPERT_GROUPS * EXPERTS_PER_GROUP
TOP_K = 2
EXPERT_HIDDEN = 512
MOE_BLOCK = 256

NORM_EPS = 1e-6

kernel_name = "hybrid_ssd_rwkv7_swa_sink_hmoe"

F32 = jnp.float32


def rms_norm(x, gain, eps=NORM_EPS):
    xf = x.astype(F32)
    y = xf * lax.rsqrt(jnp.mean(xf * xf, axis=-1, keepdims=True) + eps)
    return (y * gain.astype(F32)).astype(x.dtype)


def causal_depthwise_conv(u, w, bias):
    c = u.shape[-1]
    y = lax.conv_general_dilated(
        u, w[:, None, :].astype(u.dtype), window_strides=(1,),
        padding=[(w.shape[0] - 1, 0)], dimension_numbers=("NWC", "WIO", "NWC"),
        feature_group_count=c)
    return y + bias.astype(u.dtype)


def segsum(a):
    n = a.shape[-1]
    cs = jnp.cumsum(a, axis=-1)
    diff = cs[..., :, None] - cs[..., None, :]
    mask = jnp.tril(jnp.ones((n, n), dtype=bool))
    return jnp.where(mask, diff, -jnp.inf)


def ssd_chunked(xs, adt, bm, cm):
    b, t, h, p = xs.shape
    g, n = bm.shape[2], bm.shape[3]
    k = h // g
    c = t // SSD_CHUNK
    xs = xs.reshape(b, c, SSD_CHUNK, g, k, p)
    bm = bm.reshape(b, c, SSD_CHUNK, g, n)
    cm = cm.reshape(b, c, SSD_CHUNK, g, n)
    adt = adt.reshape(b, c, SSD_CHUNK, g, k).transpose(0, 3, 4, 1, 2)
    a_cum = jnp.cumsum(adt, axis=-1)
    decay_in = jnp.exp(segsum(adt))
    cb = jnp.einsum("bclgn,bcsgn->bcgls", cm, bm)
    y_diag = jnp.einsum("bcgls,bgkcls,bcsgkp->bclgkp", cb, decay_in, xs)
    decay_states = jnp.exp(a_cum[..., -1:] - a_cum)
    states = jnp.einsum("bclgn,bgkcl,bclgkp->bcgkpn", bm, decay_states, xs)
    states = jnp.concatenate([jnp.zeros_like(states[:, :1]), states], axis=1)
    chunk_tot = jnp.pad(a_cum[..., -1], ((0, 0), (0, 0), (0, 0), (1, 0)))
    decay_chunk = jnp.exp(segsum(chunk_tot))
    states = jnp.einsum("bgkzc,bcgkpn->bzgkpn", decay_chunk, states)[:, :-1]
    y_off = jnp.einsum("bclgn,bcgkpn,bgkcl->bclgkp", cm, states, jnp.exp(a_cum))
    return (y_diag + y_off).reshape(b, t, h, p)


def ssd_mixer(z, xbc, dt, conv_w, conv_b, dt_bias, a_log, d_skip, norm_w):
    b, t, _ = xbc.shape
    xbc = jax.nn.silu(causal_depthwise_conv(xbc, conv_w, conv_b)).astype(F32)
    gn = SSD_GROUPS * SSD_STATE
    xs = xbc[..., :SSD_INNER].reshape(b, t, SSD_HEADS, SSD_HEAD_DIM)
    bm = xbc[..., SSD_INNER:SSD_INNER + gn].reshape(b, t, SSD_GROUPS, SSD_STATE)
    cm = xbc[..., SSD_INNER + gn:].reshape(b, t, SSD_GROUPS, SSD_STATE)
    dt = jax.nn.softplus(dt.astype(F32) + dt_bias.astype(F32))
    a = -jnp.exp(a_log.astype(F32))
    y = ssd_chunked(xs * dt[..., None], a * dt, bm, cm) + d_skip.astype(F32)[:, None] * xs
    y = y.reshape(b, t, SSD_INNER) * jax.nn.silu(z.astype(F32))
    yg = y.reshape(b, t, SSD_GROUPS, SSD_INNER // SSD_GROUPS)
    yg = yg * lax.rsqrt(jnp.mean(yg * yg, axis=-1, keepdims=True) + 1e-5)
    y = yg.reshape(b, t, SSD_INNER) * norm_w.astype(F32)
    return y.astype(z.dtype)


def wkv7_scan(r, w, k, v, a, bb):
    bsz, _, h, dk = r.shape

    def step(s, inp):
        r_t, w_t, k_t, v_t, a_t, b_t = inp
        sa = jnp.einsum("bhvk,bhk->bhv", s, a_t)
        s = s * w_t[:, :, None, :] + sa[..., None] * b_t[:, :, None, :] + v_t[..., None] * k_t[:, :, None, :]
        return s, jnp.einsum("bhvk,bhk->bhv", s, r_t)

    s0 = jnp.zeros((bsz, h, dk, dk), F32)
    xs = tuple(jnp.moveaxis(u, 1, 0) for u in (r, w, k, v, a, bb))
    _, ys = lax.scan(step, s0, xs)
    return jnp.moveaxis(ys, 0, 1)


def rwkv7_mixer(u, mu, w0, w2, a0, a2, g2, k_k, k_a, r_k, ln_w, ln_b):
    b, t, _ = u.shape
    prev = jnp.pad(u, ((0, 0), (1, 0), (0, 0)))[:, :-1]
    u = (u + mu.astype(u.dtype) * (prev - u)).astype(F32)
    o1, o2, o3 = RWKV_DIM, 2 * RWKV_DIM, 3 * RWKV_DIM
    o4, o5 = o3 + DECAY_LORA, o3 + DECAY_LORA + AAA_LORA
    r, k, v = u[..., :o1], u[..., o1:o2], u[..., o2:o3]
    wl, al, gl = u[..., o3:o4], u[..., o4:o5], u[..., o5:]
    w = -jax.nn.softplus(-(w0.astype(F32) + jnp.tanh(wl) @ w2.astype(F32))) - 0.5
    decay = jnp.exp(-jnp.exp(w))
    a = jax.nn.sigmoid(a0.astype(F32) + al @ a2.astype(F32))
    g = jax.nn.sigmoid(gl) @ g2.astype(F32)
    hs = (b, t, RWKV_HEADS, RWKV_HEAD_DIM)
    kk = (k * k_k.astype(F32)).reshape(hs)
    kk = kk / jnp.maximum(jnp.sqrt(jnp.sum(kk * kk, axis=-1, keepdims=True)), 1e-12)
    k = k * (1.0 + (a - 1.0) * k_a.astype(F32))
    r, k, v, a, decay = (z.reshape(hs) for z in (r, k, v, a, decay))
    y = wkv7_scan(r, decay, k, v, -kk, kk * a)
    mean = jnp.mean(y, axis=-1, keepdims=True)
    var = jnp.mean(jnp.square(y - mean), axis=-1, keepdims=True)
    y = (y - mean) * lax.rsqrt(var + RWKV_GN_EPS)
    y = y * ln_w.astype(F32).reshape(RWKV_HEADS, RWKV_HEAD_DIM) + ln_b.astype(F32).reshape(RWKV_HEADS, RWKV_HEAD_DIM)
    y = y + jnp.sum(r * k * r_k.astype(F32), axis=-1, keepdims=True) * v
    y = y.reshape(b, t, RWKV_DIM) * g
    return y.astype(mu.dtype)


def alibi_slopes(n_heads):
    return 2.0 ** (-8.0 * jnp.arange(1, n_heads + 1, dtype=F32) / n_heads)


def swa_sink_attention(proj, q_gain, k_gain, sinks):
    b, t, _ = proj.shape
    nb = t // ATT_BLOCK
    q = proj[..., :Q_DIM].reshape(b, t, KV_HEADS, Q_PER_KV, HEAD_DIM)
    k = proj[..., Q_DIM:Q_DIM + KV_DIM].reshape(b, t, KV_HEADS, HEAD_DIM)
    v = proj[..., Q_DIM + KV_DIM:].reshape(b, t, KV_HEADS, HEAD_DIM)
    q = rms_norm(q, q_gain).astype(F32).reshape(b, nb, ATT_BLOCK, KV_HEADS, Q_PER_KV, HEAD_DIM)
    k = rms_norm(k, k_gain).astype(F32).reshape(b, nb, ATT_BLOCK, KV_HEADS, HEAD_DIM)
    v = v.astype(F32).reshape(b, nb, ATT_BLOCK, KV_HEADS, HEAD_DIM)

    def with_prev(z):
        prev = jnp.pad(z, [(0, 0), (1, 0)] + [(0, 0)] * (z.ndim - 2))[:, :-1]
        return jnp.concatenate([prev, z], axis=2)

    kw, vw = with_prev(k), with_prev(v)
    s = jnp.einsum("bnqkgd,bnskd->bnkgqs", q, kw) * (HEAD_DIM ** -0.5)
    qi = jnp.arange(ATT_BLOCK)[:, None]
    sj = jnp.arange(2 * ATT_BLOCK)[None, :]
    delta = qi + ATT_BLOCK - sj
    in_band = (delta >= 0) & (delta < WINDOW)
    has_key = (jnp.arange(nb)[:, None] > 0) | (sj >= ATT_BLOCK)
    mask = in_band[None] & has_key[:, None, :]
    slopes = alibi_slopes(ATT_HEADS).reshape(KV_HEADS, Q_PER_KV)
    s = s - slopes[:, :, None, None] * delta.astype(F32)
    s = jnp.where(mask[None, :, None, None], s, -jnp.inf)
    sink = sinks.astype(F32).reshape(KV_HEADS, Q_PER_KV)[None, None, :, :, None]
    m = jnp.maximum(jnp.max(s, axis=-1), sink)
    p = jnp.exp(s - m[..., None])
    denom = jnp.sum(p, axis=-1) + jnp.exp(sink - m)
    o = jnp.einsum("bnkgqs,bnskd->bnkgqd", p, vw) / denom[..., None]
    o = o.transpose(0, 1, 4, 2, 3, 5).reshape(b, t, Q_DIM)
    return o.astype(proj.dtype)


def hier_moe(x, w_coarse, b_coarse, w_fine, b_fine, w_gate, w_up, w_down):
    b, t, d = x.shape
    n = b * t
    xt = x.reshape(n, d)
    xf = xt.astype(F32)
    coarse_p = jax.nn.softmax(xf @ w_coarse.astype(F32) + b_coarse.astype(F32), axis=-1)
    p_group, group = lax.top_k(coarse_p, 1)
    fine_logits = jnp.einsum("nd,gde->nge", xf, w_fine.astype(F32)) + b_fine.astype(F32)
    fine_sel = fine_logits[jnp.arange(n), group[:, 0]]
    p_exp, idx = lax.top_k(jax.nn.softmax(fine_sel, axis=-1), TOP_K)
    gates = p_group * p_exp / jnp.sum(p_exp, axis=-1, keepdims=True)
    expert = group * EXPERTS_PER_GROUP + idx
    a = n * TOP_K
    e_flat = expert.reshape(a)
    tok_flat = jnp.repeat(jnp.arange(n, dtype=jnp.int32), TOP_K)
    gate_flat = gates.reshape(a).astype(x.dtype)
    order = jnp.argsort(e_flat)
    e_sorted = e_flat[order]
    counts = jnp.bincount(e_flat, length=N_EXPERTS)
    padded = (counts + MOE_BLOCK - 1) // MOE_BLOCK * MOE_BLOCK
    seg_start = jnp.cumsum(counts) - counts
    pad_end = jnp.cumsum(padded)
    pad_start = pad_end - padded
    dest = pad_start[e_sorted] + jnp.arange(a) - seg_start[e_sorted]
    n_blocks = -(-a // MOE_BLOCK) + N_EXPERTS
    slots = n_blocks * MOE_BLOCK
    tok_buf = jnp.full((slots,), n, jnp.int32).at[dest].set(tok_flat[order])
    gate_buf = jnp.zeros((slots,), x.dtype).at[dest].set(gate_flat[order])
    block_expert = jnp.minimum(
        jnp.searchsorted(pad_end, jnp.arange(n_blocks) * MOE_BLOCK, side="right"), N_EXPERTS - 1)
    x_pad = jnp.concatenate([xt, jnp.zeros((1, d), xt.dtype)], axis=0)

    def run_block(args):
        tok, e = args
        xb = x_pad[tok]
        hb = jax.nn.silu(xb @ w_gate[e]) * (xb @ w_up[e])
        return hb @ w_down[e]

    y_buf = lax.map(run_block, (tok_buf.reshape(n_blocks, MOE_BLOCK), block_expert))
    y = jax.ops.segment_sum(y_buf.reshape(slots, d) * gate_buf[:, None], tok_buf, num_segments=n + 1)[:n]
    return y.reshape(b, t, d)


def setup_inputs(seed: int = 0) -> dict:
    key = jax.random.key(seed)
    ks = iter(jax.random.split(key, 64))
    ne = (DEPTH + 1) // 2
    no = DEPTH // 2

    def nrm(shape, scale):
        return jax.random.normal(next(ks), shape, F32) * scale

    def uni(shape, lo, hi):
        return jax.random.uniform(next(ks), shape, F32, lo, hi)

    d = D_MODEL
    x = nrm((BATCH, SEQ, d), 1.0)
    ln_mix = 1.0 + nrm((DEPTH, d), 0.02)
    ln_ffn = 1.0 + nrm((DEPTH, d), 0.02)
    e_w_in = nrm((ne, d, IN_EVEN_COLS), d ** -0.5)
    e_w_out = nrm((ne, MIX_EVEN, d), MIX_EVEN ** -0.5)
    ssd_conv_w = nrm((ne, SSD_CONV, SSD_CONV_DIM), SSD_CONV ** -0.5)
    ssd_conv_b = nrm((ne, SSD_CONV_DIM), 0.02)
    dt0 = jnp.exp(uni((ne, SSD_HEADS), math.log(1e-3), math.log(1e-1)))
    ssd_dt_bias = dt0 + jnp.log(-jnp.expm1(-dt0))
    ssd_a_log = jnp.log(uni((ne, SSD_HEADS), 1.0, 16.0))
    ssd_d = 1.0 + nrm((ne, SSD_HEADS), 0.1)
    ssd_norm = 1.0 + nrm((ne, SSD_INNER), 0.02)
    rwkv_mu = uni((ne, RWKV_COLS), 0.0, 1.0)
    ramp = jnp.linspace(0.0, 1.0, RWKV_DIM, dtype=F32) ** 0.85
    rwkv_w0 = -6.5 + 5.0 * ramp + nrm((ne, RWKV_DIM), 0.05)
    rwkv_w2 = nrm((ne, DECAY_LORA, RWKV_DIM), 0.1 * DECAY_LORA ** -0.5)
    rwkv_a0 = nrm((ne, RWKV_DIM), 0.1)
    rwkv_a2 = nrm((ne, AAA_LORA, RWKV_DIM), 0.1 * AAA_LORA ** -0.5)
    rwkv_g2 = nrm((ne, GATE_LORA, RWKV_DIM), GATE_LORA ** -0.5)
    rwkv_k_k = 0.85 + nrm((ne, RWKV_DIM), 0.02)
    rwkv_k_a = 1.0 + nrm((ne, RWKV_DIM), 0.02)
    rwkv_r_k = nrm((ne, RWKV_HEADS, RWKV_HEAD_DIM), 0.1)
    rwkv_ln_w = 1.0 + nrm((ne, RWKV_DIM), 0.02)
    rwkv_ln_b = nrm((ne, RWKV_DIM), 0.02)
    o_w_in = nrm((no, d, IN_ODD_COLS), d ** -0.5)
    o_w_out = nrm((no, Q_DIM, d), Q_DIM ** -0.5)
    attn_q_norm = 1.0 + nrm((no, HEAD_DIM), 0.02)
    attn_k_norm = 1.0 + nrm((no, HEAD_DIM), 0.02)
    attn_sinks = nrm((no, ATT_HEADS), 1.0)
    moe_w_coarse = nrm((DEPTH, d, EXPERT_GROUPS), d ** -0.5)
    moe_b_coarse = nrm((DEPTH, EXPERT_GROUPS), 0.01)
    moe_w_fine = nrm((DEPTH, EXPERT_GROUPS, d, EXPERTS_PER_GROUP), d ** -0.5)
    moe_b_fine = nrm((DEPTH, EXPERT_GROUPS, EXPERTS_PER_GROUP), 0.01)
    moe_w_gate = nrm((DEPTH, N_EXPERTS, d, EXPERT_HIDDEN), d ** -0.5)
    moe_w_up = nrm((DEPTH, N_EXPERTS, d, EXPERT_HIDDEN), d ** -0.5)
    moe_w_down = nrm((DEPTH, N_EXPERTS, EXPERT_HIDDEN, d), EXPERT_HIDDEN ** -0.5)
    return {
        "x": x, "ln_mix": ln_mix, "ln_ffn": ln_ffn,
        "e_w_in": e_w_in, "e_w_out": e_w_out,
        "ssd_conv_w": ssd_conv_w, "ssd_conv_b": ssd_conv_b, "ssd_dt_bias": ssd_dt_bias,
        "ssd_a_log": ssd_a_log, "ssd_d": ssd_d, "ssd_norm": ssd_norm,
        "rwkv_mu": rwkv_mu, "rwkv_w0": rwkv_w0, "rwkv_w2": rwkv_w2, "rwkv_a0": rwkv_a0,
        "rwkv_a2": rwkv_a2, "rwkv_g2": rwkv_g2, "rwkv_k_k": rwkv_k_k, "rwkv_k_a": rwkv_k_a,
        "rwkv_r_k": rwkv_r_k, "rwkv_ln_w": rwkv_ln_w, "rwkv_ln_b": rwkv_ln_b,
        "o_w_in": o_w_in, "o_w_out": o_w_out, "attn_q_norm": attn_q_norm,
        "attn_k_norm": attn_k_norm, "attn_sinks": attn_sinks,
        "moe_w_coarse": moe_w_coarse, "moe_b_coarse": moe_b_coarse,
        "moe_w_fine": moe_w_fine, "moe_b_fine": moe_b_fine,
        "moe_w_gate": moe_w_gate, "moe_w_up": moe_w_up, "moe_w_down": moe_w_down,
    }


def reference(x, ln_mix, ln_ffn, e_w_in, e_w_out, ssd_conv_w, ssd_conv_b, ssd_dt_bias,
              ssd_a_log, ssd_d, ssd_norm, rwkv_mu, rwkv_w0, rwkv_w2, rwkv_a0, rwkv_a2,
              rwkv_g2, rwkv_k_k, rwkv_k_a, rwkv_r_k, rwkv_ln_w, rwkv_ln_b, o_w_in, o_w_out,
              attn_q_norm, attn_k_norm, attn_sinks, moe_w_coarse, moe_b_coarse, moe_w_fine,
              moe_b_fine, moe_w_gate, moe_w_up, moe_w_down):
    h = x
    for layer in range(DEPTH):
        i = layer // 2
        hn = rms_norm(h, ln_mix[layer])
        if layer % 2 == 0:
            proj = hn @ e_w_in[i]
            z = proj[..., :SSD_INNER]
            xbc = proj[..., SSD_INNER:SSD_INNER + SSD_CONV_DIM]
            dt = proj[..., SSD_INNER + SSD_CONV_DIM:SSD_COLS]
            y_ssd = ssd_mixer(z, xbc, dt, ssd_conv_w[i], ssd_conv_b[i], ssd_dt_bias[i],
                              ssd_a_log[i], ssd_d[i], ssd_norm[i])
            y_rwkv = rwkv7_mixer(proj[..., SSD_COLS:], rwkv_mu[i], rwkv_w0[i], rwkv_w2[i],
                                 rwkv_a0[i], rwkv_a2[i], rwkv_g2[i], rwkv_k_k[i], rwkv_k_a[i],
                                 rwkv_r_k[i], rwkv_ln_w[i], rwkv_ln_b[i])
            mixed = jnp.concatenate([y_ssd, y_rwkv.astype(y_ssd.dtype)], axis=-1)
            h = h + (mixed @ e_w_out[i]).astype(h.dtype)
        else:
            att = swa_sink_attention(hn @ o_w_in[i], attn_q_norm[i], attn_k_norm[i], attn_sinks[i])
            h = h + (att @ o_w_out[i]).astype(h.dtype)
        hf = rms_norm(h, ln_ffn[layer])
        h = h + hier_moe(hf, moe_w_coarse[layer], moe_b_coarse[layer], moe_w_fine[layer],
                         moe_b_fine[layer], moe_w_gate[layer], moe_w_up[layer],
                         moe_w_down[layer]).astype(h.dtype)
    return h
```

```python
import functools
import math

import jax
import jax.numpy as jnp
from jax import lax
from jax.experimental import pallas as pl
from jax.experimental.pallas import tpu as pltpu

F32 = jnp.float32
BF16 = jnp.bfloat16

D_MODEL = 1024
SSD_HEADS = 16
SSD_HEAD_DIM = 64
SSD_INNER = 1024
SSD_GROUPS = 4
SSD_STATE = 128
SSD_CONV = 4
SSD_CONV_DIM = 2048
SSD_COLS = 3088
RWKV_HEADS = 16
RWKV_HEAD_DIM = 64
RWKV_DIM = 1024
DECAY_LORA = 64
AAA_LORA = 64
GATE_LORA = 128
RWKV_GN_EPS = 64e-5
ATT_HEADS = 16
KV_HEADS = 4
Q_PER_KV = 4
HEAD_DIM = 64
Q_DIM = 1024
KV_DIM = 256
WINDOW = 128
EXPERT_GROUPS = 4
EXPERTS_PER_GROUP = 8
N_EXPERTS = 32
TOP_K = 2
EXPERT_HIDDEN = 512
MOE_BLOCK = 256
NORM_EPS = 1e-6

LANES = 128
CHUNK = 128
TAIL = 8
EVEN_COLS_PAD = 6528
VMEM_LIMIT = 56 * 1024 * 1024
NEG_BIG = -1e30


def _cparams(sem):
    return pltpu.CompilerParams(dimension_semantics=sem, vmem_limit_bytes=VMEM_LIMIT)


def _dot(a, b):
    return jnp.dot(a.astype(BF16), b.astype(BF16), preferred_element_type=F32)


def _dot_nt(a, b):
    return lax.dot_general(a.astype(BF16), b.astype(BF16), (((1,), (1,)), ((), ())),
                           preferred_element_type=F32)


def _dot_tn(a, b):
    return lax.dot_general(a.astype(BF16), b.astype(BF16), (((0,), (0,)), ((), ())),
                           preferred_element_type=F32)


def _split3(x):
    x1 = x.astype(BF16)
    r1 = x - x1.astype(F32)
    x2 = r1.astype(BF16)
    x3 = (r1 - x2.astype(F32)).astype(BF16)
    return x1, x2, x3


def _dot_exact_lhs(m_bf16, x):
    x1, x2, x3 = _split3(x)
    f = lambda p: jnp.dot(m_bf16, p, preferred_element_type=F32)
    return f(x1) + f(x2) + f(x3)


def _dot_exact_rhs(x, m_bf16):
    x1, x2, x3 = _split3(x)
    f = lambda p: jnp.dot(p, m_bf16, preferred_element_type=F32)
    return f(x1) + f(x2) + f(x3)


def _dot_f32(x, w):
    x1, x2, x3 = _split3(x)
    w1, w2, w3 = _split3(w)
    f = lambda p, q: jnp.dot(p, q, preferred_element_type=F32)
    return f(x1, w1) + (f(x1, w2) + f(x2, w1)) + (f(x2, w2) + f(x1, w3) + f(x3, w1))


def _sigmoid(x):
    return 1.0 / (1.0 + jnp.exp(-x))


def _silu(x):
    return x * _sigmoid(x)


def _softplus(x):
    return jnp.maximum(x, 0.0) + jnp.log(1.0 + jnp.exp(-jnp.abs(x)))


def _rms(x, gain):
    return x * lax.rsqrt(jnp.mean(x * x, axis=-1, keepdims=True) + NORM_EPS) * gain


def _iota2(shape, dim):
    return lax.broadcasted_iota(jnp.int32, shape, dim)


def _tri_incl_bf16(n):
    return jnp.where(_iota2((n, n), 0) >= _iota2((n, n), 1), 1.0, 0.0).astype(BF16)


def _norm_proj_kernel(h_ref, g_ref, w_ref, o_ref):
    o_ref[...] = _dot(_rms(h_ref[...], g_ref[...]), w_ref[...])


def _norm_proj(h, gain, w_bf16, tm, tn):
    n, d = h.shape
    c = w_bf16.shape[1]
    return pl.pallas_call(
        _norm_proj_kernel,
        grid=(c // tn, n // tm),
        in_specs=[pl.BlockSpec((tm, d), lambda j, i: (i, 0)),
                  pl.BlockSpec((1, d), lambda j, i: (0, 0)),
                  pl.BlockSpec((d, tn), lambda j, i: (0, j))],
        out_specs=pl.BlockSpec((tm, tn), lambda j, i: (i, j)),
        out_shape=jax.ShapeDtypeStruct((n, c), F32),
        compiler_params=_cparams(("arbitrary", "arbitrary")),
        name="norm_proj",
    )(h, gain.reshape(1, d), w_bf16)


def _combine_rows(h_ref, y2_ref, route_ref):
    r = route_ref[...]
    return h_ref[...] + r[:, 2:3] * y2_ref[:, :D_MODEL] + r[:, 3:4] * y2_ref[:, D_MODEL:]


def _combine_norm_proj_kernel(h_ref, y2_ref, route_ref, g_ref, w_ref, hnew_ref, o_ref):
    x = _combine_rows(h_ref, y2_ref, route_ref)
    hnew_ref[...] = x
    o_ref[...] = _dot(_rms(x, g_ref[...]), w_ref[...])


def _combine_norm_proj(h, y2, route, gain, w_bf16, tm):
    n, d = h.shape
    c = w_bf16.shape[1]
    return pl.pallas_call(
        _combine_norm_proj_kernel,
        grid=(n // tm,),
        in_specs=[pl.BlockSpec((tm, d), lambda i: (i, 0)),
                  pl.BlockSpec((tm, 2 * d), lambda i: (i, 0)),
                  pl.BlockSpec((tm, LANES), lambda i: (i, 0)),
                  pl.BlockSpec((1, d), lambda i: (0, 0)),
                  pl.BlockSpec((d, c), lambda i: (0, 0))],
        out_specs=[pl.BlockSpec((tm, d), lambda i: (i, 0)),
                   pl.BlockSpec((tm, c), lambda i: (i, 0))],
        out_shape=[jax.ShapeDtypeStruct((n, d), F32), jax.ShapeDtypeStruct((n, c), F32)],
        compiler_params=_cparams(("arbitrary",)),
        name="combine_norm_proj",
    )(h, y2, route, gain.reshape(1, d), w_bf16)


def _combine_kernel(h_ref, y2_ref, route_ref, o_ref):
    o_ref[...] = _combine_rows(h_ref, y2_ref, route_ref)


def _combine(h, y2, route, tm):
    n, d = h.shape
    return pl.pallas_call(
        _combine_kernel,
        grid=(n // tm,),
        in_specs=[pl.BlockSpec((tm, d), lambda i: (i, 0)),
                  pl.BlockSpec((tm, 2 * d), lambda i: (i, 0)),
                  pl.BlockSpec((tm, LANES), lambda i: (i, 0))],
        out_specs=pl.BlockSpec((tm, d), lambda i: (i, 0)),
        out_shape=jax.ShapeDtypeStruct((n, d), F32),
        compiler_params=_cparams(("arbitrary",)),
        name="combine",
    )(h, y2, route)


def _shifted_taps(buf, u_ref, n_taps):
    buf[TAIL:TAIL + CHUNK, :] = u_ref[...]
    taps = [buf[TAIL - j:TAIL - j + CHUNK, :] for j in range(n_taps)]
    return taps


def _carry_tail(buf):
    buf[0:TAIL, :] = buf[CHUNK:CHUNK + TAIL, :]


def _ssd_kernel(z_ref, x_ref, bc_ref, dt_ref, cwx_ref, cbx_ref, cwbc_ref, cbbc_ref, dtb_ref, alog_ref,
                dskip_ref, nw_ref, hexp_ref, o_ref, xbuf, bcbuf, state):
    c = pl.program_id(1)

    @pl.when(c == 0)
    def _():
        xbuf[0:TAIL, :] = jnp.zeros((TAIL, SSD_INNER), F32)
        bcbuf[0:TAIL, :] = jnp.zeros((TAIL, SSD_INNER), F32)
        state[...] = jnp.zeros_like(state)

    def conv(buf, u_ref, w_ref, b_ref):
        taps = _shifted_taps(buf, u_ref, SSD_CONV)
        acc = b_ref[...] + taps[0] * w_ref[3:4, :]
        for j in range(1, SSD_CONV):
            acc = acc + taps[j] * w_ref[3 - j:4 - j, :]
        _carry_tail(buf)
        return _silu(acc)

    xs = conv(xbuf, x_ref, cwx_ref, cbx_ref)
    bc = conv(bcbuf, bc_ref, cwbc_ref, cbbc_ref)

    lane = _iota2((CHUNK, LANES), 1)
    dt = _softplus(dt_ref[...] + dtb_ref[...])
    adt = jnp.where(lane < SSD_HEADS, -jnp.exp(alog_ref[...]) * dt, 0.0)
    tri = _tri_incl_bf16(CHUNK)
    cum = _dot_exact_lhs(tri, adt)
    cum_t = cum.T
    hexp = hexp_ref[...]
    cum_full = _dot_exact_rhs(cum, hexp)
    dt_full = _dot_exact_rhs(dt, hexp)
    tot_full = cum_full[CHUNK - 1:CHUNK, :]
    xd = xs * dt_full
    xds = xd * jnp.exp(tot_full - cum_full)
    eac = jnp.exp(cum_full)

    row = _iota2((CHUNK, CHUNK), 0)
    col = _iota2((CHUNK, CHUNK), 1)
    causal = row >= col
    lane_lo = lane < SSD_HEAD_DIM

    gw = SSD_INNER // SSD_GROUPS
    y_parts = []
    for g in range(SSD_GROUPS):
        bg = bc[:, g * SSD_STATE:(g + 1) * SSD_STATE]
        cg = bc[:, (SSD_GROUPS + g) * SSD_STATE:(SSD_GROUPS + g + 1) * SSD_STATE]
        cb = _dot_nt(cg, bg)
        s_prev = state[:, g * gw:(g + 1) * gw]
        y_off = _dot(cg, s_prev) * eac[:, g * gw:(g + 1) * gw]
        s_new = _dot(bg.T, xds[:, g * gw:(g + 1) * gw])
        state[:, g * gw:(g + 1) * gw] = jnp.exp(tot_full[:, g * gw:(g + 1) * gw]) * s_prev + s_new
        for pr in range(2):
            lo = g * gw + pr * LANES
            xd_pair = xd[:, lo:lo + LANES]
            yd = jnp.zeros((CHUNK, LANES), F32)
            for k in range(2):
                h = (lo // SSD_HEAD_DIM) + k
                diff = cum[:, h:h + 1] - cum_t[h:h + 1, :]
                decay = jnp.exp(jnp.where(causal, diff, NEG_BIG))
                keep = lane_lo if k == 0 else jnp.logical_not(lane_lo)
                yd = yd + _dot(cb * decay, jnp.where(keep, xd_pair, 0.0))
            y_parts.append(yd + y_off[:, pr * LANES:(pr + 1) * LANES])
    y = jnp.concatenate(y_parts, axis=1) + dskip_ref[...] * xs
    y = y * _silu(z_ref[...])
    outs = []
    for g in range(SSD_GROUPS):
        yg = y[:, g * gw:(g + 1) * gw]
        outs.append(yg * lax.rsqrt(jnp.mean(yg * yg, axis=-1, keepdims=True) + 1e-5))
    o_ref[...] = (jnp.concatenate(outs, axis=1) * nw_ref[...]).astype(o_ref.dtype)


def _head_expand(n_heads, head_dim):
    h = jnp.arange(LANES)[:, None]
    l = jnp.arange(n_heads * head_dim)[None, :]
    return (l // head_dim == h).astype(BF16)


def _ssd(proj, batch, seq, conv_w, conv_b, dt_bias, a_log, d_skip, norm_w):
    nc = seq // CHUNK
    n = batch * seq
    pad16 = lambda v: jnp.pad(v.astype(F32), (0, LANES - v.shape[0])).reshape(1, LANES)
    row_spec = lambda w, blk: pl.BlockSpec((CHUNK, w), lambda b, c: (b * nc + c, blk))
    full = lambda shape: pl.BlockSpec(shape, lambda b, c: (0, 0))
    return pl.pallas_call(
        _ssd_kernel,
        grid=(batch, nc),
        in_specs=[row_spec(SSD_INNER, 0),
                  row_spec(SSD_INNER, 1),
                  row_spec(SSD_INNER, 2),
                  row_spec(LANES, 50),
                  full((SSD_CONV, SSD_INNER)), full((1, SSD_INNER)),
                  full((SSD_CONV, SSD_INNER)), full((1, SSD_INNER)),
                  full((1, LANES)), full((1, LANES)),
                  full((1, SSD_INNER)), full((1, SSD_INNER)),
                  full((LANES, SSD_INNER))],
        out_specs=pl.BlockSpec((CHUNK, SSD_INNER), lambda b, c: (b * nc + c, 0)),
        out_shape=jax.ShapeDtypeStruct((n, SSD_INNER), BF16),
        scratch_shapes=[pltpu.VMEM((TAIL + CHUNK, SSD_INNER), F32),
                        pltpu.VMEM((TAIL + CHUNK, SSD_INNER), F32),
                        pltpu.VMEM((SSD_STATE, SSD_INNER), F32)],
        compiler_params=_cparams(("arbitrary", "arbitrary")),
        name="ssd",
    )(proj, proj, proj, proj,
      conv_w[:, :SSD_INNER], conv_b[:SSD_INNER].reshape(1, -1),
      conv_w[:, SSD_INNER:], conv_b[SSD_INNER:].reshape(1, -1),
      pad16(dt_bias), pad16(a_log),
      jnp.repeat(d_skip.astype(F32), SSD_HEAD_DIM).reshape(1, -1), norm_w.reshape(1, -1),
      _head_expand(SSD_HEADS, SSD_HEAD_DIM))


def _neumann_inverse(a):
    n = a.shape[0]
    eye = jnp.where(_iota2((n, n), 0) == _iota2((n, n), 1), 1.0, 0.0)
    t = eye + a
    x = a
    for _ in range(int(math.log2(n)) - 1):
        x = _dot(x, x)
        t = t + _dot(t, x)
    return t


def _rwkv_kernel(r_ref, k_ref, v_ref, lo_ref, mur_ref, muk_ref, muv_ref, mulo_ref, w0_ref, w2_ref, a0_ref,
                 a2_ref, g2_ref, kk_ref, ka_ref, rk_ref, lnw_ref, lnb_ref, o_ref,
                 rbuf, kbuf, vbuf, lobuf, state):
    c = pl.program_id(1)

    @pl.when(c == 0)
    def _():
        rbuf[0:TAIL, :] = jnp.zeros((TAIL, RWKV_DIM), F32)
        kbuf[0:TAIL, :] = jnp.zeros((TAIL, RWKV_DIM), F32)
        vbuf[0:TAIL, :] = jnp.zeros((TAIL, RWKV_DIM), F32)
        lobuf[0:TAIL, :] = jnp.zeros((TAIL, 2 * LANES), F32)
        state[...] = jnp.zeros_like(state)

    def shift(buf, u_ref, mu_ref):
        cur, prev = _shifted_taps(buf, u_ref, 2)
        _carry_tail(buf)
        return cur + mu_ref[...] * (prev - cur)

    r = shift(rbuf, r_ref, mur_ref)
    k = shift(kbuf, k_ref, muk_ref)
    v = shift(vbuf, v_ref, muv_ref)
    lo = shift(lobuf, lo_ref, mulo_ref)
    wa = lo[:, :LANES]
    w = -_softplus(-(w0_ref[...] + _dot(jnp.tanh(wa), w2_ref[...]))) - 0.5
    logw = -jnp.exp(w)
    a = _sigmoid(a0_ref[...] + _dot(wa, a2_ref[...]))
    g = _dot(_sigmoid(lo[:, LANES:]), g2_ref[...])

    lane = _iota2((CHUNK, LANES), 1)
    head0 = lane < RWKV_HEAD_DIM
    bd = (_iota2((LANES, LANES), 0) // RWKV_HEAD_DIM) == (_iota2((LANES, LANES), 1) // RWKV_HEAD_DIM)
    bd_ones = jnp.where(bd, 1.0, 0.0).astype(BF16)

    def head_sum(x):
        x1 = x.astype(BF16)
        x2 = (x - x1.astype(F32)).astype(BF16)
        return (jnp.dot(x1, bd_ones, preferred_element_type=F32)
                + jnp.dot(x2, bd_ones, preferred_element_type=F32))

    tri = _tri_incl_bf16(CHUNK)
    cum = _dot_exact_lhs(tri, logw)
    cume = cum - logw
    cmid = cum[CHUNK // 2 - 1:CHUNK // 2, :]
    cend = cum[CHUNK - 1:CHUNK, :]
    e_in_mid = jnp.exp(cum - cmid)
    e_ex_mid = jnp.exp(cume - cmid)
    e_mid_in = jnp.exp(cmid - cum)
    e_ex = jnp.exp(cume)
    e_in = jnp.exp(cum)
    e_end = jnp.exp(cend - cum)
    e_tot = jnp.exp(cend)

    row = _iota2((CHUNK, CHUNK), 0)
    col = _iota2((CHUNK, CHUNK), 1)
    strict = row > col
    incl = row >= col
    zeros = jnp.zeros((CHUNK, LANES), F32)

    for p in range(RWKV_DIM // LANES):
        sl = slice(p * LANES, (p + 1) * LANES)
        rp, vp, ap = r[:, sl], v[:, sl], a[:, sl]
        kkp = k[:, sl] * kk_ref[:, sl]
        ssq = head_sum(kkp * kkp)
        kkp = kkp / jnp.maximum(jnp.sqrt(ssq), 1e-12)
        kp = k[:, sl] * (1.0 + (ap - 1.0) * ka_ref[:, sl])
        aap = -kkp
        bp = kkp * ap

        a_mid = aap * e_ex_mid[:, sl]
        r_mid = rp * e_in_mid[:, sl]
        b_mid = bp * e_mid_in[:, sl]
        k_mid = kp * e_mid_in[:, sl]
        a_abs = aap * e_ex[:, sl]
        r_abs = rp * e_in[:, sl]
        b_end = bp * e_end[:, sl]
        k_end = kp * e_end[:, sl]

        lhs = jnp.concatenate([jnp.where(head0, a_mid, 0.0), jnp.where(head0, 0.0, a_mid),
                               jnp.where(head0, r_mid, 0.0), jnp.where(head0, 0.0, r_mid)], axis=0)
        rhs = jnp.concatenate([b_mid, k_mid], axis=0)
        prod = _dot_nt(lhs, rhs)

        ui = zeros
        a_eff = zeros
        yi = zeros
        r_eff = r_abs
        for hh in range(2):
            keep = head0 if hh == 0 else jnp.logical_not(head0)
            a_ab = jnp.where(strict, prod[hh * CHUNK:(hh + 1) * CHUNK, :CHUNK], 0.0)
            a_ak = jnp.where(strict, prod[hh * CHUNK:(hh + 1) * CHUNK, CHUNK:], 0.0)
            m_rb = jnp.where(incl, prod[(2 + hh) * CHUNK:(3 + hh) * CHUNK, :CHUNK], 0.0)
            m_rk = jnp.where(incl, prod[(2 + hh) * CHUNK:(3 + hh) * CHUNK, CHUNK:], 0.0)
            t_inv = _neumann_inverse(a_ab)
            vm = jnp.where(keep, vp, 0.0)
            wmat = _dot(t_inv, jnp.concatenate([jnp.where(keep, a_abs, 0.0), _dot(a_ak, vm)], axis=1))
            a_eff_h = wmat[:, :LANES]
            ui_h = wmat[:, LANES:]
            out = _dot(jnp.concatenate([m_rb, m_rk], axis=1),
                       jnp.concatenate([jnp.concatenate([ui_h, a_eff_h], axis=1),
                                        jnp.concatenate([vm, zeros], axis=1)], axis=0))
            yi = yi + out[:, :LANES]
            r_eff = r_eff + out[:, LANES:]
            ui = ui + ui_h
            a_eff = a_eff + a_eff_h

        s0 = state[p]
        zt = _dot_tn(jnp.concatenate([jnp.concatenate([ui, a_eff], axis=1),
                                      jnp.concatenate([vp, zeros], axis=1)], axis=0),
                     jnp.concatenate([b_end, k_end], axis=0))
        h_intra = jnp.where(bd, zt[:LANES, :], 0.0)
        g_corr = jnp.where(bd, zt[LANES:, :], 0.0)
        y = _dot_nt(r_eff, s0) + yi
        state[p] = s0 * e_tot[:, sl] + _dot(s0, g_corr) + h_intra

        mean = head_sum(y) * (1.0 / RWKV_HEAD_DIM)
        dev = y - mean
        var = head_sum(dev * dev) * (1.0 / RWKV_HEAD_DIM)
        yn = dev * lax.rsqrt(var + RWKV_GN_EPS) * lnw_ref[:, sl] + lnb_ref[:, sl]
        bonus = head_sum(rp * kp * rk_ref[:, sl])
        o_ref[:, sl] = ((yn + bonus * vp) * g[:, sl]).astype(o_ref.dtype)


def _rwkv(proj, batch, seq, mu, w0, w2, a0, a2, g2, k_k, k_a, r_k, ln_w, ln_b):
    nc = seq // CHUNK
    n = batch * seq
    d = RWKV_DIM
    row_spec = lambda w, blk: pl.BlockSpec((CHUNK, w), lambda b, c: (b * nc + c, blk))
    full = lambda shape: pl.BlockSpec(shape, lambda b, c: (0,) * len(shape))
    vec = lambda x: x.astype(F32).reshape(1, -1)
    w2p = jnp.concatenate([w2, jnp.zeros((AAA_LORA, d), w2.dtype)], axis=0).astype(BF16)
    a2p = jnp.concatenate([jnp.zeros((DECAY_LORA, d), a2.dtype), a2], axis=0).astype(BF16)
    return pl.pallas_call(
        _rwkv_kernel,
        grid=(batch, nc),
        in_specs=[row_spec(d, 3), row_spec(d, 4), row_spec(d, 5), row_spec(2 * LANES, 24),
                  full((1, d)), full((1, d)), full((1, d)), full((1, 2 * LANES)),
                  full((1, d)), full((LANES, d)), full((1, d)), full((LANES, d)), full((LANES, d)),
                  full((1, d)), full((1, d)), full((1, d)), full((1, d)), full((1, d))],
        out_specs=pl.BlockSpec((CHUNK, d), lambda b, c: (b * nc + c, 0)),
        out_shape=jax.ShapeDtypeStruct((n, d), BF16),
        scratch_shapes=[pltpu.VMEM((TAIL + CHUNK, d), F32), pltpu.VMEM((TAIL + CHUNK, d), F32),
                        pltpu.VMEM((TAIL + CHUNK, d), F32), pltpu.VMEM((TAIL + CHUNK, 2 * LANES), F32),
                        pltpu.VMEM((d // LANES, LANES, LANES), F32)],
        compiler_params=_cparams(("arbitrary", "arbitrary")),
        name="rwkv",
    )(proj, proj, proj, proj,
      vec(mu[:d]), vec(mu[d:2 * d]), vec(mu[2 * d:3 * d]), vec(mu[3 * d:]),
      vec(w0), w2p, vec(a0), a2p, g2.astype(BF16),
      vec(k_k), vec(k_a), vec(r_k), vec(ln_w), vec(ln_b))


def _route(logits):
    lane = _iota2(logits.shape, 1)
    lanef = lane.astype(F32)
    big = float(LANES)

    def first_max(x):
        m = jnp.max(x, axis=-1, keepdims=True)
        idx = jnp.min(jnp.where(x == m, lanef, big), axis=-1, keepdims=True)
        return m, idx

    cl = jnp.where(lane < EXPERT_GROUPS, logits, NEG_BIG)
    cmax, grp = first_max(cl)
    p_group = 1.0 / jnp.sum(jnp.exp(cl - cmax), axis=-1, keepdims=True)
    lo = EXPERT_GROUPS + grp * EXPERTS_PER_GROUP
    fl = jnp.where((lanef >= lo) & (lanef < lo + EXPERTS_PER_GROUP), logits, NEG_BIG)
    m0, i0 = first_max(fl)
    m1, i1 = first_max(jnp.where(lanef == i0, NEG_BIG, fl))
    e1 = jnp.exp(m1 - m0)
    g0 = p_group / (1.0 + e1)
    g1 = p_group * e1 / (1.0 + e1)
    return jnp.where(lane == 0, i0 - EXPERT_GROUPS,
                     jnp.where(lane == 1, i1 - EXPERT_GROUPS,
                               jnp.where(lane == 2, g0, jnp.where(lane == 3, g1, 0.0))))


def _outproj_router_kernel(*refs, n_in):
    ys = refs[:n_in]
    ws = refs[n_in:2 * n_in]
    h_ref, g_ref, wr_ref, br_ref, hnew_ref, hf_ref, route_ref = refs[2 * n_in:]
    acc = h_ref[...]
    for y_ref, w_ref in zip(ys, ws):
        acc = acc + jnp.dot(y_ref[...], w_ref[...], preferred_element_type=F32)
    hnew_ref[...] = acc
    hf = _rms(acc, g_ref[...])
    hf_ref[...] = hf
    route_ref[...] = _route(_dot_f32(hf, wr_ref[...]) + br_ref[...])


def _outproj_router(ys, ws, h, gain, w_router, b_router, tm):
    n, d = h.shape
    n_in = len(ys)
    in_specs = ([pl.BlockSpec((tm, y.shape[1]), lambda i: (i, 0)) for y in ys]
                + [pl.BlockSpec(w.shape, lambda i: (0, 0)) for w in ws]
                + [pl.BlockSpec((tm, d), lambda i: (i, 0)),
                   pl.BlockSpec((1, d), lambda i: (0, 0)),
                   pl.BlockSpec((d, LANES), lambda i: (0, 0)),
                   pl.BlockSpec((1, LANES), lambda i: (0, 0))])
    return pl.pallas_call(
        functools.partial(_outproj_router_kernel, n_in=n_in),
        grid=(n // tm,),
        in_specs=in_specs,
        out_specs=[pl.BlockSpec((tm, d), lambda i: (i, 0)),
                   pl.BlockSpec((tm, d), lambda i: (i, 0)),
                   pl.BlockSpec((tm, LANES), lambda i: (i, 0))],
        out_shape=[jax.ShapeDtypeStruct((n, d), F32), jax.ShapeDtypeStruct((n, d), F32),
                   jax.ShapeDtypeStruct((n, LANES), F32)],
        compiler_params=_cparams(("arbitrary",)),
        name="outproj_router",
    )(*ys, *ws, h, gain.reshape(1, d), w_router, b_router)


def _router_weights(w_coarse, b_coarse, w_fine, b_fine):
    d = w_coarse.shape[0]
    wf = jnp.transpose(w_fine, (1, 0, 2)).reshape(d, N_EXPERTS)
    w = jnp.concatenate([w_coarse, wf], axis=1).astype(F32)
    b = jnp.concatenate([b_coarse, b_fine.reshape(N_EXPERTS)]).astype(F32)
    pad = LANES - w.shape[1]
    return jnp.pad(w, ((0, 0), (0, pad))), jnp.pad(b, (0, pad)).reshape(1, LANES)


def _moe_kernel(bexp_ref, nvalid_ref, src_ref, srcn_ref, dst_ref, hf_hbm, wg_ref, wu_ref, wd_ref, y2_hbm,
                xg, yb, gsem, ssem):
    i = pl.program_id(0)
    nb = pl.num_programs(0)
    slot = lax.rem(i, 2)
    nv = nvalid_ref[i]
    nv_next = jnp.where(i + 1 < nb, nvalid_ref[jnp.minimum(i + 1, nb - 1)], 0)

    def gather_rows(idx_ref, buf_slot, count):
        def body(r, carry):
            pltpu.make_async_copy(hf_hbm.at[pl.ds(idx_ref[0, 0, r], 1), :],
                                  xg.at[buf_slot, pl.ds(r, 1), :], gsem.at[buf_slot]).start()
            return carry
        lax.fori_loop(0, count, body, 0)

    def scatter_rows(buf_slot, count):
        def body(r, carry):
            pltpu.make_async_copy(yb.at[buf_slot, pl.ds(r, 1), :],
                                  y2_hbm.at[pl.ds(dst_ref[0, 0, r], 1), :], ssem.at[buf_slot]).start()
            return carry
        lax.fori_loop(0, count, body, 0)

    def wait_rows(desc, count):
        bulk = pl.multiple_of((count // TAIL) * TAIL, TAIL)

        @pl.when(bulk > 0)
        def _():
            desc(bulk).wait()

        def body(r, carry):
            desc(1).wait()
            return carry
        lax.fori_loop(0, count - bulk, body, 0)

    def wait_gather(buf_slot, count):
        wait_rows(lambda rows: pltpu.make_async_copy(hf_hbm.at[pl.ds(0, rows), :],
                                                     xg.at[buf_slot, pl.ds(0, rows), :], gsem.at[buf_slot]),
                  count)

    def wait_scatter(buf_slot, count):
        wait_rows(lambda rows: pltpu.make_async_copy(yb.at[buf_slot, pl.ds(0, rows), :],
                                                     y2_hbm.at[pl.ds(0, rows), :], ssem.at[buf_slot]),
                  count)

    @pl.when(i == 0)
    def _():
        xg[...] = jnp.zeros_like(xg)
        gather_rows(src_ref, 0, nv)

    @pl.when(nv_next > 0)
    def _():
        gather_rows(srcn_ref, 1 - slot, nv_next)

    @pl.when(nv > 0)
    def _():
        wait_gather(slot, nv)
        x = xg[slot].astype(BF16)
        hg = jnp.dot(x, wg_ref[0], preferred_element_type=F32)
        hu = jnp.dot(x, wu_ref[0], preferred_element_type=F32)
        hb = (_silu(hg) * hu).astype(BF16)

        @pl.when(i >= 2)
        def _():
            wait_scatter(slot, nvalid_ref[jnp.maximum(i - 2, 0)])

        yb[slot] = jnp.dot(hb, wd_ref[0], preferred_element_type=F32)
        scatter_rows(slot, nv)

        @pl.when(nv_next == 0)
        def _():
            wait_scatter(slot, nv)

            @pl.when(i >= 1)
            def _():
                wait_scatter(1 - slot, nvalid_ref[jnp.maximum(i - 1, 0)])


def _moe_experts(hf, route, wg, wu, wd):
    n, d = hf.shape
    a = n * TOP_K
    n_blocks = a // MOE_BLOCK + N_EXPERTS
    slots = n_blocks * MOE_BLOCK
    e_flat = route[:, :TOP_K].astype(jnp.int32).reshape(a)
    onehot = (e_flat[:, None] == jnp.arange(N_EXPERTS, dtype=jnp.int32)[None, :]).astype(jnp.int32)
    csum = jnp.cumsum(onehot, axis=0)
    rank = jnp.sum(onehot * csum, axis=1) - 1
    counts = csum[-1]
    padded = (counts + MOE_BLOCK - 1) // MOE_BLOCK * MOE_BLOCK
    pad_end = jnp.cumsum(padded)
    pad_start = pad_end - padded
    dest = pad_start[e_flat] + rank
    flat = jnp.arange(a, dtype=jnp.int32)
    src_slot = jnp.zeros((slots,), jnp.int32).at[dest].set(flat // TOP_K)
    dst_slot = jnp.zeros((slots,), jnp.int32).at[dest].set(flat)
    block_start = jnp.arange(n_blocks, dtype=jnp.int32) * MOE_BLOCK
    block_expert = jnp.minimum(jnp.searchsorted(pad_end, block_start, side="right"),
                               N_EXPERTS - 1).astype(jnp.int32)
    n_valid = jnp.clip(pad_start[block_expert] + counts[block_expert] - block_start, 0,
                       MOE_BLOCK).astype(jnp.int32)
    src3 = src_slot.reshape(n_blocks, 1, MOE_BLOCK)
    dst3 = dst_slot.reshape(n_blocks, 1, MOE_BLOCK)

    smem_blk = lambda fn: pl.BlockSpec((1, 1, MOE_BLOCK), fn, memory_space=pltpu.SMEM)
    grid_spec = pltpu.PrefetchScalarGridSpec(
        num_scalar_prefetch=2,
        grid=(n_blocks,),
        in_specs=[smem_blk(lambda i, be, nv: (i, 0, 0)),
                  smem_blk(lambda i, be, nv: (jnp.minimum(i + 1, n_blocks - 1), 0, 0)),
                  smem_blk(lambda i, be, nv: (i, 0, 0)),
                  pl.BlockSpec(memory_space=pl.ANY),
                  pl.BlockSpec((1, d, EXPERT_HIDDEN), lambda i, be, nv: (be[i], 0, 0)),
                  pl.BlockSpec((1, d, EXPERT_HIDDEN), lambda i, be, nv: (be[i], 0, 0)),
                  pl.BlockSpec((1, EXPERT_HIDDEN, d), lambda i, be, nv: (be[i], 0, 0))],
        out_specs=pl.BlockSpec(memory_space=pl.ANY),
        scratch_shapes=[pltpu.VMEM((2, MOE_BLOCK, d), F32), pltpu.VMEM((2, MOE_BLOCK, d), F32),
                        pltpu.SemaphoreType.DMA((2,)), pltpu.SemaphoreType.DMA((2,))])
    y2 = pl.pallas_call(
        _moe_kernel,
        grid_spec=grid_spec,
        out_shape=jax.ShapeDtypeStruct((a, d), F32),
        compiler_params=_cparams(("arbitrary",)),
        name="moe_experts",
    )(block_expert, n_valid, src3, src3, dst3, hf, wg, wu, wd)
    return y2.reshape(n, TOP_K * d)


def _swa_kernel(q_ref, kvp_ref, kvc_ref, qg_ref, kg_ref, slope_ref, sink_ref, o_ref):
    nb = pl.program_id(1)
    blk = CHUNK
    qi = _iota2((blk, 2 * blk), 0)
    sj = _iota2((blk, 2 * blk), 1)
    delta = qi + blk - sj
    first_key = jnp.where(nb > 0, 0, blk)
    mask = (delta >= 0) & (delta < WINDOW) & (sj >= first_key)
    deltaf = delta.astype(F32)
    q = q_ref[...]
    kv = jnp.concatenate([kvp_ref[...], kvc_ref[...]], axis=0)
    scale = HEAD_DIM ** -0.5
    outs = []
    for kvh in range(KV_HEADS):
        kh = _rms(kv[:, kvh * HEAD_DIM:(kvh + 1) * HEAD_DIM], kg_ref[...])
        vh = kv[:, KV_DIM + kvh * HEAD_DIM:KV_DIM + (kvh + 1) * HEAD_DIM]
        for gq in range(Q_PER_KV):
            h = kvh * Q_PER_KV + gq
            qh = _rms(q[:, h * HEAD_DIM:(h + 1) * HEAD_DIM], qg_ref[...])
            s = _dot_nt(qh, kh) * scale - slope_ref[:, h:h + 1] * deltaf
            s = jnp.where(mask, s, NEG_BIG)
            sink = sink_ref[:, h:h + 1]
            m = jnp.maximum(jnp.max(s, axis=-1, keepdims=True), sink)
            p = jnp.exp(s - m)
            denom = jnp.sum(p, axis=-1, keepdims=True) + jnp.exp(sink - m)
            outs.append(_dot(p, vh) / denom)
    o_ref[...] = jnp.concatenate(outs, axis=1).astype(o_ref.dtype)


def _swa(proj, batch, seq, q_gain, k_gain, sinks):
    nbk = seq // CHUNK
    n = batch * seq
    slopes = (2.0 ** (-8.0 * jnp.arange(1, ATT_HEADS + 1, dtype=F32) / ATT_HEADS))
    pad = lambda v: jnp.pad(v.astype(F32), (0, LANES - v.shape[0])).reshape(1, LANES)
    full = lambda shape: pl.BlockSpec(shape, lambda b, j: (0, 0))
    kvw = 2 * KV_DIM
    return pl.pallas_call(
        _swa_kernel,
        grid=(batch, nbk),
        in_specs=[pl.BlockSpec((CHUNK, Q_DIM), lambda b, j: (b * nbk + j, 0)),
                  pl.BlockSpec((CHUNK, kvw), lambda b, j: (b * nbk + jnp.maximum(j - 1, 0), Q_DIM // kvw)),
                  pl.BlockSpec((CHUNK, kvw), lambda b, j: (b * nbk + j, Q_DIM // kvw)),
                  full((1, HEAD_DIM)), full((1, HEAD_DIM)), full((1, LANES)), full((1, LANES))],
        out_specs=pl.BlockSpec((CHUNK, Q_DIM), lambda b, j: (b * nbk + j, 0)),
        out_shape=jax.ShapeDtypeStruct((n, Q_DIM), BF16),
        compiler_params=_cparams(("arbitrary", "arbitrary")),
        name="swa",
    )(proj, proj, proj, q_gain.astype(F32).reshape(1, -1), k_gain.astype(F32).reshape(1, -1),
      pad(slopes), pad(sinks))


def _even_in_weight(w):
    rw = SSD_COLS
    cols = jnp.concatenate([w[:, :SSD_INNER + SSD_CONV_DIM], w[:, rw:],
                            w[:, SSD_INNER + SSD_CONV_DIM:SSD_COLS]], axis=1)
    return jnp.pad(cols, ((0, 0), (0, EVEN_COLS_PAD - cols.shape[1]))).astype(BF16)


def kernel(x, ln_mix, ln_ffn, e_w_in, e_w_out, ssd_conv_w, ssd_conv_b, ssd_dt_bias, ssd_a_log, ssd_d, ssd_norm,
           rwkv_mu, rwkv_w0, rwkv_w2, rwkv_a0, rwkv_a2, rwkv_g2, rwkv_k_k, rwkv_k_a, rwkv_r_k, rwkv_ln_w,
           rwkv_ln_b, o_w_in, o_w_out, attn_q_norm, attn_k_norm, attn_sinks, moe_w_coarse, moe_b_coarse,
           moe_w_fine, moe_b_fine, moe_w_gate, moe_w_up, moe_w_down):
    batch, seq, d = x.shape
    n = batch * seq
    tm = min(512, n)
    h = x.reshape(n, d)

    def moe(layer, hf, route):
        return _moe_experts(hf, route, moe_w_gate[layer].astype(BF16), moe_w_up[layer].astype(BF16),
                            moe_w_down[layer].astype(BF16))

    def router_w(layer):
        return _router_weights(moe_w_coarse[layer], moe_b_coarse[layer], moe_w_fine[layer], moe_b_fine[layer])

    proj = _norm_proj(h, ln_mix[0], _even_in_weight(e_w_in[0]), tm, EVEN_COLS_PAD // 3)
    y_ssd = _ssd(proj, batch, seq, ssd_conv_w[0], ssd_conv_b[0], ssd_dt_bias[0], ssd_a_log[0], ssd_d[0],
                 ssd_norm[0])
    y_rwkv = _rwkv(proj, batch, seq, rwkv_mu[0], rwkv_w0[0], rwkv_w2[0], rwkv_a0[0], rwkv_a2[0], rwkv_g2[0],
                   rwkv_k_k[0], rwkv_k_a[0], rwkv_r_k[0].reshape(-1), rwkv_ln_w[0], rwkv_ln_b[0])
    w_out = e_w_out[0].astype(BF16)
    wr, br = router_w(0)
    h, hf, route = _outproj_router([y_ssd, y_rwkv], [w_out[:SSD_INNER], w_out[SSD_INNER:]], h, ln_ffn[0],
                                   wr, br, tm)
    y2 = moe(0, hf, route)

    h, proj = _combine_norm_proj(h, y2, route, ln_mix[1], o_w_in[0].astype(BF16), tm)
    att = _swa(proj, batch, seq, attn_q_norm[0], attn_k_norm[0], attn_sinks[0])
    wr, br = router_w(1)
    h, hf, route = _outproj_router([att], [o_w_out[0].astype(BF16)], h, ln_ffn[1], wr, br, tm)
    y2 = moe(1, hf, route)
    out = _combine(h, y2, route, tm)
    return out.reshape(batch, seq, d)
```

```python
import functools
import math

import jax
import jax.numpy as jnp
from jax import lax
from jax.experimental import pallas as pl
from jax.experimental.pallas import tpu as pltpu

F32 = jnp.float32
BF16 = jnp.bfloat16

D_MODEL = 1024
SSD_HEADS = 16
SSD_HEAD_DIM = 64
SSD_INNER = 1024
SSD_GROUPS = 4
SSD_STATE = 128
SSD_CONV = 4
SSD_CONV_DIM = 2048
SSD_COLS = 3088
RWKV_HEADS = 16
RWKV_HEAD_DIM = 64
RWKV_DIM = 1024
DECAY_LORA = 64
AAA_LORA = 64
GATE_LORA = 128
RWKV_GN_EPS = 64e-5
ATT_HEADS = 16
KV_HEADS = 4
Q_PER_KV = 4
HEAD_DIM = 64
Q_DIM = 1024
KV_DIM = 256
WINDOW = 128
EXPERT_GROUPS = 4
EXPERTS_PER_GROUP = 8
N_EXPERTS = 32
TOP_K = 2
EXPERT_HIDDEN = 512
MOE_BLOCK = 256
NORM_EPS = 1e-6

LANES = 128
CHUNK = 128
TAIL = 8
DMA_UNROLL = 8
EVEN_COLS_PAD = 6528
VMEM_LIMIT = 56 * 1024 * 1024
NEG_BIG = -1e30


def _cparams(sem):
    return pltpu.CompilerParams(dimension_semantics=sem, vmem_limit_bytes=VMEM_LIMIT)


def _dot(a, b):
    return jnp.dot(a.astype(BF16), b.astype(BF16), preferred_element_type=F32)


def _dot_nt(a, b):
    return lax.dot_general(a.astype(BF16), b.astype(BF16), (((1,), (1,)), ((), ())),
                           preferred_element_type=F32)


def _dot_tn(a, b):
    return lax.dot_general(a.astype(BF16), b.astype(BF16), (((0,), (0,)), ((), ())),
                           preferred_element_type=F32)


def _split3(x):
    x1 = x.astype(BF16)
    r1 = x - x1.astype(F32)
    x2 = r1.astype(BF16)
    x3 = (r1 - x2.astype(F32)).astype(BF16)
    return x1, x2, x3


def _dot_exact_lhs(m_bf16, x):
    x1, x2, x3 = _split3(x)
    f = lambda p: jnp.dot(m_bf16, p, preferred_element_type=F32)
    return f(x1) + f(x2) + f(x3)


def _dot_exact_rhs(x, m_bf16):
    x1, x2, x3 = _split3(x)
    f = lambda p: jnp.dot(p, m_bf16, preferred_element_type=F32)
    return f(x1) + f(x2) + f(x3)


def _dot_f32(x, w):
    x1, x2, x3 = _split3(x)
    w1, w2, w3 = _split3(w)
    f = lambda p, q: jnp.dot(p, q, preferred_element_type=F32)
    return f(x1, w1) + (f(x1, w2) + f(x2, w1)) + (f(x2, w2) + f(x1, w3) + f(x3, w1))


def _sigmoid(x):
    return 1.0 / (1.0 + jnp.exp(-x))


def _silu(x):
    return x * _sigmoid(x)


def _softplus(x):
    return jnp.maximum(x, 0.0) + jnp.log(1.0 + jnp.exp(-jnp.abs(x)))


def _rms(x, gain):
    return x * lax.rsqrt(jnp.mean(x * x, axis=-1, keepdims=True) + NORM_EPS) * gain


def _iota2(shape, dim):
    return lax.broadcasted_iota(jnp.int32, shape, dim)


def _tri_incl_bf16(n):
    return jnp.where(_iota2((n, n), 0) >= _iota2((n, n), 1), 1.0, 0.0).astype(BF16)


def _norm_proj_kernel(h_ref, g_ref, w_ref, o_ref):
    o_ref[...] = _dot(_rms(h_ref[...], g_ref[...]), w_ref[...])


def _norm_proj(h, gain, w_bf16, tm, tn):
    n, d = h.shape
    c = w_bf16.shape[1]
    return pl.pallas_call(
        _norm_proj_kernel,
        grid=(c // tn, n // tm),
        in_specs=[pl.BlockSpec((tm, d), lambda j, i: (i, 0)),
                  pl.BlockSpec((1, d), lambda j, i: (0, 0)),
                  pl.BlockSpec((d, tn), lambda j, i: (0, j))],
        out_specs=pl.BlockSpec((tm, tn), lambda j, i: (i, j)),
        out_shape=jax.ShapeDtypeStruct((n, c), F32),
        compiler_params=_cparams(("arbitrary", "arbitrary")),
        name="norm_proj",
    )(h, gain.reshape(1, d), w_bf16)


def _gathered_combine(h_ref, route_ref, dest_ref, destn_ref, ys_hbm, yg, gsem):
    i = pl.program_id(0)
    nt = pl.num_programs(0)
    slot = lax.rem(i, 2)
    tm = h_ref.shape[0]

    def fetch(idx_ref, buf_slot):
        def body(g, carry):
            for u in range(DMA_UNROLL):
                t = g * DMA_UNROLL + u
                for c in range(TOP_K):
                    pltpu.make_async_copy(ys_hbm.at[pl.ds(idx_ref[0, 0, TOP_K * t + c], 1), :],
                                          yg.at[buf_slot, c, pl.ds(t, 1), :], gsem.at[buf_slot]).start()
            return carry
        lax.fori_loop(0, tm // DMA_UNROLL, body, 0)

    @pl.when(i == 0)
    def _():
        fetch(dest_ref, 0)

    @pl.when(i + 1 < nt)
    def _():
        fetch(destn_ref, 1 - slot)

    for c in range(TOP_K):
        pltpu.make_async_copy(ys_hbm.at[pl.ds(0, tm), :], yg.at[slot, c], gsem.at[slot]).wait()
    r = route_ref[...]
    return h_ref[...] + r[:, 2:3] * yg[slot, 0] + r[:, 3:4] * yg[slot, 1]


def _combine_norm_proj_kernel(h_ref, route_ref, dest_ref, destn_ref, g_ref, w_ref, ys_hbm, hnew_ref, o_ref,
                              yg, gsem):
    x = _gathered_combine(h_ref, route_ref, dest_ref, destn_ref, ys_hbm, yg, gsem)
    hnew_ref[...] = x
    o_ref[...] = _dot(_rms(x, g_ref[...]), w_ref[...])


def _combine_kernel(h_ref, route_ref, dest_ref, destn_ref, ys_hbm, o_ref, yg, gsem):
    o_ref[...] = _gathered_combine(h_ref, route_ref, dest_ref, destn_ref, ys_hbm, yg, gsem)


def _combine_specs(n, d, tm):
    nt = n // tm
    smem = lambda fn: pl.BlockSpec((1, 1, TOP_K * tm), fn, memory_space=pltpu.SMEM)
    in_specs = [pl.BlockSpec((tm, d), lambda i: (i, 0)),
                pl.BlockSpec((tm, LANES), lambda i: (i, 0)),
                smem(lambda i: (i, 0, 0)),
                smem(lambda i: (jnp.minimum(i + 1, nt - 1), 0, 0))]
    scratch = [pltpu.VMEM((2, TOP_K, tm, d), F32), pltpu.SemaphoreType.DMA((2,))]
    return in_specs, scratch


def _combine_norm_proj(h, ys, route, dest3, gain, w_bf16, tm):
    n, d = h.shape
    c = w_bf16.shape[1]
    in_specs, scratch = _combine_specs(n, d, tm)
    return pl.pallas_call(
        _combine_norm_proj_kernel,
        grid=(n // tm,),
        in_specs=in_specs + [pl.BlockSpec((1, d), lambda i: (0, 0)),
                             pl.BlockSpec((d, c), lambda i: (0, 0)),
                             pl.BlockSpec(memory_space=pl.ANY)],
        out_specs=[pl.BlockSpec((tm, d), lambda i: (i, 0)),
                   pl.BlockSpec((tm, c), lambda i: (i, 0))],
        out_shape=[jax.ShapeDtypeStruct((n, d), F32), jax.ShapeDtypeStruct((n, c), F32)],
        scratch_shapes=scratch,
        compiler_params=_cparams(("arbitrary",)),
        name="combine_norm_proj",
    )(h, route, dest3, dest3, gain.reshape(1, d), w_bf16, ys)


def _combine(h, ys, route, dest3, tm):
    n, d = h.shape
    in_specs, scratch = _combine_specs(n, d, tm)
    return pl.pallas_call(
        _combine_kernel,
        grid=(n // tm,),
        in_specs=in_specs + [pl.BlockSpec(memory_space=pl.ANY)],
        out_specs=pl.BlockSpec((tm, d), lambda i: (i, 0)),
        out_shape=jax.ShapeDtypeStruct((n, d), F32),
        scratch_shapes=scratch,
        compiler_params=_cparams(("arbitrary",)),
        name="combine",
    )(h, route, dest3, dest3, ys)


def _shifted_taps(buf, u_ref, n_taps):
    buf[TAIL:TAIL + CHUNK, :] = u_ref[...]
    taps = [buf[TAIL - j:TAIL - j + CHUNK, :] for j in range(n_taps)]
    return taps


def _carry_tail(buf):
    buf[0:TAIL, :] = buf[CHUNK:CHUNK + TAIL, :]


def _ssd_kernel(z_ref, x_ref, bc_ref, dt_ref, cwx_ref, cbx_ref, cwbc_ref, cbbc_ref, dtb_ref, alog_ref,
                dskip_ref, nw_ref, hexp_ref, o_ref, xbuf, bcbuf, state):
    c = pl.program_id(1)

    @pl.when(c == 0)
    def _():
        xbuf[0:TAIL, :] = jnp.zeros((TAIL, SSD_INNER), F32)
        bcbuf[0:TAIL, :] = jnp.zeros((TAIL, SSD_INNER), F32)
        state[...] = jnp.zeros_like(state)

    def conv(buf, u_ref, w_ref, b_ref):
        taps = _shifted_taps(buf, u_ref, SSD_CONV)
        acc = b_ref[...] + taps[0] * w_ref[3:4, :]
        for j in range(1, SSD_CONV):
            acc = acc + taps[j] * w_ref[3 - j:4 - j, :]
        _carry_tail(buf)
        return _silu(acc)

    xs = conv(xbuf, x_ref, cwx_ref, cbx_ref)
    bc = conv(bcbuf, bc_ref, cwbc_ref, cbbc_ref)

    lane = _iota2((CHUNK, LANES), 1)
    dt = _softplus(dt_ref[...] + dtb_ref[...])
    adt = jnp.where(lane < SSD_HEADS, -jnp.exp(alog_ref[...]) * dt, 0.0)
    tri = _tri_incl_bf16(CHUNK)
    cum = _dot_exact_lhs(tri, adt)
    cum_t = cum.T
    hexp = hexp_ref[...]
    cum_full = _dot_exact_rhs(cum, hexp)
    dt_full = _dot_exact_rhs(dt, hexp)
    tot_full = cum_full[CHUNK - 1:CHUNK, :]
    xd = xs * dt_full
    xds = xd * jnp.exp(tot_full - cum_full)
    eac = jnp.exp(cum_full)

    row = _iota2((CHUNK, CHUNK), 0)
    col = _iota2((CHUNK, CHUNK), 1)
    causal = row >= col
    lane_lo = lane < SSD_HEAD_DIM

    gw = SSD_INNER // SSD_GROUPS
    y_parts = []
    for g in range(SSD_GROUPS):
        bg = bc[:, g * SSD_STATE:(g + 1) * SSD_STATE]
        cg = bc[:, (SSD_GROUPS + g) * SSD_STATE:(SSD_GROUPS + g + 1) * SSD_STATE]
        cb = _dot_nt(cg, bg)
        s_prev = state[:, g * gw:(g + 1) * gw]
        y_off = _dot(cg, s_prev) * eac[:, g * gw:(g + 1) * gw]
        s_new = _dot(bg.T, xds[:, g * gw:(g + 1) * gw])
        state[:, g * gw:(g + 1) * gw] = jnp.exp(tot_full[:, g * gw:(g + 1) * gw]) * s_prev + s_new
        for pr in range(2):
            lo = g * gw + pr * LANES
            xd_pair = xd[:, lo:lo + LANES]
            yd = jnp.zeros((CHUNK, LANES), F32)
            for k in range(2):
                h = (lo // SSD_HEAD_DIM) + k
                diff = cum[:, h:h + 1] - cum_t[h:h + 1, :]
                decay = jnp.exp(jnp.where(causal, diff, NEG_BIG))
                keep = lane_lo if k == 0 else jnp.logical_not(lane_lo)
                yd = yd + _dot(cb * decay, jnp.where(keep, xd_pair, 0.0))
            y_parts.append(yd + y_off[:, pr * LANES:(pr + 1) * LANES])
    y = jnp.concatenate(y_parts, axis=1) + dskip_ref[...] * xs
    y = y * _silu(z_ref[...])
    outs = []
    for g in range(SSD_GROUPS):
        yg = y[:, g * gw:(g + 1) * gw]
        outs.append(yg * lax.rsqrt(jnp.mean(yg * yg, axis=-1, keepdims=True) + 1e-5))
    o_ref[...] = (jnp.concatenate(outs, axis=1) * nw_ref[...]).astype(o_ref.dtype)


def _head_expand(n_heads, head_dim):
    h = jnp.arange(LANES)[:, None]
    l = jnp.arange(n_heads * head_dim)[None, :]
    return (l // head_dim == h).astype(BF16)


def _ssd(proj, batch, seq, conv_w, conv_b, dt_bias, a_log, d_skip, norm_w):
    nc = seq // CHUNK
    n = batch * seq
    pad16 = lambda v: jnp.pad(v.astype(F32), (0, LANES - v.shape[0])).reshape(1, LANES)
    row_spec = lambda w, blk: pl.BlockSpec((CHUNK, w), lambda b, c: (b * nc + c, blk))
    full = lambda shape: pl.BlockSpec(shape, lambda b, c: (0, 0))
    return pl.pallas_call(
        _ssd_kernel,
        grid=(batch, nc),
        in_specs=[row_spec(SSD_INNER, 0),
                  row_spec(SSD_INNER, 1),
                  row_spec(SSD_INNER, 2),
                  row_spec(LANES, 50),
                  full((SSD_CONV, SSD_INNER)), full((1, SSD_INNER)),
                  full((SSD_CONV, SSD_INNER)), full((1, SSD_INNER)),
                  full((1, LANES)), full((1, LANES)),
                  full((1, SSD_INNER)), full((1, SSD_INNER)),
                  full((LANES, SSD_INNER))],
        out_specs=pl.BlockSpec((CHUNK, SSD_INNER), lambda b, c: (b * nc + c, 0)),
        out_shape=jax.ShapeDtypeStruct((n, SSD_INNER), BF16),
        scratch_shapes=[pltpu.VMEM((TAIL + CHUNK, SSD_INNER), F32),
                        pltpu.VMEM((TAIL + CHUNK, SSD_INNER), F32),
                        pltpu.VMEM((SSD_STATE, SSD_INNER), F32)],
        compiler_params=_cparams(("arbitrary", "arbitrary")),
        name="ssd",
    )(proj, proj, proj, proj,
      conv_w[:, :SSD_INNER], conv_b[:SSD_INNER].reshape(1, -1),
      conv_w[:, SSD_INNER:], conv_b[SSD_INNER:].reshape(1, -1),
      pad16(dt_bias), pad16(a_log),
      jnp.repeat(d_skip.astype(F32), SSD_HEAD_DIM).reshape(1, -1), norm_w.reshape(1, -1),
      _head_expand(SSD_HEADS, SSD_HEAD_DIM))


def _rwkv_kernel(r_ref, k_ref, v_ref, lo_ref, mur_ref, muk_ref, muv_ref, mulo_ref, w0_ref, w2_ref, a0_ref,
                 a2_ref, g2_ref, kk_ref, ka_ref, rk_ref, lnw_ref, lnb_ref, o_ref,
                 rbuf, kbuf, vbuf, lobuf, state):
    c = pl.program_id(1)

    @pl.when(c == 0)
    def _():
        rbuf[0:TAIL, :] = jnp.zeros((TAIL, RWKV_DIM), F32)
        kbuf[0:TAIL, :] = jnp.zeros((TAIL, RWKV_DIM), F32)
        vbuf[0:TAIL, :] = jnp.zeros((TAIL, RWKV_DIM), F32)
        lobuf[0:TAIL, :] = jnp.zeros((TAIL, 2 * LANES), F32)
        state[...] = jnp.zeros_like(state)

    def shift(buf, u_ref, mu_ref):
        cur, prev = _shifted_taps(buf, u_ref, 2)
        _carry_tail(buf)
        return cur + mu_ref[...] * (prev - cur)

    r = shift(rbuf, r_ref, mur_ref)
    k = shift(kbuf, k_ref, muk_ref)
    v = shift(vbuf, v_ref, muv_ref)
    lo = shift(lobuf, lo_ref, mulo_ref)
    wa = lo[:, :LANES]
    w = -_softplus(-(w0_ref[...] + _dot(jnp.tanh(wa), w2_ref[...]))) - 0.5
    logw = -jnp.exp(w)
    a = _sigmoid(a0_ref[...] + _dot(wa, a2_ref[...]))
    g = _dot(_sigmoid(lo[:, LANES:]), g2_ref[...])

    lane = _iota2((CHUNK, LANES), 1)
    head0 = lane < RWKV_HEAD_DIM
    bd = (_iota2((LANES, LANES), 0) // RWKV_HEAD_DIM) == (_iota2((LANES, LANES), 1) // RWKV_HEAD_DIM)
    bd_ones = jnp.where(bd, 1.0, 0.0).astype(BF16)

    def head_sum(x):
        x1 = x.astype(BF16)
        x2 = (x - x1.astype(F32)).astype(BF16)
        return (jnp.dot(x1, bd_ones, preferred_element_type=F32)
                + jnp.dot(x2, bd_ones, preferred_element_type=F32))

    tri = _tri_incl_bf16(CHUNK)
    cum = _dot_exact_lhs(tri, logw)
    cume = cum - logw
    cmid = cum[CHUNK // 2 - 1:CHUNK // 2, :]
    cend = cum[CHUNK - 1:CHUNK, :]
    e_in_mid = jnp.exp(cum - cmid)
    e_ex_mid = jnp.exp(cume - cmid)
    e_mid_in = jnp.exp(cmid - cum)
    e_ex = jnp.exp(cume)
    e_in = jnp.exp(cum)
    e_end = jnp.exp(cend - cum)
    e_tot = jnp.exp(cend)

    row = _iota2((CHUNK, CHUNK), 0)
    col = _iota2((CHUNK, CHUNK), 1)
    strict = row > col
    incl = row >= col
    zeros = jnp.zeros((CHUNK, LANES), F32)

    n_blocks = RWKV_DIM // LANES
    halves = (head0, jnp.logical_not(head0))
    blk = []
    for p in range(n_blocks):
        sl = slice(p * LANES, (p + 1) * LANES)
        rp, vp, ap = r[:, sl], v[:, sl], a[:, sl]
        kkp = k[:, sl] * kk_ref[:, sl]
        kkp = kkp / jnp.maximum(jnp.sqrt(head_sum(kkp * kkp)), 1e-12)
        kp = k[:, sl] * (1.0 + (ap - 1.0) * ka_ref[:, sl])
        aap = -kkp
        bp = kkp * ap
        a_mid = aap * e_ex_mid[:, sl]
        r_mid = rp * e_in_mid[:, sl]
        lhs = jnp.concatenate([jnp.where(head0, a_mid, 0.0), jnp.where(head0, 0.0, a_mid),
                               jnp.where(head0, r_mid, 0.0), jnp.where(head0, 0.0, r_mid)], axis=0)
        rhs = jnp.concatenate([bp * e_mid_in[:, sl], kp * e_mid_in[:, sl]], axis=0)
        blk.append(dict(sl=sl, rp=rp, vp=vp, kp=kp, a_abs=aap * e_ex[:, sl], r_abs=rp * e_in[:, sl],
                        b_end=bp * e_end[:, sl], k_end=kp * e_end[:, sl], prod=_dot_nt(lhs, rhs)))

    heads = []
    for p in range(n_blocks):
        prod = blk[p]["prod"]
        for hh in range(2):
            heads.append(dict(
                p=p, keep=halves[hh], vm=jnp.where(halves[hh], blk[p]["vp"], 0.0),
                a_ab=jnp.where(strict, prod[hh * CHUNK:(hh + 1) * CHUNK, :CHUNK], 0.0),
                a_ak=jnp.where(strict, prod[hh * CHUNK:(hh + 1) * CHUNK, CHUNK:], 0.0),
                m_rb=jnp.where(incl, prod[(2 + hh) * CHUNK:(3 + hh) * CHUNK, :CHUNK], 0.0),
                m_rk=jnp.where(incl, prod[(2 + hh) * CHUNK:(3 + hh) * CHUNK, CHUNK:], 0.0)))

    eye = jnp.where(row == col, 1.0, 0.0)
    xs = [hd["a_ab"].astype(BF16) for hd in heads]
    ts = [eye + hd["a_ab"] for hd in heads]
    avs = [_dot(hd["a_ak"], hd["vm"]) for hd in heads]
    for _ in range(int(math.log2(CHUNK)) - 1):
        xs = [jnp.dot(x, x, preferred_element_type=F32).astype(BF16) for x in xs]
        ts = [t + jnp.dot(t.astype(BF16), x, preferred_element_type=F32) for t, x in zip(ts, xs)]

    wmats = [_dot(t, jnp.concatenate([jnp.where(hd["keep"], blk[hd["p"]]["a_abs"], 0.0), av], axis=1))
             for t, hd, av in zip(ts, heads, avs)]
    outs = [_dot(jnp.concatenate([hd["m_rb"], hd["m_rk"]], axis=1),
                 jnp.concatenate([jnp.concatenate([wm[:, LANES:], wm[:, :LANES]], axis=1),
                                  jnp.concatenate([hd["vm"], zeros], axis=1)], axis=0))
            for wm, hd in zip(wmats, heads)]

    zts = []
    for p in range(n_blocks):
        w0h, w1h = wmats[2 * p], wmats[2 * p + 1]
        ui = w0h[:, LANES:] + w1h[:, LANES:]
        a_eff = w0h[:, :LANES] + w1h[:, :LANES]
        zts.append(_dot_tn(jnp.concatenate([jnp.concatenate([ui, a_eff], axis=1),
                                            jnp.concatenate([blk[p]["vp"], zeros], axis=1)], axis=0),
                           jnp.concatenate([blk[p]["b_end"], blk[p]["k_end"]], axis=0)))

    ys = []
    for p in range(n_blocks):
        s0 = state[p]
        o0, o1 = outs[2 * p], outs[2 * p + 1]
        r_eff = blk[p]["r_abs"] + o0[:, LANES:] + o1[:, LANES:]
        ys.append(_dot_nt(r_eff, s0) + o0[:, :LANES] + o1[:, :LANES])
        h_intra = jnp.where(bd, zts[p][:LANES, :], 0.0)
        g_corr = jnp.where(bd, zts[p][LANES:, :], 0.0)
        state[p] = s0 * e_tot[:, blk[p]["sl"]] + _dot(s0, g_corr) + h_intra

    means = [head_sum(y) * (1.0 / RWKV_HEAD_DIM) for y in ys]
    devs = [y - m for y, m in zip(ys, means)]
    vars_ = [head_sum(dv * dv) * (1.0 / RWKV_HEAD_DIM) for dv in devs]
    for p in range(n_blocks):
        sl = blk[p]["sl"]
        yn = devs[p] * lax.rsqrt(vars_[p] + RWKV_GN_EPS) * lnw_ref[:, sl] + lnb_ref[:, sl]
        bonus = head_sum(blk[p]["rp"] * blk[p]["kp"] * rk_ref[:, sl])
        o_ref[:, sl] = ((yn + bonus * blk[p]["vp"]) * g[:, sl]).astype(o_ref.dtype)


def _rwkv(proj, batch, seq, mu, w0, w2, a0, a2, g2, k_k, k_a, r_k, ln_w, ln_b):
    nc = seq // CHUNK
    n = batch * seq
    d = RWKV_DIM
    row_spec = lambda w, blk: pl.BlockSpec((CHUNK, w), lambda b, c: (b * nc + c, blk))
    full = lambda shape: pl.BlockSpec(shape, lambda b, c: (0,) * len(shape))
    vec = lambda x: x.astype(F32).reshape(1, -1)
    w2p = jnp.concatenate([w2, jnp.zeros((AAA_LORA, d), w2.dtype)], axis=0).astype(BF16)
    a2p = jnp.concatenate([jnp.zeros((DECAY_LORA, d), a2.dtype), a2], axis=0).astype(BF16)
    return pl.pallas_call(
        _rwkv_kernel,
        grid=(batch, nc),
        in_specs=[row_spec(d, 3), row_spec(d, 4), row_spec(d, 5), row_spec(2 * LANES, 24),
                  full((1, d)), full((1, d)), full((1, d)), full((1, 2 * LANES)),
                  full((1, d)), full((LANES, d)), full((1, d)), full((LANES, d)), full((LANES, d)),
                  full((1, d)), full((1, d)), full((1, d)), full((1, d)), full((1, d))],
        out_specs=pl.BlockSpec((CHUNK, d), lambda b, c: (b * nc + c, 0)),
        out_shape=jax.ShapeDtypeStruct((n, d), BF16),
        scratch_shapes=[pltpu.VMEM((TAIL + CHUNK, d), F32), pltpu.VMEM((TAIL + CHUNK, d), F32),
                        pltpu.VMEM((TAIL + CHUNK, d), F32), pltpu.VMEM((TAIL + CHUNK, 2 * LANES), F32),
                        pltpu.VMEM((d // LANES, LANES, LANES), F32)],
        compiler_params=_cparams(("arbitrary", "arbitrary")),
        name="rwkv",
    )(proj, proj, proj, proj,
      vec(mu[:d]), vec(mu[d:2 * d]), vec(mu[2 * d:3 * d]), vec(mu[3 * d:]),
      vec(w0), w2p, vec(a0), a2p, g2.astype(BF16),
      vec(k_k), vec(k_a), vec(r_k), vec(ln_w), vec(ln_b))


def _route(logits):
    lane = _iota2(logits.shape, 1)
    lanef = lane.astype(F32)
    big = float(LANES)

    def first_max(x):
        m = jnp.max(x, axis=-1, keepdims=True)
        idx = jnp.min(jnp.where(x == m, lanef, big), axis=-1, keepdims=True)
        return m, idx

    cl = jnp.where(lane < EXPERT_GROUPS, logits, NEG_BIG)
    cmax, grp = first_max(cl)
    p_group = 1.0 / jnp.sum(jnp.exp(cl - cmax), axis=-1, keepdims=True)
    lo = EXPERT_GROUPS + grp * EXPERTS_PER_GROUP
    fl = jnp.where((lanef >= lo) & (lanef < lo + EXPERTS_PER_GROUP), logits, NEG_BIG)
    m0, i0 = first_max(fl)
    m1, i1 = first_max(jnp.where(lanef == i0, NEG_BIG, fl))
    e1 = jnp.exp(m1 - m0)
    g0 = p_group / (1.0 + e1)
    g1 = p_group * e1 / (1.0 + e1)
    return jnp.where(lane == 0, i0 - EXPERT_GROUPS,
                     jnp.where(lane == 1, i1 - EXPERT_GROUPS,
                               jnp.where(lane == 2, g0, jnp.where(lane == 3, g1, 0.0))))


def _outproj_router_kernel(*refs, n_in):
    ys = refs[:n_in]
    ws = refs[n_in:2 * n_in]
    h_ref, g_ref, wr_ref, br_ref, hnew_ref, hf_ref, route_ref = refs[2 * n_in:]
    acc = h_ref[...]
    for y_ref, w_ref in zip(ys, ws):
        acc = acc + jnp.dot(y_ref[...], w_ref[...], preferred_element_type=F32)
    hnew_ref[...] = acc
    hf = _rms(acc, g_ref[...])
    hf_ref[...] = hf
    route_ref[...] = _route(_dot_f32(hf, wr_ref[...]) + br_ref[...])


def _outproj_router(ys, ws, h, gain, w_router, b_router, tm):
    n, d = h.shape
    n_in = len(ys)
    in_specs = ([pl.BlockSpec((tm, y.shape[1]), lambda i: (i, 0)) for y in ys]
                + [pl.BlockSpec(w.shape, lambda i: (0, 0)) for w in ws]
                + [pl.BlockSpec((tm, d), lambda i: (i, 0)),
                   pl.BlockSpec((1, d), lambda i: (0, 0)),
                   pl.BlockSpec((d, LANES), lambda i: (0, 0)),
                   pl.BlockSpec((1, LANES), lambda i: (0, 0))])
    return pl.pallas_call(
        functools.partial(_outproj_router_kernel, n_in=n_in),
        grid=(n // tm,),
        in_specs=in_specs,
        out_specs=[pl.BlockSpec((tm, d), lambda i: (i, 0)),
                   pl.BlockSpec((tm, d), lambda i: (i, 0)),
                   pl.BlockSpec((tm, LANES), lambda i: (i, 0))],
        out_shape=[jax.ShapeDtypeStruct((n, d), F32), jax.ShapeDtypeStruct((n, d), F32),
                   jax.ShapeDtypeStruct((n, LANES), F32)],
        compiler_params=_cparams(("arbitrary",)),
        name="outproj_router",
    )(*ys, *ws, h, gain.reshape(1, d), w_router, b_router)


def _router_weights(w_coarse, b_coarse, w_fine, b_fine):
    d = w_coarse.shape[0]
    wf = jnp.transpose(w_fine, (1, 0, 2)).reshape(d, N_EXPERTS)
    w = jnp.concatenate([w_coarse, wf], axis=1).astype(F32)
    b = jnp.concatenate([b_coarse, b_fine.reshape(N_EXPERTS)]).astype(F32)
    pad = LANES - w.shape[1]
    return jnp.pad(w, ((0, 0), (0, pad))), jnp.pad(b, (0, pad)).reshape(1, LANES)


def _moe_plan(route, tm):
    n = route.shape[0]
    a = n * TOP_K
    n_blocks = a // MOE_BLOCK + N_EXPERTS
    e_flat = route[:, :TOP_K].astype(jnp.int32).reshape(a)
    seg = MOE_BLOCK
    onehot = (e_flat[:, None] == jnp.arange(N_EXPERTS, dtype=jnp.int32)[None, :]).astype(F32)
    onehot = onehot.reshape(a // seg, seg, N_EXPERTS)
    tri = jnp.tril(jnp.ones((seg, seg), F32))
    within = jnp.einsum("ij,bjk->bik", tri, onehot)
    tot = within[:, -1, :]
    offs = jnp.cumsum(tot, axis=0) - tot
    rank = (jnp.sum(onehot * (within + offs[:, None, :]), axis=-1) - 1.0).astype(jnp.int32).reshape(a)
    counts = (offs[-1] + tot[-1]).astype(jnp.int32)
    padded = (counts + MOE_BLOCK - 1) // MOE_BLOCK * MOE_BLOCK
    pad_end = jnp.cumsum(padded)
    pad_start = pad_end - padded
    start_of = jnp.sum(jnp.where(e_flat[:, None] == jnp.arange(N_EXPERTS, dtype=jnp.int32)[None, :],
                                 pad_start[None, :], 0), axis=1)
    dest = (start_of + rank).astype(jnp.int32)
    block_start = jnp.arange(n_blocks, dtype=jnp.int32) * MOE_BLOCK
    block_expert = jnp.minimum(jnp.sum((pad_end[None, :] <= block_start[:, None]).astype(jnp.int32), axis=1),
                               N_EXPERTS - 1).astype(jnp.int32)
    n_used = (pad_end[-1] // MOE_BLOCK).astype(jnp.int32).reshape(1)
    pad_lo = (pad_start + counts).astype(jnp.int32)
    return dict(dest3=dest.reshape(n // tm, 1, TOP_K * tm), block_expert=block_expert, n_used=n_used,
                pad_lo=pad_lo, pad_hi=pad_end.astype(jnp.int32), n_blocks=n_blocks)


def _dispatch_kernel(padlo_ref, padhi_ref, nused_ref, hf_ref, dest_ref, xs_hbm, zblk, sem, zsem):
    i = pl.program_id(0)
    tm = hf_ref.shape[0]

    @pl.when(i == 0)
    def _():
        zblk[...] = jnp.zeros_like(zblk)

        def per_expert(e, carry):
            def zero_row(rw, c2):
                pltpu.make_async_copy(zblk.at[pl.ds(0, 1), :], xs_hbm.at[pl.ds(rw, 1), :], zsem).start()
                return c2
            lax.fori_loop(padlo_ref[e], padhi_ref[e], zero_row, 0)

            def wait_row(rw, c2):
                pltpu.make_async_copy(zblk.at[pl.ds(0, 1), :], xs_hbm.at[pl.ds(rw, 1), :], zsem).wait()
                return c2
            lax.fori_loop(padlo_ref[e], padhi_ref[e], wait_row, 0)
            return carry
        lax.fori_loop(0, N_EXPERTS, per_expert, 0)

        n_blocks = xs_hbm.shape[0] // MOE_BLOCK

        def block_copy(b):
            return pltpu.make_async_copy(
                zblk, xs_hbm.at[pl.ds(pl.multiple_of(b * MOE_BLOCK, MOE_BLOCK), MOE_BLOCK), :], zsem)

        def zero_block(b, carry):
            block_copy(b).start()
            return carry
        lax.fori_loop(nused_ref[0], n_blocks, zero_block, 0)

        def wait_block(b, carry):
            block_copy(b).wait()
            return carry
        lax.fori_loop(nused_ref[0], n_blocks, wait_block, 0)

    def body(g, carry):
        for u in range(DMA_UNROLL):
            t = g * DMA_UNROLL + u
            for c in range(TOP_K):
                pltpu.make_async_copy(hf_ref.at[pl.ds(t, 1), :],
                                      xs_hbm.at[pl.ds(dest_ref[0, 0, TOP_K * t + c], 1), :], sem).start()
        return carry
    lax.fori_loop(0, tm // DMA_UNROLL, body, 0)
    for c in range(TOP_K):
        pltpu.make_async_copy(hf_ref, xs_hbm.at[pl.ds(0, tm), :], sem).wait()


def _dispatch(hf, plan, tm):
    n, d = hf.shape
    slots = plan["n_blocks"] * MOE_BLOCK
    grid_spec = pltpu.PrefetchScalarGridSpec(
        num_scalar_prefetch=3,
        grid=(n // tm,),
        in_specs=[pl.BlockSpec((tm, d), lambda i, lo, hi, nu: (i, 0)),
                  pl.BlockSpec((1, 1, TOP_K * tm), lambda i, lo, hi, nu: (i, 0, 0), memory_space=pltpu.SMEM)],
        out_specs=pl.BlockSpec(memory_space=pl.ANY),
        scratch_shapes=[pltpu.VMEM((MOE_BLOCK, d), F32), pltpu.SemaphoreType.DMA(()),
                        pltpu.SemaphoreType.DMA(())])
    return pl.pallas_call(
        _dispatch_kernel,
        grid_spec=grid_spec,
        out_shape=jax.ShapeDtypeStruct((slots, d), F32),
        compiler_params=_cparams(("arbitrary",)),
        name="moe_dispatch",
    )(plan["pad_lo"], plan["pad_hi"], plan["n_used"], hf, plan["dest3"])


def _experts_kernel(bexp_ref, nused_ref, x_ref, wg_ref, wu_ref, wd_ref, y_ref):
    i = pl.program_id(0)

    @pl.when(i < nused_ref[0])
    def _():
        x = x_ref[...].astype(BF16)
        hg = jnp.dot(x, wg_ref[0], preferred_element_type=F32)
        hu = jnp.dot(x, wu_ref[0], preferred_element_type=F32)
        hb = (_silu(hg) * hu).astype(BF16)
        y_ref[...] = jnp.dot(hb, wd_ref[0], preferred_element_type=F32)

    @pl.when(i >= nused_ref[0])
    def _():
        y_ref[...] = jnp.zeros_like(y_ref)


def _experts(xs, plan, wg, wu, wd):
    slots, d = xs.shape
    n_blocks = plan["n_blocks"]
    used = lambda i, nu: jnp.minimum(i, nu[0] - 1)
    grid_spec = pltpu.PrefetchScalarGridSpec(
        num_scalar_prefetch=2,
        grid=(n_blocks,),
        in_specs=[pl.BlockSpec((MOE_BLOCK, d), lambda i, be, nu: (used(i, nu), 0)),
                  pl.BlockSpec((1, d, EXPERT_HIDDEN), lambda i, be, nu: (be[used(i, nu)], 0, 0)),
                  pl.BlockSpec((1, d, EXPERT_HIDDEN), lambda i, be, nu: (be[used(i, nu)], 0, 0)),
                  pl.BlockSpec((1, EXPERT_HIDDEN, d), lambda i, be, nu: (be[used(i, nu)], 0, 0))],
        out_specs=pl.BlockSpec((MOE_BLOCK, d), lambda i, be, nu: (i, 0)))
    return pl.pallas_call(
        _experts_kernel,
        grid_spec=grid_spec,
        out_shape=jax.ShapeDtypeStruct((slots, d), F32),
        compiler_params=_cparams(("arbitrary",)),
        name="moe_experts",
    )(plan["block_expert"], plan["n_used"], xs, wg, wu, wd)


def _swa_kernel(q_ref, kvp_ref, kvc_ref, qg_ref, kg_ref, slope_ref, sink_ref, o_ref):
    nb = pl.program_id(1)
    blk = CHUNK
    qi = _iota2((blk, 2 * blk), 0)
    sj = _iota2((blk, 2 * blk), 1)
    delta = qi + blk - sj
    first_key = jnp.where(nb > 0, 0, blk)
    mask = (delta >= 0) & (delta < WINDOW) & (sj >= first_key)
    deltaf = delta.astype(F32)
    neg = jnp.where(mask, 0.0, NEG_BIG)
    scale = HEAD_DIM ** -0.5

    bd = (_iota2((LANES, LANES), 0) // HEAD_DIM) == (_iota2((LANES, LANES), 1) // HEAD_DIM)
    bd_ones = jnp.where(bd, 1.0, 0.0).astype(BF16)

    def head_rms(x, gain):
        sq = x * x
        s1 = sq.astype(BF16)
        s2 = (sq - s1.astype(F32)).astype(BF16)
        ms = (jnp.dot(s1, bd_ones, preferred_element_type=F32)
              + jnp.dot(s2, bd_ones, preferred_element_type=F32)) * (1.0 / HEAD_DIM)
        return x * lax.rsqrt(ms + NORM_EPS) * gain

    n_kv_blk = KV_DIM // LANES
    kv = jnp.concatenate([kvp_ref[...], kvc_ref[...]], axis=0)
    lane_kv = _iota2((2 * blk, LANES), 1)
    kv_half = (lane_kv < HEAD_DIM, lane_kv >= HEAD_DIM)
    lane_q = _iota2((blk, LANES), 1)
    q_half = (lane_q < HEAD_DIM, lane_q >= HEAD_DIM)
    kn = [head_rms(kv[:, j * LANES:(j + 1) * LANES], kg_ref[...]) for j in range(n_kv_blk)]
    vb = [kv[:, KV_DIM + j * LANES:KV_DIM + (j + 1) * LANES] for j in range(n_kv_blk)]
    kn_sw = [pltpu.roll(x, HEAD_DIM, 1) for x in kn]
    vb_sw = [pltpu.roll(x, HEAD_DIM, 1) for x in vb]
    qn = [head_rms(q_ref[:, j * LANES:(j + 1) * LANES], qg_ref[...]) for j in range(Q_DIM // LANES)]

    s_all, v_all = [], []
    for h in range(ATT_HEADS):
        hh = h % 2
        kvh = h // Q_PER_KV
        same = (kvh % 2) == hh
        k_blk = kn[kvh // 2] if same else kn_sw[kvh // 2]
        v_blk = vb[kvh // 2] if same else vb_sw[kvh // 2]
        qm = jnp.where(q_half[hh], qn[h // 2], 0.0)
        s_all.append(_dot_nt(qm, k_blk) * scale - slope_ref[:, h:h + 1] * deltaf + neg)
        v_all.append(jnp.where(kv_half[hh], v_blk, 0.0))
    m_all = [jnp.maximum(jnp.max(s, axis=-1, keepdims=True), sink_ref[:, h:h + 1]) for h, s in enumerate(s_all)]
    p_all = [jnp.exp(s - m) for s, m in zip(s_all, m_all)]
    inv_all = [1.0 / (jnp.sum(p, axis=-1, keepdims=True) + jnp.exp(sink_ref[:, h:h + 1] - m))
               for h, (p, m) in enumerate(zip(p_all, m_all))]
    o_all = [_dot(p, vm) * inv for p, vm, inv in zip(p_all, v_all, inv_all)]
    for j in range(Q_DIM // LANES):
        o_ref[:, j * LANES:(j + 1) * LANES] = (o_all[2 * j] + o_all[2 * j + 1]).astype(o_ref.dtype)


def _swa(proj, batch, seq, q_gain, k_gain, sinks):
    nbk = seq // CHUNK
    n = batch * seq
    slopes = (2.0 ** (-8.0 * jnp.arange(1, ATT_HEADS + 1, dtype=F32) / ATT_HEADS))
    pad = lambda v: jnp.pad(v.astype(F32), (0, LANES - v.shape[0])).reshape(1, LANES)
    pair = lambda v: jnp.tile(v.astype(F32), LANES // HEAD_DIM).reshape(1, LANES)
    full = lambda shape: pl.BlockSpec(shape, lambda b, j: (0, 0))
    kvw = 2 * KV_DIM
    return pl.pallas_call(
        _swa_kernel,
        grid=(batch, nbk),
        in_specs=[pl.BlockSpec((CHUNK, Q_DIM), lambda b, j: (b * nbk + j, 0)),
                  pl.BlockSpec((CHUNK, kvw), lambda b, j: (b * nbk + jnp.maximum(j - 1, 0), Q_DIM // kvw)),
                  pl.BlockSpec((CHUNK, kvw), lambda b, j: (b * nbk + j, Q_DIM // kvw)),
                  full((1, LANES)), full((1, LANES)), full((1, LANES)), full((1, LANES))],
        out_specs=pl.BlockSpec((CHUNK, Q_DIM), lambda b, j: (b * nbk + j, 0)),
        out_shape=jax.ShapeDtypeStruct((n, Q_DIM), BF16),
        compiler_params=_cparams(("arbitrary", "arbitrary")),
        name="swa",
    )(proj, proj, proj, pair(q_gain), pair(k_gain), pad(slopes), pad(sinks))


def _even_in_weight(w):
    rw = SSD_COLS
    cols = jnp.concatenate([w[:, :SSD_INNER + SSD_CONV_DIM], w[:, rw:],
                            w[:, SSD_INNER + SSD_CONV_DIM:SSD_COLS]], axis=1)
    return jnp.pad(cols, ((0, 0), (0, EVEN_COLS_PAD - cols.shape[1]))).astype(BF16)


def kernel(x, ln_mix, ln_ffn, e_w_in, e_w_out, ssd_conv_w, ssd_conv_b, ssd_dt_bias, ssd_a_log, ssd_d, ssd_norm,
           rwkv_mu, rwkv_w0, rwkv_w2, rwkv_a0, rwkv_a2, rwkv_g2, rwkv_k_k, rwkv_k_a, rwkv_r_k, rwkv_ln_w,
           rwkv_ln_b, o_w_in, o_w_out, attn_q_norm, attn_k_norm, attn_sinks, moe_w_coarse, moe_b_coarse,
           moe_w_fine, moe_b_fine, moe_w_gate, moe_w_up, moe_w_down):
    batch, seq, d = x.shape
    n = batch * seq
    tm = min(512, n)
    h = x.reshape(n, d)

    def moe(layer, hf, route):
        plan = _moe_plan(route, tm)
        xs = _dispatch(hf, plan, tm)
        ys = _experts(xs, plan, moe_w_gate[layer].astype(BF16), moe_w_up[layer].astype(BF16),
                      moe_w_down[layer].astype(BF16))
        return ys, plan["dest3"]

    def router_w(layer):
        return _router_weights(moe_w_coarse[layer], moe_b_coarse[layer], moe_w_fine[layer], moe_b_fine[layer])

    proj = _norm_proj(h, ln_mix[0], _even_in_weight(e_w_in[0]), tm, EVEN_COLS_PAD // 3)
    y_ssd = _ssd(proj, batch, seq, ssd_conv_w[0], ssd_conv_b[0], ssd_dt_bias[0], ssd_a_log[0], ssd_d[0],
                 ssd_norm[0])
    y_rwkv = _rwkv(proj, batch, seq, rwkv_mu[0], rwkv_w0[0], rwkv_w2[0], rwkv_a0[0], rwkv_a2[0], rwkv_g2[0],
                   rwkv_k_k[0], rwkv_k_a[0], rwkv_r_k[0].reshape(-1), rwkv_ln_w[0], rwkv_ln_b[0])
    w_out = e_w_out[0].astype(BF16)
    wr, br = router_w(0)
    h, hf, route = _outproj_router([y_ssd, y_rwkv], [w_out[:SSD_INNER], w_out[SSD_INNER:]], h, ln_ffn[0],
                                   wr, br, tm)
    ys, dest3 = moe(0, hf, route)

    h, proj = _combine_norm_proj(h, ys, route, dest3, ln_mix[1], o_w_in[0].astype(BF16), tm)
    att = _swa(proj, batch, seq, attn_q_norm[0], attn_k_norm[0], attn_sinks[0])
    wr, br = router_w(1)
    h, hf, route = _outproj_router([att], [o_w_out[0].astype(BF16)], h, ln_ffn[1], wr, br, tm)
    ys, dest3 = moe(1, hf, route)
    out = _combine(h, ys, route, dest3, tm)
    return out.reshape(batch, seq, d)
```

```python
import functools
import math

import jax
import jax.numpy as jnp
from jax import lax
from jax.experimental import pallas as pl
from jax.experimental.pallas import tpu as pltpu

F32 = jnp.float32
BF16 = jnp.bfloat16

D_MODEL = 1024
SSD_HEADS = 16
SSD_HEAD_DIM = 64
SSD_INNER = 1024
SSD_GROUPS = 4
SSD_STATE = 128
SSD_CONV = 4
SSD_CONV_DIM = 2048
SSD_COLS = 3088
RWKV_HEADS = 16
RWKV_HEAD_DIM = 64
RWKV_DIM = 1024
DECAY_LORA = 64
AAA_LORA = 64
GATE_LORA = 128
RWKV_GN_EPS = 64e-5
ATT_HEADS = 16
KV_HEADS = 4
Q_PER_KV = 4
HEAD_DIM = 64
Q_DIM = 1024
KV_DIM = 256
WINDOW = 128
EXPERT_GROUPS = 4
EXPERTS_PER_GROUP = 8
N_EXPERTS = 32
TOP_K = 2
EXPERT_HIDDEN = 512
MOE_BLOCK = 256
NORM_EPS = 1e-6

LANES = 128
CHUNK = 128
TAIL = 8
DMA_UNROLL = 8
SWA_QBLOCKS = 4
EVEN_COLS_PAD = 6528
VMEM_LIMIT = 56 * 1024 * 1024
NEG_BIG = -1e30
assert WINDOW == CHUNK


def _cparams(sem):
    return pltpu.CompilerParams(dimension_semantics=sem, vmem_limit_bytes=VMEM_LIMIT)


def _dot(a, b):
    return jnp.dot(a.astype(BF16), b.astype(BF16), preferred_element_type=F32)


def _dot_nt(a, b):
    return lax.dot_general(a.astype(BF16), b.astype(BF16), (((1,), (1,)), ((), ())),
                           preferred_element_type=F32)


def _dot_tn(a, b):
    return lax.dot_general(a.astype(BF16), b.astype(BF16), (((0,), (0,)), ((), ())),
                           preferred_element_type=F32)


def _split3(x):
    x1 = x.astype(BF16)
    r1 = x - x1.astype(F32)
    x2 = r1.astype(BF16)
    x3 = (r1 - x2.astype(F32)).astype(BF16)
    return x1, x2, x3


def _dot_exact_lhs(m_bf16, x):
    x1, x2, x3 = _split3(x)
    f = lambda p: jnp.dot(m_bf16, p, preferred_element_type=F32)
    return f(x1) + f(x2) + f(x3)


def _dot_exact_rhs(x, m_bf16):
    x1, x2, x3 = _split3(x)
    f = lambda p: jnp.dot(p, m_bf16, preferred_element_type=F32)
    return f(x1) + f(x2) + f(x3)


def _dot_f32(x, w):
    x1, x2, _ = _split3(x)
    w1, w2, _ = _split3(w)
    f = lambda p, q: jnp.dot(p, q, preferred_element_type=F32)
    return f(x1, w1) + (f(x1, w2) + f(x2, w1))


def _sigmoid(x):
    return 1.0 / (1.0 + jnp.exp(-x))


def _silu(x):
    return x * _sigmoid(x)


def _softplus(x):
    return jnp.maximum(x, 0.0) + jnp.log(1.0 + jnp.exp(-jnp.abs(x)))


def _rms(x, gain):
    return x * lax.rsqrt(jnp.mean(x * x, axis=-1, keepdims=True) + NORM_EPS) * gain


def _iota2(shape, dim):
    return lax.broadcasted_iota(jnp.int32, shape, dim)


def _tri_incl_bf16(n):
    return jnp.where(_iota2((n, n), 0) >= _iota2((n, n), 1), 1.0, 0.0).astype(BF16)


def _norm_proj_kernel(h_ref, g_ref, w_ref, o_ref):
    o_ref[...] = _dot(_rms(h_ref[...], g_ref[...]), w_ref[...])


def _norm_proj(h, gain, w_bf16, tm, tn):
    n, d = h.shape
    c = w_bf16.shape[1]
    return pl.pallas_call(
        _norm_proj_kernel,
        grid=(c // tn, n // tm),
        in_specs=[pl.BlockSpec((tm, d), lambda j, i: (i, 0)),
                  pl.BlockSpec((1, d), lambda j, i: (0, 0)),
                  pl.BlockSpec((d, tn), lambda j, i: (0, j))],
        out_specs=pl.BlockSpec((tm, tn), lambda j, i: (i, j)),
        out_shape=jax.ShapeDtypeStruct((n, c), F32),
        compiler_params=_cparams(("arbitrary", "arbitrary")),
        name="norm_proj",
    )(h, gain.reshape(1, d), w_bf16)


def _gathered_combine(h_ref, route_ref, dest_ref, destn_ref, ys_hbm, yg, gsem):
    i = pl.program_id(0)
    nt = pl.num_programs(0)
    slot = lax.rem(i, 2)
    tm = h_ref.shape[0]

    def fetch(idx_ref, buf_slot):
        def body(g, carry):
            for u in range(DMA_UNROLL):
                t = g * DMA_UNROLL + u
                for c in range(TOP_K):
                    pltpu.make_async_copy(ys_hbm.at[pl.ds(idx_ref[0, 0, TOP_K * t + c], 1), :],
                                          yg.at[buf_slot, c, pl.ds(t, 1), :], gsem.at[buf_slot]).start()
            return carry
        lax.fori_loop(0, tm // DMA_UNROLL, body, 0)

    @pl.when(i == 0)
    def _():
        fetch(dest_ref, 0)

    @pl.when(i + 1 < nt)
    def _():
        fetch(destn_ref, 1 - slot)

    for c in range(TOP_K):
        pltpu.make_async_copy(ys_hbm.at[pl.ds(0, tm), :], yg.at[slot, c], gsem.at[slot]).wait()
    r = route_ref[...]
    return h_ref[...] + r[:, 2:3] * yg[slot, 0] + r[:, 3:4] * yg[slot, 1]


def _combine_norm_proj_kernel(h_ref, route_ref, dest_ref, destn_ref, g_ref, w_ref, ys_hbm, hnew_ref, o_ref,
                              yg, gsem):
    x = _gathered_combine(h_ref, route_ref, dest_ref, destn_ref, ys_hbm, yg, gsem)
    hnew_ref[...] = x
    o_ref[...] = _dot(_rms(x, g_ref[...]), w_ref[...])


def _combine_kernel(h_ref, route_ref, dest_ref, destn_ref, ys_hbm, o_ref, yg, gsem):
    o_ref[...] = _gathered_combine(h_ref, route_ref, dest_ref, destn_ref, ys_hbm, yg, gsem)


def _combine_specs(n, d, tm):
    nt = n // tm
    smem = lambda fn: pl.BlockSpec((1, 1, TOP_K * tm), fn, memory_space=pltpu.SMEM)
    in_specs = [pl.BlockSpec((tm, d), lambda i: (i, 0)),
                pl.BlockSpec((tm, LANES), lambda i: (i, 0)),
                smem(lambda i: (i, 0, 0)),
                smem(lambda i: (jnp.minimum(i + 1, nt - 1), 0, 0))]
    scratch = [pltpu.VMEM((2, TOP_K, tm, d), F32), pltpu.SemaphoreType.DMA((2,))]
    return in_specs, scratch


def _combine_norm_proj(h, ys, route, dest3, gain, w_bf16, tm):
    n, d = h.shape
    c = w_bf16.shape[1]
    in_specs, scratch = _combine_specs(n, d, tm)
    return pl.pallas_call(
        _combine_norm_proj_kernel,
        grid=(n // tm,),
        in_specs=in_specs + [pl.BlockSpec((1, d), lambda i: (0, 0)),
                             pl.BlockSpec((d, c), lambda i: (0, 0)),
                             pl.BlockSpec(memory_space=pl.ANY)],
        out_specs=[pl.BlockSpec((tm, d), lambda i: (i, 0)),
                   pl.BlockSpec((tm, c), lambda i: (i, 0))],
        out_shape=[jax.ShapeDtypeStruct((n, d), F32), jax.ShapeDtypeStruct((n, c), F32)],
        scratch_shapes=scratch,
        compiler_params=_cparams(("arbitrary",)),
        name="combine_norm_proj",
    )(h, route, dest3, dest3, gain.reshape(1, d), w_bf16, ys)


def _combine(h, ys, route, dest3, tm):
    n, d = h.shape
    in_specs, scratch = _combine_specs(n, d, tm)
    return pl.pallas_call(
        _combine_kernel,
        grid=(n // tm,),
        in_specs=in_specs + [pl.BlockSpec(memory_space=pl.ANY)],
        out_specs=pl.BlockSpec((tm, d), lambda i: (i, 0)),
        out_shape=jax.ShapeDtypeStruct((n, d), F32),
        scratch_shapes=scratch,
        compiler_params=_cparams(("arbitrary",)),
        name="combine",
    )(h, route, dest3, dest3, ys)


def _shifted_taps(buf, u_ref, n_taps):
    buf[TAIL:TAIL + CHUNK, :] = u_ref[...]
    taps = [buf[TAIL - j:TAIL - j + CHUNK, :] for j in range(n_taps)]
    return taps


def _carry_tail(buf):
    buf[0:TAIL, :] = buf[CHUNK:CHUNK + TAIL, :]


def _ssd_kernel(z_ref, x_ref, bc_ref, dt_ref, cwx_ref, cbx_ref, cwbc_ref, cbbc_ref, dtb_ref, alog_ref,
                dskip_ref, nw_ref, hexp_ref, o_ref, xbuf, bcbuf, state):
    c = pl.program_id(1)

    @pl.when(c == 0)
    def _():
        xbuf[0:TAIL, :] = jnp.zeros((TAIL, SSD_INNER), F32)
        bcbuf[0:TAIL, :] = jnp.zeros((TAIL, SSD_INNER), F32)
        state[...] = jnp.zeros_like(state)

    def conv(buf, u_ref, w_ref, b_ref):
        taps = _shifted_taps(buf, u_ref, SSD_CONV)
        acc = b_ref[...] + taps[0] * w_ref[3:4, :]
        for j in range(1, SSD_CONV):
            acc = acc + taps[j] * w_ref[3 - j:4 - j, :]
        _carry_tail(buf)
        return _silu(acc)

    xs = conv(xbuf, x_ref, cwx_ref, cbx_ref)
    bc = conv(bcbuf, bc_ref, cwbc_ref, cbbc_ref)

    lane = _iota2((CHUNK, LANES), 1)
    dt = _softplus(dt_ref[...] + dtb_ref[...])
    adt = jnp.where(lane < SSD_HEADS, -jnp.exp(alog_ref[...]) * dt, 0.0)
    tri = _tri_incl_bf16(CHUNK)
    cum = _dot_exact_lhs(tri, adt)
    cum_t = cum.T
    hexp = hexp_ref[...]
    cum_full = _dot_exact_rhs(cum, hexp)
    dt_full = _dot_exact_rhs(dt, hexp)
    tot_full = cum_full[CHUNK - 1:CHUNK, :]
    xd = xs * dt_full
    xds = xd * jnp.exp(tot_full - cum_full)
    eac = jnp.exp(cum_full)

    row = _iota2((CHUNK, CHUNK), 0)
    col = _iota2((CHUNK, CHUNK), 1)
    causal = row >= col
    lane_lo = lane < SSD_HEAD_DIM

    gw = SSD_INNER // SSD_GROUPS
    y_parts = []
    for g in range(SSD_GROUPS):
        bg = bc[:, g * SSD_STATE:(g + 1) * SSD_STATE]
        cg = bc[:, (SSD_GROUPS + g) * SSD_STATE:(SSD_GROUPS + g + 1) * SSD_STATE]
        cb = _dot_nt(cg, bg)
        s_prev = state[:, g * gw:(g + 1) * gw]
        y_off = _dot(cg, s_prev) * eac[:, g * gw:(g + 1) * gw]
        s_new = _dot(bg.T, xds[:, g * gw:(g + 1) * gw])
        state[:, g * gw:(g + 1) * gw] = jnp.exp(tot_full[:, g * gw:(g + 1) * gw]) * s_prev + s_new
        for pr in range(2):
            lo = g * gw + pr * LANES
            xd_pair = xd[:, lo:lo + LANES]
            yd = jnp.zeros((CHUNK, LANES), F32)
            for k in range(2):
                h = (lo // SSD_HEAD_DIM) + k
                diff = cum[:, h:h + 1] - cum_t[h:h + 1, :]
                decay = jnp.exp(jnp.where(causal, diff, NEG_BIG))
                keep = lane_lo if k == 0 else jnp.logical_not(lane_lo)
                yd = yd + _dot(cb * decay, jnp.where(keep, xd_pair, 0.0))
            y_parts.append(yd + y_off[:, pr * LANES:(pr + 1) * LANES])
    y = jnp.concatenate(y_parts, axis=1) + dskip_ref[...] * xs
    y = y * _silu(z_ref[...])
    outs = []
    for g in range(SSD_GROUPS):
        yg = y[:, g * gw:(g + 1) * gw]
        outs.append(yg * lax.rsqrt(jnp.mean(yg * yg, axis=-1, keepdims=True) + 1e-5))
    o_ref[...] = (jnp.concatenate(outs, axis=1) * nw_ref[...]).astype(o_ref.dtype)


def _head_expand(n_heads, head_dim):
    h = jnp.arange(LANES)[:, None]
    l = jnp.arange(n_heads * head_dim)[None, :]
    return (l // head_dim == h).astype(BF16)


def _ssd(proj, batch, seq, conv_w, conv_b, dt_bias, a_log, d_skip, norm_w):
    nc = seq // CHUNK
    n = batch * seq
    pad16 = lambda v: jnp.pad(v.astype(F32), (0, LANES - v.shape[0])).reshape(1, LANES)
    row_spec = lambda w, blk: pl.BlockSpec((CHUNK, w), lambda b, c: (b * nc + c, blk))
    full = lambda shape: pl.BlockSpec(shape, lambda b, c: (0, 0))
    return pl.pallas_call(
        _ssd_kernel,
        grid=(batch, nc),
        in_specs=[row_spec(SSD_INNER, 0),
                  row_spec(SSD_INNER, 1),
                  row_spec(SSD_INNER, 2),
                  row_spec(LANES, 50),
                  full((SSD_CONV, SSD_INNER)), full((1, SSD_INNER)),
                  full((SSD_CONV, SSD_INNER)), full((1, SSD_INNER)),
                  full((1, LANES)), full((1, LANES)),
                  full((1, SSD_INNER)), full((1, SSD_INNER)),
                  full((LANES, SSD_INNER))],
        out_specs=pl.BlockSpec((CHUNK, SSD_INNER), lambda b, c: (b * nc + c, 0)),
        out_shape=jax.ShapeDtypeStruct((n, SSD_INNER), BF16),
        scratch_shapes=[pltpu.VMEM((TAIL + CHUNK, SSD_INNER), F32),
                        pltpu.VMEM((TAIL + CHUNK, SSD_INNER), F32),
                        pltpu.VMEM((SSD_STATE, SSD_INNER), F32)],
        compiler_params=_cparams(("arbitrary", "arbitrary")),
        name="ssd",
    )(proj, proj, proj, proj,
      conv_w[:, :SSD_INNER], conv_b[:SSD_INNER].reshape(1, -1),
      conv_w[:, SSD_INNER:], conv_b[SSD_INNER:].reshape(1, -1),
      pad16(dt_bias), pad16(a_log),
      jnp.repeat(d_skip.astype(F32), SSD_HEAD_DIM).reshape(1, -1), norm_w.reshape(1, -1),
      _head_expand(SSD_HEADS, SSD_HEAD_DIM))


def _rwkv_kernel(r_ref, k_ref, v_ref, lo_ref, mur_ref, muk_ref, muv_ref, mulo_ref, w0_ref, w2_ref, a0_ref,
                 a2_ref, g2_ref, kk_ref, ka_ref, rk_ref, lnw_ref, lnb_ref, o_ref,
                 rbuf, kbuf, vbuf, lobuf, state):
    c = pl.program_id(1)

    @pl.when(c == 0)
    def _():
        rbuf[0:TAIL, :] = jnp.zeros((TAIL, RWKV_DIM), F32)
        kbuf[0:TAIL, :] = jnp.zeros((TAIL, RWKV_DIM), F32)
        vbuf[0:TAIL, :] = jnp.zeros((TAIL, RWKV_DIM), F32)
        lobuf[0:TAIL, :] = jnp.zeros((TAIL, 2 * LANES), F32)
        state[...] = jnp.zeros_like(state)

    def shift(buf, u_ref, mu_ref):
        cur, prev = _shifted_taps(buf, u_ref, 2)
        _carry_tail(buf)
        return cur + mu_ref[...] * (prev - cur)

    r = shift(rbuf, r_ref, mur_ref)
    k = shift(kbuf, k_ref, muk_ref)
    v = shift(vbuf, v_ref, muv_ref)
    lo = shift(lobuf, lo_ref, mulo_ref)
    wa = lo[:, :LANES]
    w = -_softplus(-(w0_ref[...] + _dot(jnp.tanh(wa), w2_ref[...]))) - 0.5
    logw = -jnp.exp(w)
    a = _sigmoid(a0_ref[...] + _dot(wa, a2_ref[...]))
    g = _dot(_sigmoid(lo[:, LANES:]), g2_ref[...])

    lane = _iota2((CHUNK, LANES), 1)
    head0 = lane < RWKV_HEAD_DIM
    bd = (_iota2((LANES, LANES), 0) // RWKV_HEAD_DIM) == (_iota2((LANES, LANES), 1) // RWKV_HEAD_DIM)
    bd_ones = jnp.where(bd, 1.0, 0.0).astype(BF16)

    def head_sum(x):
        return jnp.dot(x.astype(BF16), bd_ones, preferred_element_type=F32)

    tri = _tri_incl_bf16(CHUNK)
    cum = _dot_exact_lhs(tri, logw)
    cume = cum - logw
    cmid = cum[CHUNK // 2 - 1:CHUNK // 2, :]
    cend = cum[CHUNK - 1:CHUNK, :]
    e_in_mid = jnp.exp(cum - cmid)
    e_ex_mid = jnp.exp(cume - cmid)
    e_mid_in = jnp.exp(cmid - cum)
    e_ex = jnp.exp(cume)
    e_in = jnp.exp(cum)
    e_end = jnp.exp(cend - cum)
    e_tot = jnp.exp(cend)

    row = _iota2((CHUNK, CHUNK), 0)
    col = _iota2((CHUNK, CHUNK), 1)
    strict = row > col
    incl = row >= col
    zeros = jnp.zeros((CHUNK, LANES), F32)

    n_blocks = RWKV_DIM // LANES
    halves = (head0, jnp.logical_not(head0))
    blk = []
    for p in range(n_blocks):
        sl = slice(p * LANES, (p + 1) * LANES)
        rp, vp, ap = r[:, sl], v[:, sl], a[:, sl]
        kkp = k[:, sl] * kk_ref[:, sl]
        kkp = kkp / jnp.maximum(jnp.sqrt(head_sum(kkp * kkp)), 1e-12)
        kp = k[:, sl] * (1.0 + (ap - 1.0) * ka_ref[:, sl])
        aap = -kkp
        bp = kkp * ap
        a_mid = aap * e_ex_mid[:, sl]
        r_mid = rp * e_in_mid[:, sl]
        lhs = jnp.concatenate([jnp.where(head0, a_mid, 0.0), jnp.where(head0, 0.0, a_mid),
                               jnp.where(head0, r_mid, 0.0), jnp.where(head0, 0.0, r_mid)], axis=0)
        rhs = jnp.concatenate([bp * e_mid_in[:, sl], kp * e_mid_in[:, sl]], axis=0)
        blk.append(dict(sl=sl, rp=rp, vp=vp, kp=kp, a_abs=aap * e_ex[:, sl], r_abs=rp * e_in[:, sl],
                        b_end=bp * e_end[:, sl], k_end=kp * e_end[:, sl], prod=_dot_nt(lhs, rhs)))

    heads = []
    for p in range(n_blocks):
        prod = blk[p]["prod"]
        for hh in range(2):
            heads.append(dict(
                p=p, keep=halves[hh], vm=jnp.where(halves[hh], blk[p]["vp"], 0.0),
                a_ab=jnp.where(strict, prod[hh * CHUNK:(hh + 1) * CHUNK, :CHUNK], 0.0),
                a_ak=jnp.where(strict, prod[hh * CHUNK:(hh + 1) * CHUNK, CHUNK:], 0.0),
                m_rb=jnp.where(incl, prod[(2 + hh) * CHUNK:(3 + hh) * CHUNK, :CHUNK], 0.0),
                m_rk=jnp.where(incl, prod[(2 + hh) * CHUNK:(3 + hh) * CHUNK, CHUNK:], 0.0)))

    eye = jnp.where(row == col, 1.0, 0.0)
    n_levels = int(math.log2(CHUNK)) - 1
    ts = [eye + hd["a_ab"] for hd in heads]
    xs = [hd["a_ab"].astype(BF16) for hd in heads]
    avs = [_dot(hd["a_ak"], hd["vm"]) for hd in heads]
    xs = [jnp.dot(x, x, preferred_element_type=F32).astype(BF16) for x in xs]
    for _ in range(n_levels - 1):
        zs = [jnp.dot(x, jnp.concatenate([x, t.astype(BF16)], axis=1), preferred_element_type=F32)
              for x, t in zip(xs, ts)]
        xs = [z[:, :CHUNK].astype(BF16) for z in zs]
        ts = [t + z[:, CHUNK:] for t, z in zip(ts, zs)]
    ts = [t + jnp.dot(x, t.astype(BF16), preferred_element_type=F32) for x, t in zip(xs, ts)]

    wmats = [_dot(t, jnp.concatenate([jnp.where(hd["keep"], blk[hd["p"]]["a_abs"], 0.0), av], axis=1))
             for t, hd, av in zip(ts, heads, avs)]
    outs = [_dot(jnp.concatenate([hd["m_rb"], hd["m_rk"]], axis=1),
                 jnp.concatenate([jnp.concatenate([wm[:, LANES:], wm[:, :LANES]], axis=1),
                                  jnp.concatenate([hd["vm"], zeros], axis=1)], axis=0))
            for wm, hd in zip(wmats, heads)]

    zts = []
    for p in range(n_blocks):
        w0h, w1h = wmats[2 * p], wmats[2 * p + 1]
        ui = w0h[:, LANES:] + w1h[:, LANES:]
        a_eff = w0h[:, :LANES] + w1h[:, :LANES]
        zts.append(_dot_tn(jnp.concatenate([jnp.concatenate([ui, a_eff], axis=1),
                                            jnp.concatenate([blk[p]["vp"], zeros], axis=1)], axis=0),
                           jnp.concatenate([blk[p]["b_end"], blk[p]["k_end"]], axis=0)))

    ys = []
    for p in range(n_blocks):
        s0 = state[p]
        o0, o1 = outs[2 * p], outs[2 * p + 1]
        r_eff = blk[p]["r_abs"] + o0[:, LANES:] + o1[:, LANES:]
        ys.append(_dot_nt(r_eff, s0) + o0[:, :LANES] + o1[:, :LANES])
        h_intra = jnp.where(bd, zts[p][:LANES, :], 0.0)
        g_corr = jnp.where(bd, zts[p][LANES:, :], 0.0)
        state[p] = s0 * e_tot[:, blk[p]["sl"]] + _dot(s0, g_corr) + h_intra

    means = [head_sum(y) * (1.0 / RWKV_HEAD_DIM) for y in ys]
    devs = [y - m for y, m in zip(ys, means)]
    vars_ = [head_sum(dv * dv) * (1.0 / RWKV_HEAD_DIM) for dv in devs]
    for p in range(n_blocks):
        sl = blk[p]["sl"]
        yn = devs[p] * lax.rsqrt(vars_[p] + RWKV_GN_EPS) * lnw_ref[:, sl] + lnb_ref[:, sl]
        bonus = head_sum(blk[p]["rp"] * blk[p]["kp"] * rk_ref[:, sl])
        o_ref[:, sl] = ((yn + bonus * blk[p]["vp"]) * g[:, sl]).astype(o_ref.dtype)


def _rwkv(proj, batch, seq, mu, w0, w2, a0, a2, g2, k_k, k_a, r_k, ln_w, ln_b):
    nc = seq // CHUNK
    n = batch * seq
    d = RWKV_DIM
    row_spec = lambda w, blk: pl.BlockSpec((CHUNK, w), lambda b, c: (b * nc + c, blk))
    full = lambda shape: pl.BlockSpec(shape, lambda b, c: (0,) * len(shape))
    vec = lambda x: x.astype(F32).reshape(1, -1)
    w2p = jnp.concatenate([w2, jnp.zeros((AAA_LORA, d), w2.dtype)], axis=0).astype(BF16)
    a2p = jnp.concatenate([jnp.zeros((DECAY_LORA, d), a2.dtype), a2], axis=0).astype(BF16)
    return pl.pallas_call(
        _rwkv_kernel,
        grid=(batch, nc),
        in_specs=[row_spec(d, 3), row_spec(d, 4), row_spec(d, 5), row_spec(2 * LANES, 24),
                  full((1, d)), full((1, d)), full((1, d)), full((1, 2 * LANES)),
                  full((1, d)), full((LANES, d)), full((1, d)), full((LANES, d)), full((LANES, d)),
                  full((1, d)), full((1, d)), full((1, d)), full((1, d)), full((1, d))],
        out_specs=pl.BlockSpec((CHUNK, d), lambda b, c: (b * nc + c, 0)),
        out_shape=jax.ShapeDtypeStruct((n, d), BF16),
        scratch_shapes=[pltpu.VMEM((TAIL + CHUNK, d), F32), pltpu.VMEM((TAIL + CHUNK, d), F32),
                        pltpu.VMEM((TAIL + CHUNK, d), F32), pltpu.VMEM((TAIL + CHUNK, 2 * LANES), F32),
                        pltpu.VMEM((d // LANES, LANES, LANES), F32)],
        compiler_params=_cparams(("arbitrary", "arbitrary")),
        name="rwkv",
    )(proj, proj, proj, proj,
      vec(mu[:d]), vec(mu[d:2 * d]), vec(mu[2 * d:3 * d]), vec(mu[3 * d:]),
      vec(w0), w2p, vec(a0), a2p, g2.astype(BF16),
      vec(k_k), vec(k_a), vec(r_k), vec(ln_w), vec(ln_b))


def _route(logits):
    lane = _iota2(logits.shape, 1)
    lanef = lane.astype(F32)
    big = float(LANES)

    def first_max(x):
        m = jnp.max(x, axis=-1, keepdims=True)
        idx = jnp.min(jnp.where(x == m, lanef, big), axis=-1, keepdims=True)
        return m, idx

    cl = jnp.where(lane < EXPERT_GROUPS, logits, NEG_BIG)
    cmax, grp = first_max(cl)
    p_group = 1.0 / jnp.sum(jnp.exp(cl - cmax), axis=-1, keepdims=True)
    lo = EXPERT_GROUPS + grp * EXPERTS_PER_GROUP
    fl = jnp.where((lanef >= lo) & (lanef < lo + EXPERTS_PER_GROUP), logits, NEG_BIG)
    m0, i0 = first_max(fl)
    m1, i1 = first_max(jnp.where(lanef == i0, NEG_BIG, fl))
    e1 = jnp.exp(m1 - m0)
    g0 = p_group / (1.0 + e1)
    g1 = p_group * e1 / (1.0 + e1)
    return jnp.where(lane == 0, i0 - EXPERT_GROUPS,
                     jnp.where(lane == 1, i1 - EXPERT_GROUPS,
                               jnp.where(lane == 2, g0, jnp.where(lane == 3, g1, 0.0))))


def _outproj_router_kernel(*refs, n_in):
    ys = refs[:n_in]
    ws = refs[n_in:2 * n_in]
    h_ref, g_ref, wr_ref, br_ref, hnew_ref, hf_ref, route_ref = refs[2 * n_in:]
    acc = h_ref[...]
    for y_ref, w_ref in zip(ys, ws):
        acc = acc + jnp.dot(y_ref[...], w_ref[...], preferred_element_type=F32)
    hnew_ref[...] = acc
    hf = _rms(acc, g_ref[...])
    hf_ref[...] = hf
    route_ref[...] = _route(_dot_f32(hf, wr_ref[...]) + br_ref[...])


def _outproj_router(ys, ws, h, gain, w_router, b_router, tm):
    n, d = h.shape
    n_in = len(ys)
    in_specs = ([pl.BlockSpec((tm, y.shape[1]), lambda i: (i, 0)) for y in ys]
                + [pl.BlockSpec(w.shape, lambda i: (0, 0)) for w in ws]
                + [pl.BlockSpec((tm, d), lambda i: (i, 0)),
                   pl.BlockSpec((1, d), lambda i: (0, 0)),
                   pl.BlockSpec((d, LANES), lambda i: (0, 0)),
                   pl.BlockSpec((1, LANES), lambda i: (0, 0))])
    return pl.pallas_call(
        functools.partial(_outproj_router_kernel, n_in=n_in),
        grid=(n // tm,),
        in_specs=in_specs,
        out_specs=[pl.BlockSpec((tm, d), lambda i: (i, 0)),
                   pl.BlockSpec((tm, d), lambda i: (i, 0)),
                   pl.BlockSpec((tm, LANES), lambda i: (i, 0))],
        out_shape=[jax.ShapeDtypeStruct((n, d), F32), jax.ShapeDtypeStruct((n, d), F32),
                   jax.ShapeDtypeStruct((n, LANES), F32)],
        compiler_params=_cparams(("arbitrary",)),
        name="outproj_router",
    )(*ys, *ws, h, gain.reshape(1, d), w_router, b_router)


def _router_weights(w_coarse, b_coarse, w_fine, b_fine):
    d = w_coarse.shape[0]
    wf = jnp.transpose(w_fine, (1, 0, 2)).reshape(d, N_EXPERTS)
    w = jnp.concatenate([w_coarse, wf], axis=1).astype(F32)
    b = jnp.concatenate([b_coarse, b_fine.reshape(N_EXPERTS)]).astype(F32)
    pad = LANES - w.shape[1]
    return jnp.pad(w, ((0, 0), (0, pad))), jnp.pad(b, (0, pad)).reshape(1, LANES)


def _moe_plan(route, tm):
    n = route.shape[0]
    a = n * TOP_K
    n_blocks = a // MOE_BLOCK + N_EXPERTS
    e_flat = route[:, :TOP_K].astype(jnp.int32).reshape(a)
    seg = MOE_BLOCK
    onehot = (e_flat[:, None] == jnp.arange(N_EXPERTS, dtype=jnp.int32)[None, :]).astype(F32)
    onehot = onehot.reshape(a // seg, seg, N_EXPERTS)
    tri = jnp.tril(jnp.ones((seg, seg), F32))
    within = jnp.einsum("ij,bjk->bik", tri, onehot)
    tot = within[:, -1, :]
    offs = jnp.cumsum(tot, axis=0) - tot
    rank = (jnp.sum(onehot * (within + offs[:, None, :]), axis=-1) - 1.0).astype(jnp.int32).reshape(a)
    counts = (offs[-1] + tot[-1]).astype(jnp.int32)
    padded = (counts + MOE_BLOCK - 1) // MOE_BLOCK * MOE_BLOCK
    pad_end = jnp.cumsum(padded)
    pad_start = pad_end - padded
    start_of = jnp.sum(jnp.where(e_flat[:, None] == jnp.arange(N_EXPERTS, dtype=jnp.int32)[None, :],
                                 pad_start[None, :], 0), axis=1)
    dest = (start_of + rank).astype(jnp.int32)
    block_start = jnp.arange(n_blocks, dtype=jnp.int32) * MOE_BLOCK
    block_expert = jnp.minimum(jnp.sum((pad_end[None, :] <= block_start[:, None]).astype(jnp.int32), axis=1),
                               N_EXPERTS - 1).astype(jnp.int32)
    n_used = (pad_end[-1] // MOE_BLOCK).astype(jnp.int32).reshape(1)
    pad_lo = (pad_start + counts).astype(jnp.int32)
    return dict(dest3=dest.reshape(n // tm, 1, TOP_K * tm), block_expert=block_expert, n_used=n_used,
                pad_lo=pad_lo, pad_hi=pad_end.astype(jnp.int32), n_blocks=n_blocks)


def _dispatch_kernel(padlo_ref, padhi_ref, nused_ref, hf_ref, dest_ref, xs_hbm, zblk, sem, zsem):
    i = pl.program_id(0)
    tm = hf_ref.shape[0]

    @pl.when(i == 0)
    def _():
        zblk[...] = jnp.zeros_like(zblk)

        def per_expert(e, carry):
            def zero_row(rw, c2):
                pltpu.make_async_copy(zblk.at[pl.ds(0, 1), :], xs_hbm.at[pl.ds(rw, 1), :], zsem).start()
                return c2
            lax.fori_loop(padlo_ref[e], padhi_ref[e], zero_row, 0)

            def wait_row(rw, c2):
                pltpu.make_async_copy(zblk.at[pl.ds(0, 1), :], xs_hbm.at[pl.ds(rw, 1), :], zsem).wait()
                return c2
            lax.fori_loop(padlo_ref[e], padhi_ref[e], wait_row, 0)
            return carry
        lax.fori_loop(0, N_EXPERTS, per_expert, 0)

        n_blocks = xs_hbm.shape[0] // MOE_BLOCK

        def block_copy(b):
            return pltpu.make_async_copy(
                zblk, xs_hbm.at[pl.ds(pl.multiple_of(b * MOE_BLOCK, MOE_BLOCK), MOE_BLOCK), :], zsem)

        def zero_block(b, carry):
            block_copy(b).start()
            return carry
        lax.fori_loop(nused_ref[0], n_blocks, zero_block, 0)

        def wait_block(b, carry):
            block_copy(b).wait()
            return carry
        lax.fori_loop(nused_ref[0], n_blocks, wait_block, 0)

    def body(g, carry):
        for u in range(DMA_UNROLL):
            t = g * DMA_UNROLL + u
            for c in range(TOP_K):
                pltpu.make_async_copy(hf_ref.at[pl.ds(t, 1), :],
                                      xs_hbm.at[pl.ds(dest_ref[0, 0, TOP_K * t + c], 1), :], sem).start()
        return carry
    lax.fori_loop(0, tm // DMA_UNROLL, body, 0)
    for c in range(TOP_K):
        pltpu.make_async_copy(hf_ref, xs_hbm.at[pl.ds(0, tm), :], sem).wait()


def _dispatch(hf, plan, tm):
    n, d = hf.shape
    slots = plan["n_blocks"] * MOE_BLOCK
    grid_spec = pltpu.PrefetchScalarGridSpec(
        num_scalar_prefetch=3,
        grid=(n // tm,),
        in_specs=[pl.BlockSpec((tm, d), lambda i, lo, hi, nu: (i, 0)),
                  pl.BlockSpec((1, 1, TOP_K * tm), lambda i, lo, hi, nu: (i, 0, 0), memory_space=pltpu.SMEM)],
        out_specs=pl.BlockSpec(memory_space=pl.ANY),
        scratch_shapes=[pltpu.VMEM((MOE_BLOCK, d), F32), pltpu.SemaphoreType.DMA(()),
                        pltpu.SemaphoreType.DMA(())])
    return pl.pallas_call(
        _dispatch_kernel,
        grid_spec=grid_spec,
        out_shape=jax.ShapeDtypeStruct((slots, d), F32),
        compiler_params=_cparams(("arbitrary",)),
        name="moe_dispatch",
    )(plan["pad_lo"], plan["pad_hi"], plan["n_used"], hf, plan["dest3"])


def _experts_kernel(bexp_ref, nused_ref, x_ref, wg_ref, wu_ref, wd_ref, y_ref, wg_s, wu_s, wd_s):
    i = pl.program_id(0)
    active = i < nused_ref[0]

    @pl.when(active & ((i == 0) | (bexp_ref[i] != bexp_ref[jnp.maximum(i - 1, 0)])))
    def _():
        wg_s[...] = wg_ref[0].astype(BF16)
        wu_s[...] = wu_ref[0].astype(BF16)
        wd_s[...] = wd_ref[0].astype(BF16)

    @pl.when(active)
    def _():
        x = x_ref[...].astype(BF16)
        hg = jnp.dot(x, wg_s[...], preferred_element_type=F32)
        hu = jnp.dot(x, wu_s[...], preferred_element_type=F32)
        hb = (_silu(hg) * hu).astype(BF16)
        y_ref[...] = jnp.dot(hb, wd_s[...], preferred_element_type=F32)

    @pl.when(i >= nused_ref[0])
    def _():
        y_ref[...] = jnp.zeros_like(y_ref)


def _experts(xs, plan, layer, wg, wu, wd):
    slots, d = xs.shape
    n_blocks = plan["n_blocks"]
    used = lambda i, nu: jnp.minimum(i, nu[0] - 1)
    w_idx = lambda i, be, nu: (layer, be[used(i, nu)], 0, 0)
    grid_spec = pltpu.PrefetchScalarGridSpec(
        num_scalar_prefetch=2,
        grid=(n_blocks,),
        in_specs=[pl.BlockSpec((MOE_BLOCK, d), lambda i, be, nu: (used(i, nu), 0)),
                  pl.BlockSpec((None, 1, d, EXPERT_HIDDEN), w_idx),
                  pl.BlockSpec((None, 1, d, EXPERT_HIDDEN), w_idx),
                  pl.BlockSpec((None, 1, EXPERT_HIDDEN, d), w_idx)],
        out_specs=pl.BlockSpec((MOE_BLOCK, d), lambda i, be, nu: (i, 0)),
        scratch_shapes=[pltpu.VMEM((d, EXPERT_HIDDEN), BF16), pltpu.VMEM((d, EXPERT_HIDDEN), BF16),
                        pltpu.VMEM((EXPERT_HIDDEN, d), BF16)])
    return pl.pallas_call(
        _experts_kernel,
        grid_spec=grid_spec,
        out_shape=jax.ShapeDtypeStruct((slots, d), F32),
        compiler_params=_cparams(("arbitrary",)),
        name="moe_experts",
    )(plan["block_expert"], plan["n_used"], xs, wg, wu, wd)


def _swa_kernel(q_ref, kvp_ref, kvc_ref, qg_ref, kg_ref, slope_ref, sink_ref, o_ref):
    jb = pl.program_id(1)
    blk = CHUNK
    qi = _iota2((blk, blk), 0)
    kj = _iota2((blk, blk), 1)
    from_prev = kj > qi
    deltaf = jnp.where(from_prev, qi + blk - kj, qi - kj).astype(F32)
    no_prev = jnp.where(from_prev, jnp.where(jb > 0, 0.0, NEG_BIG), 0.0)
    scale = HEAD_DIM ** -0.5

    bd = (_iota2((LANES, LANES), 0) // HEAD_DIM) == (_iota2((LANES, LANES), 1) // HEAD_DIM)
    bd_ones = jnp.where(bd, 1.0, 0.0).astype(BF16)

    def head_rms(x, gain):
        sq = x * x
        s1 = sq.astype(BF16)
        s2 = (sq - s1.astype(F32)).astype(BF16)
        ms = (jnp.dot(s1, bd_ones, preferred_element_type=F32)
              + jnp.dot(s2, bd_ones, preferred_element_type=F32)) * (1.0 / HEAD_DIM)
        return x * lax.rsqrt(ms + NORM_EPS) * gain

    n_kv_blk = KV_DIM // LANES
    kv = jnp.concatenate([kvp_ref[...], kvc_ref[...]], axis=0)
    lane_kv = _iota2((kv.shape[0], LANES), 1)
    kv_half = (lane_kv < HEAD_DIM, lane_kv >= HEAD_DIM)
    lane_q = _iota2((blk, LANES), 1)
    q_half = (lane_q < HEAD_DIM, lane_q >= HEAD_DIM)
    kn = [head_rms(kv[:, j * LANES:(j + 1) * LANES], kg_ref[...]) for j in range(n_kv_blk)]
    vb = [kv[:, KV_DIM + j * LANES:KV_DIM + (j + 1) * LANES] for j in range(n_kv_blk)]
    kn_sw = [pltpu.roll(x, HEAD_DIM, 1) for x in kn]
    vb_sw = [pltpu.roll(x, HEAD_DIM, 1) for x in vb]
    v_same = [jnp.where(kv_half[g % 2], vb[g // 2], 0.0) for g in range(KV_HEADS)]
    v_swap = [jnp.where(kv_half[1 - g % 2], vb_sw[g // 2], 0.0) for g in range(KV_HEADS)]
    qn = [head_rms(q_ref[:, j * LANES:(j + 1) * LANES], qg_ref[...]) for j in range(Q_DIM // LANES)]

    rep = lambda x: jnp.concatenate([x] * Q_PER_KV, axis=0)
    from_prev4, delta4 = rep(from_prev), rep(deltaf)
    neg4 = [rep(no_prev)] + [None] * (SWA_QBLOCKS - 1)
    col = lambda ref, hs: jnp.concatenate([jnp.broadcast_to(ref[:, h:h + 1], (blk, 1)) for h in hs], axis=0)
    order = lambda g: [g * Q_PER_KV + g % 2, g * Q_PER_KV + g % 2 + 2,
                       g * Q_PER_KV + 1 - g % 2, g * Q_PER_KV + 3 - g % 2]
    units = [(u, g) for u in range(SWA_QBLOCKS) for g in range(KV_HEADS)]
    keys = lambda x, u: x[u * blk:(u + 2) * blk]
    qrow = lambda x, u: x[u * blk:(u + 1) * blk]

    qms = [[jnp.where(q_half[h % 2], qrow(qn[h // 2], u), 0.0) for h in order(g)] for u, g in units]
    scs = [jnp.concatenate([_dot_nt(jnp.concatenate(qm[:2], axis=0), keys(kn[g // 2], u)),
                            _dot_nt(jnp.concatenate(qm[2:], axis=0), keys(kn_sw[g // 2], u))], axis=0)
           for qm, (u, g) in zip(qms, units)]
    sinks = [col(sink_ref, order(g)) for u, g in units]
    ss = []
    for sc, (u, g) in zip(scs, units):
        s = jnp.where(from_prev4, sc[:, :blk], sc[:, blk:]) * scale - col(slope_ref, order(g)) * delta4
        ss.append(s if neg4[u] is None else s + neg4[u])
    ms = [jnp.maximum(jnp.max(s, axis=-1, keepdims=True), sk) for s, sk in zip(ss, sinks)]
    ps = [jnp.exp(s - m) for s, m in zip(ss, ms)]
    invs = [1.0 / (jnp.sum(p, axis=-1, keepdims=True) + jnp.exp(sk - m)) for p, sk, m in zip(ps, sinks, ms)]
    pcats = [jnp.concatenate([jnp.where(from_prev4, p, 0.0), jnp.where(from_prev4, 0.0, p)], axis=1) for p in ps]
    outs = [jnp.concatenate([_dot(pc[:2 * blk], keys(v_same[g], u)), _dot(pc[2 * blk:], keys(v_swap[g], u))],
                            axis=0) * inv
            for pc, inv, (u, g) in zip(pcats, invs, units)]
    for o, (u, g) in zip(outs, units):
        hs = order(g)
        for j in sorted({h // 2 for h in hs}):
            pair = sum(o[idx * blk:(idx + 1) * blk] for idx, h in enumerate(hs) if h // 2 == j)
            o_ref[u * blk:(u + 1) * blk, j * LANES:(j + 1) * LANES] = pair.astype(o_ref.dtype)


def _swa(proj, batch, seq, q_gain, k_gain, sinks):
    nbk = seq // CHUNK
    qrows = SWA_QBLOCKS * CHUNK
    nsteps = seq // qrows
    n = batch * seq
    slopes = (2.0 ** (-8.0 * jnp.arange(1, ATT_HEADS + 1, dtype=F32) / ATT_HEADS))
    pad = lambda v: jnp.pad(v.astype(F32), (0, LANES - v.shape[0])).reshape(1, LANES)
    pair = lambda v: jnp.tile(v.astype(F32), LANES // HEAD_DIM).reshape(1, LANES)
    full = lambda shape: pl.BlockSpec(shape, lambda b, j: (0, 0))
    kvw = 2 * KV_DIM
    return pl.pallas_call(
        _swa_kernel,
        grid=(batch, nsteps),
        in_specs=[pl.BlockSpec((qrows, Q_DIM), lambda b, j: (b * nsteps + j, 0)),
                  pl.BlockSpec((CHUNK, kvw),
                               lambda b, j: (b * nbk + jnp.maximum(SWA_QBLOCKS * j - 1, 0), Q_DIM // kvw)),
                  pl.BlockSpec((qrows, kvw), lambda b, j: (b * nsteps + j, Q_DIM // kvw)),
                  full((1, LANES)), full((1, LANES)), full((1, LANES)), full((1, LANES))],
        out_specs=pl.BlockSpec((qrows, Q_DIM), lambda b, j: (b * nsteps + j, 0)),
        out_shape=jax.ShapeDtypeStruct((n, Q_DIM), BF16),
        compiler_params=_cparams(("arbitrary", "arbitrary")),
        name="swa",
    )(proj, proj, proj, pair(q_gain), pair(k_gain), pad(slopes), pad(sinks))


def _even_in_weight(w):
    rw = SSD_COLS
    cols = jnp.concatenate([w[:, :SSD_INNER + SSD_CONV_DIM], w[:, rw:],
                            w[:, SSD_INNER + SSD_CONV_DIM:SSD_COLS]], axis=1)
    return jnp.pad(cols, ((0, 0), (0, EVEN_COLS_PAD - cols.shape[1]))).astype(BF16)


def kernel(x, ln_mix, ln_ffn, e_w_in, e_w_out, ssd_conv_w, ssd_conv_b, ssd_dt_bias, ssd_a_log, ssd_d, ssd_norm,
           rwkv_mu, rwkv_w0, rwkv_w2, rwkv_a0, rwkv_a2, rwkv_g2, rwkv_k_k, rwkv_k_a, rwkv_r_k, rwkv_ln_w,
           rwkv_ln_b, o_w_in, o_w_out, attn_q_norm, attn_k_norm, attn_sinks, moe_w_coarse, moe_b_coarse,
           moe_w_fine, moe_b_fine, moe_w_gate, moe_w_up, moe_w_down):
    batch, seq, d = x.shape
    n = batch * seq
    tm = min(512, n)
    h = x.reshape(n, d)

    def moe(layer, hf, route):
        plan = _moe_plan(route, tm)
        xs = _dispatch(hf, plan, tm)
        ys = _experts(xs, plan, layer, moe_w_gate, moe_w_up, moe_w_down)
        return ys, plan["dest3"]

    def router_w(layer):
        return _router_weights(moe_w_coarse[layer], moe_b_coarse[layer], moe_w_fine[layer], moe_b_fine[layer])

    proj = _norm_proj(h, ln_mix[0], _even_in_weight(e_w_in[0]), tm, EVEN_COLS_PAD // 3)
    y_ssd = _ssd(proj, batch, seq, ssd_conv_w[0], ssd_conv_b[0], ssd_dt_bias[0], ssd_a_log[0], ssd_d[0],
                 ssd_norm[0])
    y_rwkv = _rwkv(proj, batch, seq, rwkv_mu[0], rwkv_w0[0], rwkv_w2[0], rwkv_a0[0], rwkv_a2[0], rwkv_g2[0],
                   rwkv_k_k[0], rwkv_k_a[0], rwkv_r_k[0].reshape(-1), rwkv_ln_w[0], rwkv_ln_b[0])
    w_out = e_w_out[0].astype(BF16)
    wr, br = router_w(0)
    h, hf, route = _outproj_router([y_ssd, y_rwkv], [w_out[:SSD_INNER], w_out[SSD_INNER:]], h, ln_ffn[0],
                                   wr, br, tm)
    ys, dest3 = moe(0, hf, route)

    h, proj = _combine_norm_proj(h, ys, route, dest3, ln_mix[1], o_w_in[0].astype(BF16), tm)
    att = _swa(proj, batch, seq, attn_q_norm[0], attn_k_norm[0], attn_sinks[0])
    wr, br = router_w(1)
    h, hf, route = _outproj_router([att], [o_w_out[0].astype(BF16)], h, ln_ffn[1], wr, br, tm)
    ys, dest3 = moe(1, hf, route)
    out = _combine(h, ys, route, dest3, tm)
    return out.reshape(batch, seq, d)
```

```python
import functools
import math

import jax
import jax.numpy as jnp
from jax import lax
from jax.experimental import pallas as pl
from jax.experimental.pallas import tpu as pltpu

F32 = jnp.float32
BF16 = jnp.bfloat16

D_MODEL = 1024
SSD_HEADS = 16
SSD_HEAD_DIM = 64
SSD_INNER = 1024
SSD_GROUPS = 4
SSD_STATE = 128
SSD_CONV = 4
SSD_CONV_DIM = 2048
SSD_COLS = 3088
RWKV_HEADS = 16
RWKV_HEAD_DIM = 64
RWKV_DIM = 1024
DECAY_LORA = 64
AAA_LORA = 64
GATE_LORA = 128
RWKV_GN_EPS = 64e-5
ATT_HEADS = 16
KV_HEADS = 4
Q_PER_KV = 4
HEAD_DIM = 64
Q_DIM = 1024
KV_DIM = 256
WINDOW = 128
EXPERT_GROUPS = 4
EXPERTS_PER_GROUP = 8
N_EXPERTS = 32
TOP_K = 2
EXPERT_HIDDEN = 512
MOE_BLOCK = 256
NORM_EPS = 1e-6

LANES = 128
CHUNK = 128
TAIL = 8
DMA_UNROLL = 8
SWA_QBLOCKS = 4
EVEN_COLS_PAD = 6528
VMEM_LIMIT = 56 * 1024 * 1024
NEG_BIG = -1e30
assert WINDOW == CHUNK


def _cparams(sem):
    return pltpu.CompilerParams(dimension_semantics=sem, vmem_limit_bytes=VMEM_LIMIT)


def _dot(a, b):
    return jnp.dot(a.astype(BF16), b.astype(BF16), preferred_element_type=F32)


def _dot_nt(a, b):
    return lax.dot_general(a.astype(BF16), b.astype(BF16), (((1,), (1,)), ((), ())),
                           preferred_element_type=F32)


def _dot_tn(a, b):
    return lax.dot_general(a.astype(BF16), b.astype(BF16), (((0,), (0,)), ((), ())),
                           preferred_element_type=F32)


def _split3(x):
    x1 = x.astype(BF16)
    r1 = x - x1.astype(F32)
    x2 = r1.astype(BF16)
    x3 = (r1 - x2.astype(F32)).astype(BF16)
    return x1, x2, x3


def _dot_exact_lhs(m_bf16, x):
    x1, x2, x3 = _split3(x)
    f = lambda p: jnp.dot(m_bf16, p, preferred_element_type=F32)
    return f(x1) + f(x2) + f(x3)


def _dot_exact_rhs(x, m_bf16):
    x1, x2, x3 = _split3(x)
    f = lambda p: jnp.dot(p, m_bf16, preferred_element_type=F32)
    return f(x1) + f(x2) + f(x3)


def _dot_f32(x, w):
    x1, x2, _ = _split3(x)
    w1, w2, _ = _split3(w)
    f = lambda p, q: jnp.dot(p, q, preferred_element_type=F32)
    return f(x1, w1) + (f(x1, w2) + f(x2, w1))


def _sigmoid(x):
    return 1.0 / (1.0 + jnp.exp(-x))


def _silu(x):
    return x * _sigmoid(x)


def _softplus(x):
    return jnp.maximum(x, 0.0) + jnp.log(1.0 + jnp.exp(-jnp.abs(x)))


def _rms(x, gain):
    return x * lax.rsqrt(jnp.mean(x * x, axis=-1, keepdims=True) + NORM_EPS) * gain


def _iota2(shape, dim):
    return lax.broadcasted_iota(jnp.int32, shape, dim)


def _tri_incl_bf16(n):
    return jnp.where(_iota2((n, n), 0) >= _iota2((n, n), 1), 1.0, 0.0).astype(BF16)


def _norm_proj_kernel(h_ref, g_ref, w_ref, o_ref):
    o_ref[...] = _dot(_rms(h_ref[...], g_ref[...]), w_ref[...])


def _norm_proj(h, gain, w_bf16, tm, tn):
    n, d = h.shape
    c = w_bf16.shape[1]
    return pl.pallas_call(
        _norm_proj_kernel,
        grid=(c // tn, n // tm),
        in_specs=[pl.BlockSpec((tm, d), lambda j, i: (i, 0)),
                  pl.BlockSpec((1, d), lambda j, i: (0, 0)),
                  pl.BlockSpec((d, tn), lambda j, i: (0, j))],
        out_specs=pl.BlockSpec((tm, tn), lambda j, i: (i, j)),
        out_shape=jax.ShapeDtypeStruct((n, c), F32),
        compiler_params=_cparams(("arbitrary", "arbitrary")),
        name="norm_proj",
    )(h, gain.reshape(1, d), w_bf16)


def _gathered_combine(h_ref, route_ref, dest_ref, destn_ref, ys_hbm, yg, gsem):
    i = pl.program_id(0)
    nt = pl.num_programs(0)
    slot = lax.rem(i, 2)
    tm = h_ref.shape[0]

    def row_copy(idx_ref, buf_slot, t, c):
        return pltpu.make_async_copy(ys_hbm.at[pl.ds(idx_ref[0, 0, TOP_K * t + c], 1), :],
                                     yg.at[buf_slot, c, pl.ds(t, 1), :], gsem.at[buf_slot])

    def wait_tile(buf_slot):
        for c in range(TOP_K):
            pltpu.make_async_copy(ys_hbm.at[pl.ds(0, tm), :], yg.at[buf_slot, c], gsem.at[buf_slot]).wait()

    @pl.when(i == 0)
    def _():
        def body(g, carry):
            for u in range(DMA_UNROLL):
                for c in range(TOP_K):
                    row_copy(dest_ref, 0, g * DMA_UNROLL + u, c).start()
            return carry
        lax.fori_loop(0, tm // DMA_UNROLL, body, 0)

    wait_tile(slot)
    for t in range(tm):
        for c in range(TOP_K):
            row_copy(destn_ref, 1 - slot, t, c).start()
    r = route_ref[...]
    x = h_ref[...] + r[:, 2:3] * yg[slot, 0] + r[:, 3:4] * yg[slot, 1]

    def finalize():
        @pl.when(i == nt - 1)
        def _():
            wait_tile(1 - slot)
    return x, finalize


def _combine_norm_proj_kernel(h_ref, route_ref, dest_ref, destn_ref, g_ref, w_ref, ys_hbm, hnew_ref, o_ref,
                              yg, gsem):
    x, finalize = _gathered_combine(h_ref, route_ref, dest_ref, destn_ref, ys_hbm, yg, gsem)
    hnew_ref[...] = x
    o_ref[...] = _dot(_rms(x, g_ref[...]), w_ref[...])
    finalize()


def _combine_kernel(h_ref, route_ref, dest_ref, destn_ref, ys_hbm, o_ref, yg, gsem):
    x, finalize = _gathered_combine(h_ref, route_ref, dest_ref, destn_ref, ys_hbm, yg, gsem)
    o_ref[...] = x
    finalize()


def _combine_specs(n, d, tm):
    nt = n // tm
    smem = lambda fn: pl.BlockSpec((1, 1, TOP_K * tm), fn, memory_space=pltpu.SMEM)
    in_specs = [pl.BlockSpec((tm, d), lambda i: (i, 0)),
                pl.BlockSpec((tm, LANES), lambda i: (i, 0)),
                smem(lambda i: (i, 0, 0)),
                smem(lambda i: (jnp.minimum(i + 1, nt - 1), 0, 0))]
    scratch = [pltpu.VMEM((2, TOP_K, tm, d), F32), pltpu.SemaphoreType.DMA((2,))]
    return in_specs, scratch


def _combine_norm_proj(h, ys, route, dest3, gain, w_bf16, tm):
    n, d = h.shape
    c = w_bf16.shape[1]
    in_specs, scratch = _combine_specs(n, d, tm)
    return pl.pallas_call(
        _combine_norm_proj_kernel,
        grid=(n // tm,),
        in_specs=in_specs + [pl.BlockSpec((1, d), lambda i: (0, 0)),
                             pl.BlockSpec((d, c), lambda i: (0, 0)),
                             pl.BlockSpec(memory_space=pl.ANY)],
        out_specs=[pl.BlockSpec((tm, d), lambda i: (i, 0)),
                   pl.BlockSpec((tm, c), lambda i: (i, 0))],
        out_shape=[jax.ShapeDtypeStruct((n, d), F32), jax.ShapeDtypeStruct((n, c), F32)],
        scratch_shapes=scratch,
        compiler_params=_cparams(("arbitrary",)),
        name="combine_norm_proj",
    )(h, route, dest3, dest3, gain.reshape(1, d), w_bf16, ys)


def _combine(h, ys, route, dest3, tm):
    n, d = h.shape
    in_specs, scratch = _combine_specs(n, d, tm)
    return pl.pallas_call(
        _combine_kernel,
        grid=(n // tm,),
        in_specs=in_specs + [pl.BlockSpec(memory_space=pl.ANY)],
        out_specs=pl.BlockSpec((tm, d), lambda i: (i, 0)),
        out_shape=jax.ShapeDtypeStruct((n, d), F32),
        scratch_shapes=scratch,
        compiler_params=_cparams(("arbitrary",)),
        name="combine",
    )(h, route, dest3, dest3, ys)


def _shifted_taps(buf, u_ref, n_taps):
    buf[TAIL:TAIL + CHUNK, :] = u_ref[...]
    taps = [buf[TAIL - j:TAIL - j + CHUNK, :] for j in range(n_taps)]
    return taps


def _carry_tail(buf):
    buf[0:TAIL, :] = buf[CHUNK:CHUNK + TAIL, :]


def _ssd_stages(z_ref, x_ref, bc_ref, dt_ref, cwx_ref, cbx_ref, cwbc_ref, cbbc_ref, dtb_ref, alog_ref,
                dskip_ref, nw_ref, hexp_ref, o_ref, xbuf, bcbuf, state):
    st = {}
    gw = SSD_INNER // SSD_GROUPS

    def init():
        @pl.when(pl.program_id(1) == 0)
        def _():
            xbuf[0:TAIL, :] = jnp.zeros((TAIL, SSD_INNER), F32)
            bcbuf[0:TAIL, :] = jnp.zeros((TAIL, SSD_INNER), F32)
            state[...] = jnp.zeros_like(state)

    def conv(buf, u_ref, w_ref, b_ref):
        taps = _shifted_taps(buf, u_ref, SSD_CONV)
        acc = b_ref[...] + taps[0] * w_ref[3:4, :]
        for j in range(1, SSD_CONV):
            acc = acc + taps[j] * w_ref[3 - j:4 - j, :]
        _carry_tail(buf)
        return _silu(acc)

    def convs():
        st["xs"] = conv(xbuf, x_ref, cwx_ref, cbx_ref)
        st["bc"] = conv(bcbuf, bc_ref, cwbc_ref, cbbc_ref)

    def decays():
        lane = _iota2((CHUNK, LANES), 1)
        dt = _softplus(dt_ref[...] + dtb_ref[...])
        adt = jnp.where(lane < SSD_HEADS, -jnp.exp(alog_ref[...]) * dt, 0.0)
        cum = _dot_exact_lhs(_tri_incl_bf16(CHUNK), adt)
        hexp = hexp_ref[...]
        cum_full = _dot_exact_rhs(cum, hexp)
        tot_full = cum_full[CHUNK - 1:CHUNK, :]
        xd = st["xs"] * _dot_exact_rhs(dt, hexp)
        st.update(cum=cum, cum_t=cum.T, tot_full=tot_full, xd=xd, xds=xd * jnp.exp(tot_full - cum_full),
                  eac=jnp.exp(cum_full), lane_lo=lane < SSD_HEAD_DIM,
                  causal=_iota2((CHUNK, CHUNK), 0) >= _iota2((CHUNK, CHUNK), 1), y_parts=[])

    def group(g):
        bc, cum, cum_t, xd = st["bc"], st["cum"], st["cum_t"], st["xd"]
        bg = bc[:, g * SSD_STATE:(g + 1) * SSD_STATE]
        cg = bc[:, (SSD_GROUPS + g) * SSD_STATE:(SSD_GROUPS + g + 1) * SSD_STATE]
        cb = _dot_nt(cg, bg)
        s_prev = state[:, g * gw:(g + 1) * gw]
        y_off = _dot(cg, s_prev) * st["eac"][:, g * gw:(g + 1) * gw]
        s_new = _dot(bg.T, st["xds"][:, g * gw:(g + 1) * gw])
        state[:, g * gw:(g + 1) * gw] = jnp.exp(st["tot_full"][:, g * gw:(g + 1) * gw]) * s_prev + s_new
        for pr in range(2):
            lo = g * gw + pr * LANES
            xd_pair = xd[:, lo:lo + LANES]
            yd = jnp.zeros((CHUNK, LANES), F32)
            for k in range(2):
                h = (lo // SSD_HEAD_DIM) + k
                diff = cum[:, h:h + 1] - cum_t[h:h + 1, :]
                decay = jnp.exp(jnp.where(st["causal"], diff, NEG_BIG))
                keep = st["lane_lo"] if k == 0 else jnp.logical_not(st["lane_lo"])
                yd = yd + _dot(cb * decay, jnp.where(keep, xd_pair, 0.0))
            st["y_parts"].append(yd + y_off[:, pr * LANES:(pr + 1) * LANES])

    def finish():
        y = jnp.concatenate(st["y_parts"], axis=1) + dskip_ref[...] * st["xs"]
        y = y * _silu(z_ref[...])
        outs = []
        for g in range(SSD_GROUPS):
            yg = y[:, g * gw:(g + 1) * gw]
            outs.append(yg * lax.rsqrt(jnp.mean(yg * yg, axis=-1, keepdims=True) + 1e-5))
        o_ref[...] = (jnp.concatenate(outs, axis=1) * nw_ref[...]).astype(o_ref.dtype)

    return [init, convs, decays] + [functools.partial(group, g) for g in range(SSD_GROUPS)] + [finish]


def _head_expand(n_heads, head_dim):
    h = jnp.arange(LANES)[:, None]
    l = jnp.arange(n_heads * head_dim)[None, :]
    return (l // head_dim == h).astype(BF16)


def _ssd_operands(proj, nc, conv_w, conv_b, dt_bias, a_log, d_skip, norm_w):
    pad16 = lambda v: jnp.pad(v.astype(F32), (0, LANES - v.shape[0])).reshape(1, LANES)
    row_spec = lambda w, blk: pl.BlockSpec((CHUNK, w), lambda b, c: (b * nc + c, blk))
    full = lambda shape: pl.BlockSpec(shape, lambda b, c: (0, 0))
    in_specs = [row_spec(SSD_INNER, 0),
                row_spec(SSD_INNER, 1),
                row_spec(SSD_INNER, 2),
                row_spec(LANES, 50),
                full((SSD_CONV, SSD_INNER)), full((1, SSD_INNER)),
                full((SSD_CONV, SSD_INNER)), full((1, SSD_INNER)),
                full((1, LANES)), full((1, LANES)),
                full((1, SSD_INNER)), full((1, SSD_INNER)),
                full((LANES, SSD_INNER))]
    operands = [proj, proj, proj, proj,
                conv_w[:, :SSD_INNER], conv_b[:SSD_INNER].reshape(1, -1),
                conv_w[:, SSD_INNER:], conv_b[SSD_INNER:].reshape(1, -1),
                pad16(dt_bias), pad16(a_log),
                jnp.repeat(d_skip.astype(F32), SSD_HEAD_DIM).reshape(1, -1), norm_w.reshape(1, -1),
                _head_expand(SSD_HEADS, SSD_HEAD_DIM)]
    scratch = [pltpu.VMEM((TAIL + CHUNK, SSD_INNER), F32),
               pltpu.VMEM((TAIL + CHUNK, SSD_INNER), F32),
               pltpu.VMEM((SSD_STATE, SSD_INNER), F32)]
    return in_specs, operands, scratch


def _rwkv_kernel(r_ref, k_ref, v_ref, lo_ref, mur_ref, muk_ref, muv_ref, mulo_ref, w0_ref, w2_ref, a0_ref,
                 a2_ref, g2_ref, kk_ref, ka_ref, rk_ref, lnw_ref, lnb_ref, o_ref,
                 rbuf, kbuf, vbuf, lobuf, state, side_work=()):
    side = iter(side_work)
    run_side = lambda: next(side, lambda: None)()
    run_side()
    c = pl.program_id(1)

    @pl.when(c == 0)
    def _():
        rbuf[0:TAIL, :] = jnp.zeros((TAIL, RWKV_DIM), F32)
        kbuf[0:TAIL, :] = jnp.zeros((TAIL, RWKV_DIM), F32)
        vbuf[0:TAIL, :] = jnp.zeros((TAIL, RWKV_DIM), F32)
        lobuf[0:TAIL, :] = jnp.zeros((TAIL, 2 * LANES), F32)
        state[...] = jnp.zeros_like(state)

    def shift(buf, u_ref, mu_ref):
        cur, prev = _shifted_taps(buf, u_ref, 2)
        _carry_tail(buf)
        return cur + mu_ref[...] * (prev - cur)

    r = shift(rbuf, r_ref, mur_ref)
    k = shift(kbuf, k_ref, muk_ref)
    v = shift(vbuf, v_ref, muv_ref)
    lo = shift(lobuf, lo_ref, mulo_ref)
    wa = lo[:, :LANES]
    w = -_softplus(-(w0_ref[...] + _dot(jnp.tanh(wa), w2_ref[...]))) - 0.5
    logw = -jnp.exp(w)
    a = _sigmoid(a0_ref[...] + _dot(wa, a2_ref[...]))
    g = _dot(_sigmoid(lo[:, LANES:]), g2_ref[...])

    lane = _iota2((CHUNK, LANES), 1)
    head0 = lane < RWKV_HEAD_DIM
    bd = (_iota2((LANES, LANES), 0) // RWKV_HEAD_DIM) == (_iota2((LANES, LANES), 1) // RWKV_HEAD_DIM)
    bd_ones = jnp.where(bd, 1.0, 0.0).astype(BF16)

    def head_sum(x):
        return jnp.dot(x.astype(BF16), bd_ones, preferred_element_type=F32)

    tri = _tri_incl_bf16(CHUNK)
    cum = _dot_exact_lhs(tri, logw)
    cume = cum - logw
    cmid = cum[CHUNK // 2 - 1:CHUNK // 2, :]
    cend = cum[CHUNK - 1:CHUNK, :]
    e_in_mid = jnp.exp(cum - cmid)
    e_ex_mid = jnp.exp(cume - cmid)
    e_mid_in = jnp.exp(cmid - cum)
    e_ex = jnp.exp(cume)
    e_in = jnp.exp(cum)
    e_end = jnp.exp(cend - cum)
    e_tot = jnp.exp(cend)

    row = _iota2((CHUNK, CHUNK), 0)
    col = _iota2((CHUNK, CHUNK), 1)
    strict = row > col
    incl = row >= col
    zeros = jnp.zeros((CHUNK, LANES), F32)

    n_blocks = RWKV_DIM // LANES
    halves = (head0, jnp.logical_not(head0))
    blk = []
    for p in range(n_blocks):
        sl = slice(p * LANES, (p + 1) * LANES)
        rp, vp, ap = r[:, sl], v[:, sl], a[:, sl]
        kkp = k[:, sl] * kk_ref[:, sl]
        kkp = kkp / jnp.maximum(jnp.sqrt(head_sum(kkp * kkp)), 1e-12)
        kp = k[:, sl] * (1.0 + (ap - 1.0) * ka_ref[:, sl])
        aap = -kkp
        bp = kkp * ap
        a_mid = aap * e_ex_mid[:, sl]
        r_mid = rp * e_in_mid[:, sl]
        lhs = jnp.concatenate([jnp.where(head0, a_mid, 0.0), jnp.where(head0, 0.0, a_mid),
                               jnp.where(head0, r_mid, 0.0), jnp.where(head0, 0.0, r_mid)], axis=0)
        rhs = jnp.concatenate([bp * e_mid_in[:, sl], kp * e_mid_in[:, sl]], axis=0)
        blk.append(dict(sl=sl, rp=rp, vp=vp, kp=kp, a_abs=aap * e_ex[:, sl], r_abs=rp * e_in[:, sl],
                        b_end=bp * e_end[:, sl], k_end=kp * e_end[:, sl], prod=_dot_nt(lhs, rhs)))
    run_side()

    heads = []
    for p in range(n_blocks):
        prod = blk[p]["prod"]
        for hh in range(2):
            heads.append(dict(
                p=p, keep=halves[hh], vm=jnp.where(halves[hh], blk[p]["vp"], 0.0),
                a_ab=jnp.where(strict, prod[hh * CHUNK:(hh + 1) * CHUNK, :CHUNK], 0.0),
                a_ak=jnp.where(strict, prod[hh * CHUNK:(hh + 1) * CHUNK, CHUNK:], 0.0),
                m_rb=jnp.where(incl, prod[(2 + hh) * CHUNK:(3 + hh) * CHUNK, :CHUNK], 0.0),
                m_rk=jnp.where(incl, prod[(2 + hh) * CHUNK:(3 + hh) * CHUNK, CHUNK:], 0.0)))

    eye = jnp.where(row == col, 1.0, 0.0)
    n_levels = int(math.log2(CHUNK)) - 1
    ts = [eye + hd["a_ab"] for hd in heads]
    xs = [hd["a_ab"].astype(BF16) for hd in heads]
    avs = [_dot(hd["a_ak"], hd["vm"]) for hd in heads]
    run_side()
    xs = [jnp.dot(x, x, preferred_element_type=F32).astype(BF16) for x in xs]
    for _ in range(n_levels - 1):
        run_side()
        zs = [jnp.dot(x, jnp.concatenate([x, t.astype(BF16)], axis=1), preferred_element_type=F32)
              for x, t in zip(xs, ts)]
        xs = [z[:, :CHUNK].astype(BF16) for z in zs]
        ts = [t + z[:, CHUNK:] for t, z in zip(ts, zs)]
    ts = [t + jnp.dot(x, t.astype(BF16), preferred_element_type=F32) for x, t in zip(xs, ts)]
    for _ in side:
        _()

    wmats = [_dot(t, jnp.concatenate([jnp.where(hd["keep"], blk[hd["p"]]["a_abs"], 0.0), av], axis=1))
             for t, hd, av in zip(ts, heads, avs)]
    outs = [_dot(jnp.concatenate([hd["m_rb"], hd["m_rk"]], axis=1),
                 jnp.concatenate([jnp.concatenate([wm[:, LANES:], wm[:, :LANES]], axis=1),
                                  jnp.concatenate([hd["vm"], zeros], axis=1)], axis=0))
            for wm, hd in zip(wmats, heads)]

    zts = []
    for p in range(n_blocks):
        w0h, w1h = wmats[2 * p], wmats[2 * p + 1]
        ui = w0h[:, LANES:] + w1h[:, LANES:]
        a_eff = w0h[:, :LANES] + w1h[:, :LANES]
        zts.append(_dot_tn(jnp.concatenate([jnp.concatenate([ui, a_eff], axis=1),
                                            jnp.concatenate([blk[p]["vp"], zeros], axis=1)], axis=0),
                           jnp.concatenate([blk[p]["b_end"], blk[p]["k_end"]], axis=0)))

    ys = []
    for p in range(n_blocks):
        s0 = state[p]
        o0, o1 = outs[2 * p], outs[2 * p + 1]
        r_eff = blk[p]["r_abs"] + o0[:, LANES:] + o1[:, LANES:]
        ys.append(_dot_nt(r_eff, s0) + o0[:, :LANES] + o1[:, :LANES])
        h_intra = jnp.where(bd, zts[p][:LANES, :], 0.0)
        g_corr = jnp.where(bd, zts[p][LANES:, :], 0.0)
        state[p] = s0 * e_tot[:, blk[p]["sl"]] + _dot(s0, g_corr) + h_intra

    means = [head_sum(y) * (1.0 / RWKV_HEAD_DIM) for y in ys]
    devs = [y - m for y, m in zip(ys, means)]
    vars_ = [head_sum(dv * dv) * (1.0 / RWKV_HEAD_DIM) for dv in devs]
    for p in range(n_blocks):
        sl = blk[p]["sl"]
        yn = devs[p] * lax.rsqrt(vars_[p] + RWKV_GN_EPS) * lnw_ref[:, sl] + lnb_ref[:, sl]
        bonus = head_sum(blk[p]["rp"] * blk[p]["kp"] * rk_ref[:, sl])
        o_ref[:, sl] = ((yn + bonus * blk[p]["vp"]) * g[:, sl]).astype(o_ref.dtype)


def _rwkv_operands(proj, nc, mu, w0, w2, a0, a2, g2, k_k, k_a, r_k, ln_w, ln_b):
    d = RWKV_DIM
    row_spec = lambda w, blk: pl.BlockSpec((CHUNK, w), lambda b, c: (b * nc + c, blk))
    full = lambda shape: pl.BlockSpec(shape, lambda b, c: (0,) * len(shape))
    vec = lambda x: x.astype(F32).reshape(1, -1)
    w2p = jnp.concatenate([w2, jnp.zeros((AAA_LORA, d), w2.dtype)], axis=0).astype(BF16)
    a2p = jnp.concatenate([jnp.zeros((DECAY_LORA, d), a2.dtype), a2], axis=0).astype(BF16)
    in_specs = [row_spec(d, 3), row_spec(d, 4), row_spec(d, 5), row_spec(2 * LANES, 24),
                full((1, d)), full((1, d)), full((1, d)), full((1, 2 * LANES)),
                full((1, d)), full((LANES, d)), full((1, d)), full((LANES, d)), full((LANES, d)),
                full((1, d)), full((1, d)), full((1, d)), full((1, d)), full((1, d))]
    operands = [proj, proj, proj, proj,
                vec(mu[:d]), vec(mu[d:2 * d]), vec(mu[2 * d:3 * d]), vec(mu[3 * d:]),
                vec(w0), w2p, vec(a0), a2p, g2.astype(BF16),
                vec(k_k), vec(k_a), vec(r_k), vec(ln_w), vec(ln_b)]
    scratch = [pltpu.VMEM((TAIL + CHUNK, d), F32), pltpu.VMEM((TAIL + CHUNK, d), F32),
               pltpu.VMEM((TAIL + CHUNK, d), F32), pltpu.VMEM((TAIL + CHUNK, 2 * LANES), F32),
               pltpu.VMEM((d // LANES, LANES, LANES), F32)]
    return in_specs, operands, scratch


def _mixers_kernel(*refs, n_ssd_in, n_rwkv_in, n_ssd_scratch):
    ssd_in = refs[:n_ssd_in]
    rwkv_in = refs[n_ssd_in:n_ssd_in + n_rwkv_in]
    o_ssd, o_rwkv = refs[n_ssd_in + n_rwkv_in:n_ssd_in + n_rwkv_in + 2]
    scratch = refs[n_ssd_in + n_rwkv_in + 2:]
    _rwkv_kernel(*rwkv_in, o_rwkv, *scratch[n_ssd_scratch:],
                 side_work=_ssd_stages(*ssd_in, o_ssd, *scratch[:n_ssd_scratch]))


def _mixers(proj, batch, seq, ssd_params, rwkv_params):
    nc = seq // CHUNK
    n = batch * seq
    s_specs, s_ops, s_scratch = _ssd_operands(proj, nc, *ssd_params)
    r_specs, r_ops, r_scratch = _rwkv_operands(proj, nc, *rwkv_params)
    out_spec = lambda w: pl.BlockSpec((CHUNK, w), lambda b, c: (b * nc + c, 0))
    return pl.pallas_call(
        functools.partial(_mixers_kernel, n_ssd_in=len(s_specs), n_rwkv_in=len(r_specs),
                          n_ssd_scratch=len(s_scratch)),
        grid=(batch, nc),
        in_specs=s_specs + r_specs,
        out_specs=[out_spec(SSD_INNER), out_spec(RWKV_DIM)],
        out_shape=[jax.ShapeDtypeStruct((n, SSD_INNER), BF16), jax.ShapeDtypeStruct((n, RWKV_DIM), BF16)],
        scratch_shapes=s_scratch + r_scratch,
        compiler_params=_cparams(("arbitrary", "arbitrary")),
        name="mixers",
    )(*s_ops, *r_ops)


def _route(logits):
    lane = _iota2(logits.shape, 1)
    lanef = lane.astype(F32)
    big = float(LANES)

    def first_max(x):
        m = jnp.max(x, axis=-1, keepdims=True)
        idx = jnp.min(jnp.where(x == m, lanef, big), axis=-1, keepdims=True)
        return m, idx

    cl = jnp.where(lane < EXPERT_GROUPS, logits, NEG_BIG)
    cmax, grp = first_max(cl)
    p_group = 1.0 / jnp.sum(jnp.exp(cl - cmax), axis=-1, keepdims=True)
    lo = EXPERT_GROUPS + grp * EXPERTS_PER_GROUP
    fl = jnp.where((lanef >= lo) & (lanef < lo + EXPERTS_PER_GROUP), logits, NEG_BIG)
    m0, i0 = first_max(fl)
    m1, i1 = first_max(jnp.where(lanef == i0, NEG_BIG, fl))
    e1 = jnp.exp(m1 - m0)
    g0 = p_group / (1.0 + e1)
    g1 = p_group * e1 / (1.0 + e1)
    return jnp.where(lane == 0, i0 - EXPERT_GROUPS,
                     jnp.where(lane == 1, i1 - EXPERT_GROUPS,
                               jnp.where(lane == 2, g0, jnp.where(lane == 3, g1, 0.0))))


def _outproj_router_kernel(*refs, n_in):
    ys = refs[:n_in]
    ws = refs[n_in:2 * n_in]
    h_ref, g_ref, wr_ref, br_ref, hnew_ref, hf_ref, route_ref = refs[2 * n_in:]
    acc = h_ref[...]
    for y_ref, w_ref in zip(ys, ws):
        acc = acc + jnp.dot(y_ref[...], w_ref[...], preferred_element_type=F32)
    hnew_ref[...] = acc
    hf = _rms(acc, g_ref[...])
    hf_ref[...] = hf
    route_ref[...] = _route(_dot_f32(hf, wr_ref[...]) + br_ref[...])


def _outproj_router(ys, ws, h, gain, w_router, b_router, tm):
    n, d = h.shape
    n_in = len(ys)
    in_specs = ([pl.BlockSpec((tm, y.shape[1]), lambda i: (i, 0)) for y in ys]
                + [pl.BlockSpec(w.shape, lambda i: (0, 0)) for w in ws]
                + [pl.BlockSpec((tm, d), lambda i: (i, 0)),
                   pl.BlockSpec((1, d), lambda i: (0, 0)),
                   pl.BlockSpec((d, LANES), lambda i: (0, 0)),
                   pl.BlockSpec((1, LANES), lambda i: (0, 0))])
    return pl.pallas_call(
        functools.partial(_outproj_router_kernel, n_in=n_in),
        grid=(n // tm,),
        in_specs=in_specs,
        out_specs=[pl.BlockSpec((tm, d), lambda i: (i, 0)),
                   pl.BlockSpec((tm, d), lambda i: (i, 0)),
                   pl.BlockSpec((tm, LANES), lambda i: (i, 0))],
        out_shape=[jax.ShapeDtypeStruct((n, d), F32), jax.ShapeDtypeStruct((n, d), F32),
                   jax.ShapeDtypeStruct((n, LANES), F32)],
        compiler_params=_cparams(("arbitrary",)),
        name="outproj_router",
    )(*ys, *ws, h, gain.reshape(1, d), w_router, b_router)


def _router_weights(w_coarse, b_coarse, w_fine, b_fine):
    d = w_coarse.shape[0]
    wf = jnp.transpose(w_fine, (1, 0, 2)).reshape(d, N_EXPERTS)
    w = jnp.concatenate([w_coarse, wf], axis=1).astype(F32)
    b = jnp.concatenate([b_coarse, b_fine.reshape(N_EXPERTS)]).astype(F32)
    pad = LANES - w.shape[1]
    return jnp.pad(w, ((0, 0), (0, pad))), jnp.pad(b, (0, pad)).reshape(1, LANES)


def _moe_plan(route, tm):
    n = route.shape[0]
    a = n * TOP_K
    n_blocks = a // MOE_BLOCK + N_EXPERTS
    e_flat = route[:, :TOP_K].astype(jnp.int32).reshape(a)
    seg = MOE_BLOCK
    onehot = (e_flat[:, None] == jnp.arange(N_EXPERTS, dtype=jnp.int32)[None, :]).astype(F32)
    onehot = onehot.reshape(a // seg, seg, N_EXPERTS)
    tri = jnp.tril(jnp.ones((seg, seg), F32))
    within = jnp.einsum("ij,bjk->bik", tri, onehot)
    tot = within[:, -1, :]
    offs = jnp.cumsum(tot, axis=0) - tot
    rank = (jnp.sum(onehot * (within + offs[:, None, :]), axis=-1) - 1.0).astype(jnp.int32).reshape(a)
    counts = (offs[-1] + tot[-1]).astype(jnp.int32)
    padded = (counts + MOE_BLOCK - 1) // MOE_BLOCK * MOE_BLOCK
    pad_end = jnp.cumsum(padded)
    pad_start = pad_end - padded
    start_of = jnp.sum(jnp.where(e_flat[:, None] == jnp.arange(N_EXPERTS, dtype=jnp.int32)[None, :],
                                 pad_start[None, :], 0), axis=1)
    dest = (start_of + rank).astype(jnp.int32)
    block_start = jnp.arange(n_blocks, dtype=jnp.int32) * MOE_BLOCK
    block_expert = jnp.minimum(jnp.sum((pad_end[None, :] <= block_start[:, None]).astype(jnp.int32), axis=1),
                               N_EXPERTS - 1).astype(jnp.int32)
    n_used = (pad_end[-1] // MOE_BLOCK).astype(jnp.int32).reshape(1)
    pad_lo = (pad_start + counts).astype(jnp.int32)
    return dict(dest3=dest.reshape(n // tm, 1, TOP_K * tm), block_expert=block_expert, n_used=n_used,
                pad_lo=pad_lo, pad_hi=pad_end.astype(jnp.int32), n_blocks=n_blocks)


def _dispatch_kernel(padlo_ref, padhi_ref, nused_ref, hf_ref, dest_ref, xs_hbm, zblk, sem, zsem):
    i = pl.program_id(0)
    tm = hf_ref.shape[0]

    @pl.when(i == 0)
    def _():
        zblk[...] = jnp.zeros_like(zblk)

        def per_expert(e, carry):
            def zero_row(rw, c2):
                pltpu.make_async_copy(zblk.at[pl.ds(0, 1), :], xs_hbm.at[pl.ds(rw, 1), :], zsem).start()
                return c2
            lax.fori_loop(padlo_ref[e], padhi_ref[e], zero_row, 0)

            def wait_row(rw, c2):
                pltpu.make_async_copy(zblk.at[pl.ds(0, 1), :], xs_hbm.at[pl.ds(rw, 1), :], zsem).wait()
                return c2
            lax.fori_loop(padlo_ref[e], padhi_ref[e], wait_row, 0)
            return carry
        lax.fori_loop(0, N_EXPERTS, per_expert, 0)

        n_blocks = xs_hbm.shape[0] // MOE_BLOCK

        def block_copy(b):
            return pltpu.make_async_copy(
                zblk, xs_hbm.at[pl.ds(pl.multiple_of(b * MOE_BLOCK, MOE_BLOCK), MOE_BLOCK), :], zsem)

        def zero_block(b, carry):
            block_copy(b).start()
            return carry
        lax.fori_loop(nused_ref[0], n_blocks, zero_block, 0)

        def wait_block(b, carry):
            block_copy(b).wait()
            return carry
        lax.fori_loop(nused_ref[0], n_blocks, wait_block, 0)

    for t in range(tm):
        for c in range(TOP_K):
            pltpu.make_async_copy(hf_ref.at[pl.ds(t, 1), :],
                                  xs_hbm.at[pl.ds(dest_ref[0, 0, TOP_K * t + c], 1), :], sem).start()
    for c in range(TOP_K):
        pltpu.make_async_copy(hf_ref, xs_hbm.at[pl.ds(0, tm), :], sem).wait()


def _dispatch(hf, plan, tm):
    n, d = hf.shape
    slots = plan["n_blocks"] * MOE_BLOCK
    grid_spec = pltpu.PrefetchScalarGridSpec(
        num_scalar_prefetch=3,
        grid=(n // tm,),
        in_specs=[pl.BlockSpec((tm, d), lambda i, lo, hi, nu: (i, 0)),
                  pl.BlockSpec((1, 1, TOP_K * tm), lambda i, lo, hi, nu: (i, 0, 0), memory_space=pltpu.SMEM)],
        out_specs=pl.BlockSpec(memory_space=pl.ANY),
        scratch_shapes=[pltpu.VMEM((MOE_BLOCK, d), F32), pltpu.SemaphoreType.DMA(()),
                        pltpu.SemaphoreType.DMA(())])
    return pl.pallas_call(
        _dispatch_kernel,
        grid_spec=grid_spec,
        out_shape=jax.ShapeDtypeStruct((slots, d), F32),
        compiler_params=_cparams(("arbitrary",)),
        name="moe_dispatch",
    )(plan["pad_lo"], plan["pad_hi"], plan["n_used"], hf, plan["dest3"])


def _experts_kernel(bexp_ref, nused_ref, x_ref, wg_ref, wu_ref, wd_ref, y_ref, wg_s, wu_s, wd_s):
    i = pl.program_id(0)
    active = i < nused_ref[0]

    @pl.when(active & ((i == 0) | (bexp_ref[i] != bexp_ref[jnp.maximum(i - 1, 0)])))
    def _():
        wg_s[...] = wg_ref[0].astype(BF16)
        wu_s[...] = wu_ref[0].astype(BF16)
        wd_s[...] = wd_ref[0].astype(BF16)

    @pl.when(active)
    def _():
        x = x_ref[...].astype(BF16)
        hg = jnp.dot(x, wg_s[...], preferred_element_type=F32)
        hu = jnp.dot(x, wu_s[...], preferred_element_type=F32)
        hb = (_silu(hg) * hu).astype(BF16)
        y_ref[...] = jnp.dot(hb, wd_s[...], preferred_element_type=F32)

    @pl.when(i >= nused_ref[0])
    def _():
        y_ref[...] = jnp.zeros_like(y_ref)


def _experts(xs, plan, layer, wg, wu, wd):
    slots, d = xs.shape
    n_blocks = plan["n_blocks"]
    used = lambda i, nu: jnp.minimum(i, nu[0] - 1)
    w_idx = lambda i, be, nu: (layer, be[used(i, nu)], 0, 0)
    grid_spec = pltpu.PrefetchScalarGridSpec(
        num_scalar_prefetch=2,
        grid=(n_blocks,),
        in_specs=[pl.BlockSpec((MOE_BLOCK, d), lambda i, be, nu: (used(i, nu), 0)),
                  pl.BlockSpec((None, 1, d, EXPERT_HIDDEN), w_idx),
                  pl.BlockSpec((None, 1, d, EXPERT_HIDDEN), w_idx),
                  pl.BlockSpec((None, 1, EXPERT_HIDDEN, d), w_idx)],
        out_specs=pl.BlockSpec((MOE_BLOCK, d), lambda i, be, nu: (i, 0)),
        scratch_shapes=[pltpu.VMEM((d, EXPERT_HIDDEN), BF16), pltpu.VMEM((d, EXPERT_HIDDEN), BF16),
                        pltpu.VMEM((EXPERT_HIDDEN, d), BF16)])
    return pl.pallas_call(
        _experts_kernel,
        grid_spec=grid_spec,
        out_shape=jax.ShapeDtypeStruct((slots, d), F32),
        compiler_params=_cparams(("arbitrary",)),
        name="moe_experts",
    )(plan["block_expert"], plan["n_used"], xs, wg, wu, wd)


def _swa_kernel(q_ref, kvp_ref, kvc_ref, qg_ref, kg_ref, slope_ref, sink_ref, o_ref):
    jb = pl.program_id(1)
    blk = CHUNK
    qi = _iota2((blk, blk), 0)
    kj = _iota2((blk, blk), 1)
    from_prev = kj > qi
    deltaf = jnp.where(from_prev, qi + blk - kj, qi - kj).astype(F32)
    no_prev = jnp.where(from_prev, jnp.where(jb > 0, 0.0, NEG_BIG), 0.0)
    scale = HEAD_DIM ** -0.5

    bd = (_iota2((LANES, LANES), 0) // HEAD_DIM) == (_iota2((LANES, LANES), 1) // HEAD_DIM)
    bd_ones = jnp.where(bd, 1.0, 0.0).astype(BF16)

    def head_rms(x, gain):
        sq = x * x
        s1 = sq.astype(BF16)
        s2 = (sq - s1.astype(F32)).astype(BF16)
        ms = (jnp.dot(s1, bd_ones, preferred_element_type=F32)
              + jnp.dot(s2, bd_ones, preferred_element_type=F32)) * (1.0 / HEAD_DIM)
        return x * lax.rsqrt(ms + NORM_EPS) * gain

    n_kv_blk = KV_DIM // LANES
    kv = jnp.concatenate([kvp_ref[...], kvc_ref[...]], axis=0)
    lane_kv = _iota2((kv.shape[0], LANES), 1)
    kv_half = (lane_kv < HEAD_DIM, lane_kv >= HEAD_DIM)
    lane_q = _iota2((blk, LANES), 1)
    q_half = (lane_q < HEAD_DIM, lane_q >= HEAD_DIM)
    kn = [head_rms(kv[:, j * LANES:(j + 1) * LANES], kg_ref[...]) for j in range(n_kv_blk)]
    vb = [kv[:, KV_DIM + j * LANES:KV_DIM + (j + 1) * LANES] for j in range(n_kv_blk)]
    kn_sw = [pltpu.roll(x, HEAD_DIM, 1) for x in kn]
    vb_sw = [pltpu.roll(x, HEAD_DIM, 1) for x in vb]
    v_same = [jnp.where(kv_half[g % 2], vb[g // 2], 0.0) for g in range(KV_HEADS)]
    v_swap = [jnp.where(kv_half[1 - g % 2], vb_sw[g // 2], 0.0) for g in range(KV_HEADS)]
    qn = [head_rms(q_ref[:, j * LANES:(j + 1) * LANES], qg_ref[...]) for j in range(Q_DIM // LANES)]

    rep = lambda x: jnp.concatenate([x] * Q_PER_KV, axis=0)
    from_prev4, delta4 = rep(from_prev), rep(deltaf)
    neg4 = [rep(no_prev)] + [None] * (SWA_QBLOCKS - 1)
    col = lambda ref, hs: jnp.concatenate([jnp.broadcast_to(ref[:, h:h + 1], (blk, 1)) for h in hs], axis=0)
    order = lambda g: [g * Q_PER_KV + g % 2, g * Q_PER_KV + g % 2 + 2,
                       g * Q_PER_KV + 1 - g % 2, g * Q_PER_KV + 3 - g % 2]
    units = [(u, g) for u in range(SWA_QBLOCKS) for g in range(KV_HEADS)]
    keys = lambda x, u: x[u * blk:(u + 2) * blk]
    qrow = lambda x, u: x[u * blk:(u + 1) * blk]

    qms = [[jnp.where(q_half[h % 2], qrow(qn[h // 2], u), 0.0) for h in order(g)] for u, g in units]
    scs = [jnp.concatenate([_dot_nt(jnp.concatenate(qm[:2], axis=0), keys(kn[g // 2], u)),
                            _dot_nt(jnp.concatenate(qm[2:], axis=0), keys(kn_sw[g // 2], u))], axis=0)
           for qm, (u, g) in zip(qms, units)]
    sinks = [col(sink_ref, order(g)) for u, g in units]
    ss = []
    for sc, (u, g) in zip(scs, units):
        s = jnp.where(from_prev4, sc[:, :blk], sc[:, blk:]) * scale - col(slope_ref, order(g)) * delta4
        ss.append(s if neg4[u] is None else s + neg4[u])
    ms = [jnp.maximum(jnp.max(s, axis=-1, keepdims=True), sk) for s, sk in zip(ss, sinks)]
    ps = [jnp.exp(s - m) for s, m in zip(ss, ms)]
    invs = [1.0 / (jnp.sum(p, axis=-1, keepdims=True) + jnp.exp(sk - m)) for p, sk, m in zip(ps, sinks, ms)]
    pcats = [jnp.concatenate([jnp.where(from_prev4, p, 0.0), jnp.where(from_prev4, 0.0, p)], axis=1) for p in ps]
    outs = [jnp.concatenate([_dot(pc[:2 * blk], keys(v_same[g], u)), _dot(pc[2 * blk:], keys(v_swap[g], u))],
                            axis=0) * inv
            for pc, inv, (u, g) in zip(pcats, invs, units)]
    for o, (u, g) in zip(outs, units):
        hs = order(g)
        for j in sorted({h // 2 for h in hs}):
            pair = sum(o[idx * blk:(idx + 1) * blk] for idx, h in enumerate(hs) if h // 2 == j)
            o_ref[u * blk:(u + 1) * blk, j * LANES:(j + 1) * LANES] = pair.astype(o_ref.dtype)


def _swa(proj, batch, seq, q_gain, k_gain, sinks):
    nbk = seq // CHUNK
    qrows = SWA_QBLOCKS * CHUNK
    nsteps = seq // qrows
    n = batch * seq
    slopes = (2.0 ** (-8.0 * jnp.arange(1, ATT_HEADS + 1, dtype=F32) / ATT_HEADS))
    pad = lambda v: jnp.pad(v.astype(F32), (0, LANES - v.shape[0])).reshape(1, LANES)
    pair = lambda v: jnp.tile(v.astype(F32), LANES // HEAD_DIM).reshape(1, LANES)
    full = lambda shape: pl.BlockSpec(shape, lambda b, j: (0, 0))
    kvw = 2 * KV_DIM
    return pl.pallas_call(
        _swa_kernel,
        grid=(batch, nsteps),
        in_specs=[pl.BlockSpec((qrows, Q_DIM), lambda b, j: (b * nsteps + j, 0)),
                  pl.BlockSpec((CHUNK, kvw),
                               lambda b, j: (b * nbk + jnp.maximum(SWA_QBLOCKS * j - 1, 0), Q_DIM // kvw)),
                  pl.BlockSpec((qrows, kvw), lambda b, j: (b * nsteps + j, Q_DIM // kvw)),
                  full((1, LANES)), full((1, LANES)), full((1, LANES)), full((1, LANES))],
        out_specs=pl.BlockSpec((qrows, Q_DIM), lambda b, j: (b * nsteps + j, 0)),
        out_shape=jax.ShapeDtypeStruct((n, Q_DIM), BF16),
        compiler_params=_cparams(("arbitrary", "arbitrary")),
        name="swa",
    )(proj, proj, proj, pair(q_gain), pair(k_gain), pad(slopes), pad(sinks))


def _even_in_weight(w):
    rw = SSD_COLS
    cols = jnp.concatenate([w[:, :SSD_INNER + SSD_CONV_DIM], w[:, rw:],
                            w[:, SSD_INNER + SSD_CONV_DIM:SSD_COLS]], axis=1)
    return jnp.pad(cols, ((0, 0), (0, EVEN_COLS_PAD - cols.shape[1]))).astype(BF16)


def kernel(x, ln_mix, ln_ffn, e_w_in, e_w_out, ssd_conv_w, ssd_conv_b, ssd_dt_bias, ssd_a_log, ssd_d, ssd_norm,
           rwkv_mu, rwkv_w0, rwkv_w2, rwkv_a0, rwkv_a2, rwkv_g2, rwkv_k_k, rwkv_k_a, rwkv_r_k, rwkv_ln_w,
           rwkv_ln_b, o_w_in, o_w_out, attn_q_norm, attn_k_norm, attn_sinks, moe_w_coarse, moe_b_coarse,
           moe_w_fine, moe_b_fine, moe_w_gate, moe_w_up, moe_w_down):
    batch, seq, d = x.shape
    n = batch * seq
    tm = min(512, n)
    h = x.reshape(n, d)

    def moe(layer, hf, route):
        plan = _moe_plan(route, tm)
        xs = _dispatch(hf, plan, tm)
        ys = _experts(xs, plan, layer, moe_w_gate, moe_w_up, moe_w_down)
        return ys, plan["dest3"]

    def router_w(layer):
        return _router_weights(moe_w_coarse[layer], moe_b_coarse[layer], moe_w_fine[layer], moe_b_fine[layer])

    proj = _norm_proj(h, ln_mix[0], _even_in_weight(e_w_in[0]), tm, EVEN_COLS_PAD // 3)
    y_ssd, y_rwkv = _mixers(
        proj, batch, seq,
        (ssd_conv_w[0], ssd_conv_b[0], ssd_dt_bias[0], ssd_a_log[0], ssd_d[0], ssd_norm[0]),
        (rwkv_mu[0], rwkv_w0[0], rwkv_w2[0], rwkv_a0[0], rwkv_a2[0], rwkv_g2[0], rwkv_k_k[0], rwkv_k_a[0],
         rwkv_r_k[0].reshape(-1), rwkv_ln_w[0], rwkv_ln_b[0]))
    w_out = e_w_out[0].astype(BF16)
    wr, br = router_w(0)
    h, hf, route = _outproj_router([y_ssd, y_rwkv], [w_out[:SSD_INNER], w_out[SSD_INNER:]], h, ln_ffn[0],
                                   wr, br, tm)
    ys, dest3 = moe(0, hf, route)

    h, proj = _combine_norm_proj(h, ys, route, dest3, ln_mix[1], o_w_in[0].astype(BF16), tm)
    att = _swa(proj, batch, seq, attn_q_norm[0], attn_k_norm[0], attn_sinks[0])
    wr, br = router_w(1)
    h, hf, route = _outproj_router([att], [o_w_out[0].astype(BF16)], h, ln_ffn[1], wr, br, tm)
    ys, dest3 = moe(1, hf, route)
    out = _combine(h, ys, route, dest3, tm)
    return out.reshape(batch, seq, d)
```

```python
import functools
import math

import jax
import jax.numpy as jnp
from jax import lax
from jax.experimental import pallas as pl
from jax.experimental.pallas import tpu as pltpu

F32 = jnp.float32
BF16 = jnp.bfloat16

D_MODEL = 1024
SSD_HEADS = 16
SSD_HEAD_DIM = 64
SSD_INNER = 1024
SSD_GROUPS = 4
SSD_STATE = 128
SSD_CONV = 4
SSD_CONV_DIM = 2048
SSD_COLS = 3088
RWKV_HEADS = 16
RWKV_HEAD_DIM = 64
RWKV_DIM = 1024
DECAY_LORA = 64
AAA_LORA = 64
GATE_LORA = 128
RWKV_GN_EPS = 64e-5
ATT_HEADS = 16
KV_HEADS = 4
Q_PER_KV = 4
HEAD_DIM = 64
Q_DIM = 1024
KV_DIM = 256
WINDOW = 128
EXPERT_GROUPS = 4
EXPERTS_PER_GROUP = 8
N_EXPERTS = 32
TOP_K = 2
EXPERT_HIDDEN = 512
MOE_BLOCK = 256
NORM_EPS = 1e-6

LANES = 128
CHUNK = 128
TAIL = 8
DMA_UNROLL = 8
INVERT_CHUNK = 1024
SWA_QBLOCKS = 4
EVEN_COLS_PAD = 6528
VMEM_LIMIT = 56 * 1024 * 1024
NEG_BIG = -1e30
assert WINDOW == CHUNK


def _cparams(sem):
    return pltpu.CompilerParams(dimension_semantics=sem, vmem_limit_bytes=VMEM_LIMIT)


def _dot(a, b):
    return jnp.dot(a.astype(BF16), b.astype(BF16), preferred_element_type=F32)


def _dot_nt(a, b):
    return lax.dot_general(a.astype(BF16), b.astype(BF16), (((1,), (1,)), ((), ())),
                           preferred_element_type=F32)


def _dot_tn(a, b):
    return lax.dot_general(a.astype(BF16), b.astype(BF16), (((0,), (0,)), ((), ())),
                           preferred_element_type=F32)


def _split3(x):
    x1 = x.astype(BF16)
    r1 = x - x1.astype(F32)
    x2 = r1.astype(BF16)
    x3 = (r1 - x2.astype(F32)).astype(BF16)
    return x1, x2, x3


def _dot_exact_lhs(m_bf16, x):
    x1, x2, x3 = _split3(x)
    f = lambda p: jnp.dot(m_bf16, p, preferred_element_type=F32)
    return f(x1) + f(x2) + f(x3)


def _dot_exact_rhs(x, m_bf16):
    x1, x2, x3 = _split3(x)
    f = lambda p: jnp.dot(p, m_bf16, preferred_element_type=F32)
    return f(x1) + f(x2) + f(x3)


def _dot_f32(x, w):
    x1, x2, _ = _split3(x)
    w1, w2, _ = _split3(w)
    f = lambda p, q: jnp.dot(p, q, preferred_element_type=F32)
    return f(x1, w1) + (f(x1, w2) + f(x2, w1))


def _sigmoid(x):
    return 1.0 / (1.0 + jnp.exp(-x))


def _silu(x):
    return x * _sigmoid(x)


def _softplus(x):
    return jnp.maximum(x, 0.0) + jnp.log(1.0 + jnp.exp(-jnp.abs(x)))


def _rms(x, gain):
    return x * lax.rsqrt(jnp.mean(x * x, axis=-1, keepdims=True) + NORM_EPS) * gain


def _iota2(shape, dim):
    return lax.broadcasted_iota(jnp.int32, shape, dim)


def _tri_incl_bf16(n):
    return jnp.where(_iota2((n, n), 0) >= _iota2((n, n), 1), 1.0, 0.0).astype(BF16)


def _norm_proj_kernel(h_ref, g_ref, w_ref, o_ref):
    o_ref[...] = _dot(_rms(h_ref[...], g_ref[...]), w_ref[...])


def _norm_proj(h, gain, w_bf16, tm, tn):
    n, d = h.shape
    c = w_bf16.shape[1]
    return pl.pallas_call(
        _norm_proj_kernel,
        grid=(c // tn, n // tm),
        in_specs=[pl.BlockSpec((tm, d), lambda j, i: (i, 0)),
                  pl.BlockSpec((1, d), lambda j, i: (0, 0)),
                  pl.BlockSpec((d, tn), lambda j, i: (0, j))],
        out_specs=pl.BlockSpec((tm, tn), lambda j, i: (i, j)),
        out_shape=jax.ShapeDtypeStruct((n, c), F32),
        compiler_params=_cparams(("arbitrary", "arbitrary")),
        name="norm_proj",
    )(h, gain.reshape(1, d), w_bf16)


def _combine_rows(h_ref, y0_ref, y1_ref, route_ref):
    r = route_ref[...]
    return h_ref[...] + r[:, 2:3] * y0_ref[...] + r[:, 3:4] * y1_ref[...]


def _combine_norm_proj_kernel(h_ref, y0_ref, y1_ref, route_ref, g_ref, w_ref, hnew_ref, o_ref):
    x = _combine_rows(h_ref, y0_ref, y1_ref, route_ref)
    hnew_ref[...] = x
    o_ref[...] = _dot(_rms(x, g_ref[...]), w_ref[...])


def _combine_kernel(h_ref, y0_ref, y1_ref, route_ref, o_ref):
    o_ref[...] = _combine_rows(h_ref, y0_ref, y1_ref, route_ref)


def _combine_specs(n, d, tm):
    nt = n // tm
    return [pl.BlockSpec((tm, d), lambda i: (i, 0)),
            pl.BlockSpec((tm, d), lambda i: (i, 0)),
            pl.BlockSpec((tm, d), lambda i: (nt + i, 0)),
            pl.BlockSpec((tm, LANES), lambda i: (i, 0))]


def _combine_norm_proj(h, y2, route, gain, w_bf16, tm):
    n, d = h.shape
    c = w_bf16.shape[1]
    return pl.pallas_call(
        _combine_norm_proj_kernel,
        grid=(n // tm,),
        in_specs=_combine_specs(n, d, tm) + [pl.BlockSpec((1, d), lambda i: (0, 0)),
                                             pl.BlockSpec((d, c), lambda i: (0, 0))],
        out_specs=[pl.BlockSpec((tm, d), lambda i: (i, 0)),
                   pl.BlockSpec((tm, c), lambda i: (i, 0))],
        out_shape=[jax.ShapeDtypeStruct((n, d), F32), jax.ShapeDtypeStruct((n, c), F32)],
        compiler_params=_cparams(("arbitrary",)),
        name="combine_norm_proj",
    )(h, y2, y2, route, gain.reshape(1, d), w_bf16)


def _combine(h, y2, route, tm):
    n, d = h.shape
    return pl.pallas_call(
        _combine_kernel,
        grid=(n // tm,),
        in_specs=_combine_specs(n, d, tm),
        out_specs=pl.BlockSpec((tm, d), lambda i: (i, 0)),
        out_shape=jax.ShapeDtypeStruct((n, d), F32),
        compiler_params=_cparams(("arbitrary",)),
        name="combine",
    )(h, y2, y2, route)


def _shifted_taps(buf, u_ref, n_taps):
    buf[TAIL:TAIL + CHUNK, :] = u_ref[...]
    taps = [buf[TAIL - j:TAIL - j + CHUNK, :] for j in range(n_taps)]
    return taps


def _carry_tail(buf):
    buf[0:TAIL, :] = buf[CHUNK:CHUNK + TAIL, :]


def _ssd_stages(z_ref, x_ref, bc_ref, dt_ref, cwx_ref, cbx_ref, cwbc_ref, cbbc_ref, dtb_ref, alog_ref,
                dskip_ref, nw_ref, hexp_ref, o_ref, xbuf, bcbuf, state):
    st = {}
    gw = SSD_INNER // SSD_GROUPS

    def init():
        @pl.when(pl.program_id(1) == 0)
        def _():
            xbuf[0:TAIL, :] = jnp.zeros((TAIL, SSD_INNER), F32)
            bcbuf[0:TAIL, :] = jnp.zeros((TAIL, SSD_INNER), F32)
            state[...] = jnp.zeros_like(state)

    def conv(buf, u_ref, w_ref, b_ref):
        taps = _shifted_taps(buf, u_ref, SSD_CONV)
        acc = b_ref[...] + taps[0] * w_ref[3:4, :]
        for j in range(1, SSD_CONV):
            acc = acc + taps[j] * w_ref[3 - j:4 - j, :]
        _carry_tail(buf)
        return _silu(acc)

    def convs():
        st["xs"] = conv(xbuf, x_ref, cwx_ref, cbx_ref)
        st["bc"] = conv(bcbuf, bc_ref, cwbc_ref, cbbc_ref)

    def decays():
        lane = _iota2((CHUNK, LANES), 1)
        dt = _softplus(dt_ref[...] + dtb_ref[...])
        adt = jnp.where(lane < SSD_HEADS, -jnp.exp(alog_ref[...]) * dt, 0.0)
        cum = _dot_exact_lhs(_tri_incl_bf16(CHUNK), adt)
        hexp = hexp_ref[...]
        cum_full = _dot_exact_rhs(cum, hexp)
        tot_full = cum_full[CHUNK - 1:CHUNK, :]
        xd = st["xs"] * _dot_exact_rhs(dt, hexp)
        st.update(cum=cum, cum_t=cum.T, tot_full=tot_full, xd=xd, xds=xd * jnp.exp(tot_full - cum_full),
                  eac=jnp.exp(cum_full), lane_lo=lane < SSD_HEAD_DIM,
                  causal=_iota2((CHUNK, CHUNK), 0) >= _iota2((CHUNK, CHUNK), 1), y_parts=[])

    def group(g):
        bc, cum, cum_t, xd = st["bc"], st["cum"], st["cum_t"], st["xd"]
        bg = bc[:, g * SSD_STATE:(g + 1) * SSD_STATE]
        cg = bc[:, (SSD_GROUPS + g) * SSD_STATE:(SSD_GROUPS + g + 1) * SSD_STATE]
        cb = _dot_nt(cg, bg)
        s_prev = state[:, g * gw:(g + 1) * gw]
        y_off = _dot(cg, s_prev) * st["eac"][:, g * gw:(g + 1) * gw]
        s_new = _dot(bg.T, st["xds"][:, g * gw:(g + 1) * gw])
        state[:, g * gw:(g + 1) * gw] = jnp.exp(st["tot_full"][:, g * gw:(g + 1) * gw]) * s_prev + s_new
        for pr in range(2):
            lo = g * gw + pr * LANES
            xd_pair = xd[:, lo:lo + LANES]
            yd = jnp.zeros((CHUNK, LANES), F32)
            for k in range(2):
                h = (lo // SSD_HEAD_DIM) + k
                diff = cum[:, h:h + 1] - cum_t[h:h + 1, :]
                decay = jnp.exp(jnp.where(st["causal"], diff, NEG_BIG))
                keep = st["lane_lo"] if k == 0 else jnp.logical_not(st["lane_lo"])
                yd = yd + _dot(cb * decay, jnp.where(keep, xd_pair, 0.0))
            st["y_parts"].append(yd + y_off[:, pr * LANES:(pr + 1) * LANES])

    def finish():
        y = jnp.concatenate(st["y_parts"], axis=1) + dskip_ref[...] * st["xs"]
        y = y * _silu(z_ref[...])
        outs = []
        for g in range(SSD_GROUPS):
            yg = y[:, g * gw:(g + 1) * gw]
            outs.append(yg * lax.rsqrt(jnp.mean(yg * yg, axis=-1, keepdims=True) + 1e-5))
        o_ref[...] = (jnp.concatenate(outs, axis=1) * nw_ref[...]).astype(o_ref.dtype)

    return [init, convs, decays] + [functools.partial(group, g) for g in range(SSD_GROUPS)] + [finish]


def _head_expand(n_heads, head_dim):
    h = jnp.arange(LANES)[:, None]
    l = jnp.arange(n_heads * head_dim)[None, :]
    return (l // head_dim == h).astype(BF16)


def _ssd_operands(proj, nc, conv_w, conv_b, dt_bias, a_log, d_skip, norm_w):
    pad16 = lambda v: jnp.pad(v.astype(F32), (0, LANES - v.shape[0])).reshape(1, LANES)
    row_spec = lambda w, blk: pl.BlockSpec((CHUNK, w), lambda b, c: (b * nc + c, blk))
    full = lambda shape: pl.BlockSpec(shape, lambda b, c: (0, 0))
    in_specs = [row_spec(SSD_INNER, 0),
                row_spec(SSD_INNER, 1),
                row_spec(SSD_INNER, 2),
                row_spec(LANES, 50),
                full((SSD_CONV, SSD_INNER)), full((1, SSD_INNER)),
                full((SSD_CONV, SSD_INNER)), full((1, SSD_INNER)),
                full((1, LANES)), full((1, LANES)),
                full((1, SSD_INNER)), full((1, SSD_INNER)),
                full((LANES, SSD_INNER))]
    operands = [proj, proj, proj, proj,
                conv_w[:, :SSD_INNER], conv_b[:SSD_INNER].reshape(1, -1),
                conv_w[:, SSD_INNER:], conv_b[SSD_INNER:].reshape(1, -1),
                pad16(dt_bias), pad16(a_log),
                jnp.repeat(d_skip.astype(F32), SSD_HEAD_DIM).reshape(1, -1), norm_w.reshape(1, -1),
                _head_expand(SSD_HEADS, SSD_HEAD_DIM)]
    scratch = [pltpu.VMEM((TAIL + CHUNK, SSD_INNER), F32),
               pltpu.VMEM((TAIL + CHUNK, SSD_INNER), F32),
               pltpu.VMEM((SSD_STATE, SSD_INNER), F32)]
    return in_specs, operands, scratch


def _rwkv_kernel(r_ref, k_ref, v_ref, lo_ref, mur_ref, muk_ref, muv_ref, mulo_ref, w0_ref, w2_ref, a0_ref,
                 a2_ref, g2_ref, kk_ref, ka_ref, rk_ref, lnw_ref, lnb_ref, o_ref,
                 rbuf, kbuf, vbuf, lobuf, state, side_work=()):
    side = iter(side_work)
    run_side = lambda: next(side, lambda: None)()
    run_side()
    c = pl.program_id(1)

    @pl.when(c == 0)
    def _():
        rbuf[0:TAIL, :] = jnp.zeros((TAIL, RWKV_DIM), F32)
        kbuf[0:TAIL, :] = jnp.zeros((TAIL, RWKV_DIM), F32)
        vbuf[0:TAIL, :] = jnp.zeros((TAIL, RWKV_DIM), F32)
        lobuf[0:TAIL, :] = jnp.zeros((TAIL, 2 * LANES), F32)
        state[...] = jnp.zeros_like(state)

    def shift(buf, u_ref, mu_ref):
        cur, prev = _shifted_taps(buf, u_ref, 2)
        _carry_tail(buf)
        return cur + mu_ref[...] * (prev - cur)

    r = shift(rbuf, r_ref, mur_ref)
    k = shift(kbuf, k_ref, muk_ref)
    v = shift(vbuf, v_ref, muv_ref)
    lo = shift(lobuf, lo_ref, mulo_ref)
    wa = lo[:, :LANES]
    w = -_softplus(-(w0_ref[...] + _dot(jnp.tanh(wa), w2_ref[...]))) - 0.5
    logw = -jnp.exp(w)
    a = _sigmoid(a0_ref[...] + _dot(wa, a2_ref[...]))
    g = _dot(_sigmoid(lo[:, LANES:]), g2_ref[...])

    lane = _iota2((CHUNK, LANES), 1)
    head0 = lane < RWKV_HEAD_DIM
    bd = (_iota2((LANES, LANES), 0) // RWKV_HEAD_DIM) == (_iota2((LANES, LANES), 1) // RWKV_HEAD_DIM)
    bd_ones = jnp.where(bd, 1.0, 0.0).astype(BF16)

    def head_sum(x):
        return jnp.dot(x.astype(BF16), bd_ones, preferred_element_type=F32)

    tri = _tri_incl_bf16(CHUNK)
    cum = _dot_exact_lhs(tri, logw)
    cume = cum - logw
    cmid = cum[CHUNK // 2 - 1:CHUNK // 2, :]
    cend = cum[CHUNK - 1:CHUNK, :]
    e_in_mid = jnp.exp(cum - cmid)
    e_ex_mid = jnp.exp(cume - cmid)
    e_mid_in = jnp.exp(cmid - cum)
    e_ex = jnp.exp(cume)
    e_in = jnp.exp(cum)
    e_end = jnp.exp(cend - cum)
    e_tot = jnp.exp(cend)

    row = _iota2((CHUNK, CHUNK), 0)
    col = _iota2((CHUNK, CHUNK), 1)
    strict = row > col
    incl = row >= col
    zeros = jnp.zeros((CHUNK, LANES), F32)

    n_blocks = RWKV_DIM // LANES
    blk = []
    for p in range(n_blocks):
        sl = slice(p * LANES, (p + 1) * LANES)
        rp, vp, ap = r[:, sl], v[:, sl], a[:, sl]
        kkp = k[:, sl] * kk_ref[:, sl]
        kkp = kkp / jnp.maximum(jnp.sqrt(head_sum(kkp * kkp)), 1e-12)
        kp = k[:, sl] * (1.0 + (ap - 1.0) * ka_ref[:, sl])
        aap = -kkp
        bp = kkp * ap
        a_mid = aap * e_ex_mid[:, sl]
        r_mid = rp * e_in_mid[:, sl]
        bf = lambda x: x.astype(BF16)
        lhs = jnp.concatenate([bf(jnp.where(head0, a_mid, 0.0)), bf(jnp.where(head0, 0.0, a_mid)),
                               bf(jnp.where(head0, r_mid, 0.0)), bf(jnp.where(head0, 0.0, r_mid))], axis=0)
        rhs = jnp.concatenate([bf(bp * e_mid_in[:, sl]), bf(kp * e_mid_in[:, sl])], axis=0)
        a_abs = aap * e_ex[:, sl]
        blk.append(dict(sl=sl, rp=rp, vp=vp, kp=kp, r_abs=rp * e_in[:, sl],
                        a_abs=(bf(jnp.where(head0, a_abs, 0.0)), bf(jnp.where(head0, 0.0, a_abs))),
                        vm=(bf(jnp.where(head0, vp, 0.0)), bf(jnp.where(head0, 0.0, vp))),
                        b_end=bf(bp * e_end[:, sl]), k_end=bf(kp * e_end[:, sl]), prod=_dot_nt(lhs, rhs)))
    run_side()

    heads = []
    for p in range(n_blocks):
        prod = blk[p]["prod"]
        for hh in range(2):
            heads.append(dict(
                vm=blk[p]["vm"][hh], a_abs=blk[p]["a_abs"][hh],
                a_ab=jnp.where(strict, prod[hh * CHUNK:(hh + 1) * CHUNK, :CHUNK], 0.0),
                a_ak=jnp.where(strict, prod[hh * CHUNK:(hh + 1) * CHUNK, CHUNK:], 0.0).astype(BF16),
                m_rb=jnp.where(incl, prod[(2 + hh) * CHUNK:(3 + hh) * CHUNK, :CHUNK], 0.0).astype(BF16),
                m_rk=jnp.where(incl, prod[(2 + hh) * CHUNK:(3 + hh) * CHUNK, CHUNK:], 0.0).astype(BF16)))

    eye = jnp.where(row == col, 1.0, 0.0)
    n_levels = int(math.log2(CHUNK)) - 1
    ts = [eye + hd["a_ab"] for hd in heads]
    xs = [hd["a_ab"].astype(BF16) for hd in heads]
    avs = [_dot(hd["a_ak"], hd["vm"]).astype(BF16) for hd in heads]
    run_side()
    xs = [jnp.dot(x, x, preferred_element_type=F32).astype(BF16) for x in xs]
    for _ in range(n_levels - 1):
        run_side()
        zs = [jnp.dot(x, jnp.concatenate([x, t.astype(BF16)], axis=1), preferred_element_type=F32)
              for x, t in zip(xs, ts)]
        xs = [z[:, :CHUNK].astype(BF16) for z in zs]
        ts = [t + z[:, CHUNK:] for t, z in zip(ts, zs)]
    ts = [t + jnp.dot(x, t.astype(BF16), preferred_element_type=F32) for x, t in zip(xs, ts)]
    for _ in side:
        _()

    wmats = [_dot(t, jnp.concatenate([hd["a_abs"], av], axis=1)).astype(BF16)
             for t, hd, av in zip(ts, heads, avs)]
    zeros_bf = zeros.astype(BF16)
    outs = [_dot(jnp.concatenate([hd["m_rb"], hd["m_rk"]], axis=1),
                 jnp.concatenate([jnp.concatenate([wm[:, LANES:], wm[:, :LANES]], axis=1),
                                  jnp.concatenate([hd["vm"], zeros_bf], axis=1)], axis=0))
            for wm, hd in zip(wmats, heads)]

    zts = []
    for p in range(n_blocks):
        w0h, w1h = wmats[2 * p], wmats[2 * p + 1]
        ui = w0h[:, LANES:] + w1h[:, LANES:]
        a_eff = w0h[:, :LANES] + w1h[:, :LANES]
        zts.append(_dot_tn(jnp.concatenate([jnp.concatenate([ui, a_eff], axis=1),
                                            jnp.concatenate([blk[p]["vp"].astype(BF16), zeros_bf], axis=1)],
                                           axis=0),
                           jnp.concatenate([blk[p]["b_end"], blk[p]["k_end"]], axis=0)))

    ys = []
    for p in range(n_blocks):
        s0 = state[p]
        o0, o1 = outs[2 * p], outs[2 * p + 1]
        r_eff = blk[p]["r_abs"] + o0[:, LANES:] + o1[:, LANES:]
        ys.append(_dot_nt(r_eff, s0) + o0[:, :LANES] + o1[:, :LANES])
        h_intra = jnp.where(bd, zts[p][:LANES, :], 0.0)
        g_corr = jnp.where(bd, zts[p][LANES:, :], 0.0)
        state[p] = s0 * e_tot[:, blk[p]["sl"]] + _dot(s0, g_corr) + h_intra

    means = [head_sum(y) * (1.0 / RWKV_HEAD_DIM) for y in ys]
    devs = [y - m for y, m in zip(ys, means)]
    vars_ = [head_sum(dv * dv) * (1.0 / RWKV_HEAD_DIM) for dv in devs]
    for p in range(n_blocks):
        sl = blk[p]["sl"]
        yn = devs[p] * lax.rsqrt(vars_[p] + RWKV_GN_EPS) * lnw_ref[:, sl] + lnb_ref[:, sl]
        bonus = head_sum(blk[p]["rp"] * blk[p]["kp"] * rk_ref[:, sl])
        o_ref[:, sl] = ((yn + bonus * blk[p]["vp"]) * g[:, sl]).astype(o_ref.dtype)


def _rwkv_operands(proj, nc, mu, w0, w2, a0, a2, g2, k_k, k_a, r_k, ln_w, ln_b):
    d = RWKV_DIM
    row_spec = lambda w, blk: pl.BlockSpec((CHUNK, w), lambda b, c: (b * nc + c, blk))
    full = lambda shape: pl.BlockSpec(shape, lambda b, c: (0,) * len(shape))
    vec = lambda x: x.astype(F32).reshape(1, -1)
    w2p = jnp.concatenate([w2, jnp.zeros((AAA_LORA, d), w2.dtype)], axis=0).astype(BF16)
    a2p = jnp.concatenate([jnp.zeros((DECAY_LORA, d), a2.dtype), a2], axis=0).astype(BF16)
    in_specs = [row_spec(d, 3), row_spec(d, 4), row_spec(d, 5), row_spec(2 * LANES, 24),
                full((1, d)), full((1, d)), full((1, d)), full((1, 2 * LANES)),
                full((1, d)), full((LANES, d)), full((1, d)), full((LANES, d)), full((LANES, d)),
                full((1, d)), full((1, d)), full((1, d)), full((1, d)), full((1, d))]
    operands = [proj, proj, proj, proj,
                vec(mu[:d]), vec(mu[d:2 * d]), vec(mu[2 * d:3 * d]), vec(mu[3 * d:]),
                vec(w0), w2p, vec(a0), a2p, g2.astype(BF16),
                vec(k_k), vec(k_a), vec(r_k), vec(ln_w), vec(ln_b)]
    scratch = [pltpu.VMEM((TAIL + CHUNK, d), F32), pltpu.VMEM((TAIL + CHUNK, d), F32),
               pltpu.VMEM((TAIL + CHUNK, d), F32), pltpu.VMEM((TAIL + CHUNK, 2 * LANES), F32),
               pltpu.VMEM((d // LANES, LANES, LANES), F32)]
    return in_specs, operands, scratch


def _mixers_kernel(*refs, n_ssd_in, n_rwkv_in, n_ssd_scratch):
    ssd_in = refs[:n_ssd_in]
    rwkv_in = refs[n_ssd_in:n_ssd_in + n_rwkv_in]
    o_ssd, o_rwkv = refs[n_ssd_in + n_rwkv_in:n_ssd_in + n_rwkv_in + 2]
    scratch = refs[n_ssd_in + n_rwkv_in + 2:]
    _rwkv_kernel(*rwkv_in, o_rwkv, *scratch[n_ssd_scratch:],
                 side_work=_ssd_stages(*ssd_in, o_ssd, *scratch[:n_ssd_scratch]))


def _mixers(proj, batch, seq, ssd_params, rwkv_params):
    nc = seq // CHUNK
    n = batch * seq
    s_specs, s_ops, s_scratch = _ssd_operands(proj, nc, *ssd_params)
    r_specs, r_ops, r_scratch = _rwkv_operands(proj, nc, *rwkv_params)
    out_spec = lambda w: pl.BlockSpec((CHUNK, w), lambda b, c: (b * nc + c, 0))
    return pl.pallas_call(
        functools.partial(_mixers_kernel, n_ssd_in=len(s_specs), n_rwkv_in=len(r_specs),
                          n_ssd_scratch=len(s_scratch)),
        grid=(batch, nc),
        in_specs=s_specs + r_specs,
        out_specs=[out_spec(SSD_INNER), out_spec(RWKV_DIM)],
        out_shape=[jax.ShapeDtypeStruct((n, SSD_INNER), BF16), jax.ShapeDtypeStruct((n, RWKV_DIM), BF16)],
        scratch_shapes=s_scratch + r_scratch,
        compiler_params=_cparams(("arbitrary", "arbitrary")),
        name="mixers",
    )(*s_ops, *r_ops)


def _route(logits):
    lane = _iota2(logits.shape, 1)
    lanef = lane.astype(F32)
    big = float(LANES)

    def first_max(x):
        m = jnp.max(x, axis=-1, keepdims=True)
        idx = jnp.min(jnp.where(x == m, lanef, big), axis=-1, keepdims=True)
        return m, idx

    cl = jnp.where(lane < EXPERT_GROUPS, logits, NEG_BIG)
    cmax, grp = first_max(cl)
    p_group = 1.0 / jnp.sum(jnp.exp(cl - cmax), axis=-1, keepdims=True)
    lo = EXPERT_GROUPS + grp * EXPERTS_PER_GROUP
    fl = jnp.where((lanef >= lo) & (lanef < lo + EXPERTS_PER_GROUP), logits, NEG_BIG)
    m0, i0 = first_max(fl)
    m1, i1 = first_max(jnp.where(lanef == i0, NEG_BIG, fl))
    e1 = jnp.exp(m1 - m0)
    g0 = p_group / (1.0 + e1)
    g1 = p_group * e1 / (1.0 + e1)
    return jnp.where(lane == 0, i0 - EXPERT_GROUPS,
                     jnp.where(lane == 1, i1 - EXPERT_GROUPS,
                               jnp.where(lane == 2, g0, jnp.where(lane == 3, g1, 0.0))))


def _outproj_router_kernel(*refs, n_in):
    ys = refs[:n_in]
    ws = refs[n_in:2 * n_in]
    h_ref, g_ref, wr_ref, br_ref, hnew_ref, hf_ref, route_ref = refs[2 * n_in:]
    acc = h_ref[...]
    for y_ref, w_ref in zip(ys, ws):
        acc = acc + jnp.dot(y_ref[...], w_ref[...], preferred_element_type=F32)
    hnew_ref[...] = acc
    hf = _rms(acc, g_ref[...])
    hf_ref[...] = hf
    route_ref[...] = _route(_dot_f32(hf, wr_ref[...]) + br_ref[...])


def _outproj_router(ys, ws, h, gain, w_router, b_router, tm):
    n, d = h.shape
    n_in = len(ys)
    in_specs = ([pl.BlockSpec((tm, y.shape[1]), lambda i: (i, 0)) for y in ys]
                + [pl.BlockSpec(w.shape, lambda i: (0, 0)) for w in ws]
                + [pl.BlockSpec((tm, d), lambda i: (i, 0)),
                   pl.BlockSpec((1, d), lambda i: (0, 0)),
                   pl.BlockSpec((d, LANES), lambda i: (0, 0)),
                   pl.BlockSpec((1, LANES), lambda i: (0, 0))])
    return pl.pallas_call(
        functools.partial(_outproj_router_kernel, n_in=n_in),
        grid=(n // tm,),
        in_specs=in_specs,
        out_specs=[pl.BlockSpec((tm, d), lambda i: (i, 0)),
                   pl.BlockSpec((tm, d), lambda i: (i, 0)),
                   pl.BlockSpec((tm, LANES), lambda i: (i, 0))],
        out_shape=[jax.ShapeDtypeStruct((n, d), F32), jax.ShapeDtypeStruct((n, d), F32),
                   jax.ShapeDtypeStruct((n, LANES), F32)],
        compiler_params=_cparams(("arbitrary",)),
        name="outproj_router",
    )(*ys, *ws, h, gain.reshape(1, d), w_router, b_router)


def _router_weights(w_coarse, b_coarse, w_fine, b_fine):
    d = w_coarse.shape[0]
    wf = jnp.transpose(w_fine, (1, 0, 2)).reshape(d, N_EXPERTS)
    w = jnp.concatenate([w_coarse, wf], axis=1).astype(F32)
    b = jnp.concatenate([b_coarse, b_fine.reshape(N_EXPERTS)]).astype(F32)
    pad = LANES - w.shape[1]
    return jnp.pad(w, ((0, 0), (0, pad))), jnp.pad(b, (0, pad)).reshape(1, LANES)


def _invert_kernel(dest_ref, inv_ref):
    step = pl.program_id(0)
    n_clear = inv_ref.shape[0] // INVERT_CHUNK

    @pl.when(step < n_clear)
    def _():
        def clear(i, carry):
            for j in range(DMA_UNROLL):
                inv_ref[step * INVERT_CHUNK + i * DMA_UNROLL + j] = -1
            return carry
        lax.fori_loop(0, INVERT_CHUNK // DMA_UNROLL, clear, 0)

    @pl.when(step >= n_clear)
    def _():
        base = (step - n_clear) * INVERT_CHUNK

        def fill(i, carry):
            for j in range(DMA_UNROLL):
                k = i * DMA_UNROLL + j
                inv_ref[dest_ref[k]] = base + k
            return carry
        lax.fori_loop(0, INVERT_CHUNK // DMA_UNROLL, fill, 0)


def _moe_plan(route):
    n = route.shape[0]
    a = n * TOP_K
    n_blocks = a // MOE_BLOCK + N_EXPERTS
    e_flat = route[:, :TOP_K].astype(jnp.int32).reshape(a)
    seg = MOE_BLOCK
    onehot = (e_flat[:, None] == jnp.arange(N_EXPERTS, dtype=jnp.int32)[None, :]).astype(F32)
    onehot = onehot.reshape(a // seg, seg, N_EXPERTS)
    tri = jnp.tril(jnp.ones((seg, seg), F32))
    within = jnp.einsum("ij,bjk->bik", tri, onehot)
    tot = within[:, -1, :]
    offs = jnp.cumsum(tot, axis=0) - tot
    rank = (jnp.sum(onehot * (within + offs[:, None, :]), axis=-1) - 1.0).astype(jnp.int32).reshape(a)
    counts = (offs[-1] + tot[-1]).astype(jnp.int32)
    padded = (counts + MOE_BLOCK - 1) // MOE_BLOCK * MOE_BLOCK
    pad_end = jnp.cumsum(padded)
    pad_start = pad_end - padded
    start_of = jnp.sum(jnp.where(e_flat[:, None] == jnp.arange(N_EXPERTS, dtype=jnp.int32)[None, :],
                                 pad_start[None, :], 0), axis=1)
    dest = (start_of + rank).astype(jnp.int32)
    block_start = jnp.arange(n_blocks, dtype=jnp.int32) * MOE_BLOCK
    block_expert = jnp.minimum(jnp.sum((pad_end[None, :] <= block_start[:, None]).astype(jnp.int32), axis=1),
                               N_EXPERTS - 1).astype(jnp.int32)
    n_used = (pad_end[-1] // MOE_BLOCK).astype(jnp.int32).reshape(1)

    n_clear = n_blocks * MOE_BLOCK // INVERT_CHUNK
    inv = pl.pallas_call(
        _invert_kernel,
        grid=(n_clear + a // INVERT_CHUNK,),
        in_specs=[pl.BlockSpec((INVERT_CHUNK,), lambda i: (jnp.maximum(i - n_clear, 0),),
                               memory_space=pltpu.SMEM)],
        out_specs=pl.BlockSpec(memory_space=pltpu.SMEM),
        out_shape=jax.ShapeDtypeStruct((n_blocks * MOE_BLOCK,), jnp.int32),
        compiler_params=_cparams(("arbitrary",)),
        name="moe_invert",
    )(dest)
    inv = inv.reshape(n_blocks, MOE_BLOCK)
    slot = jnp.arange(n_blocks * MOE_BLOCK, dtype=jnp.int32).reshape(n_blocks, MOE_BLOCK)
    seg_end = jnp.cumsum(counts)
    seg_end_of_block = jnp.sum(jnp.where(block_expert[:, None] == jnp.arange(N_EXPERTS, dtype=jnp.int32)[None, :],
                                         seg_end[None, :], 0), axis=1)
    holds = inv >= 0
    token = inv // TOP_K
    src = jnp.where(holds, token, 0)
    dst = jnp.where(holds, (inv % TOP_K) * n + token, a + slot - seg_end_of_block[:, None])
    shape3 = (n_blocks, 1, MOE_BLOCK)
    return dict(src3=src.reshape(shape3), dst3=dst.reshape(shape3), block_expert=block_expert, n_used=n_used,
                n_blocks=n_blocks)


def _experts_kernel(bexp_ref, nused_ref, src_ref, srcn_ref, dstp_ref, dst_ref, hf_hbm, wg_ref, wu_ref, wd_ref,
                    y2_hbm, xg, yb, wg_s, wu_s, wd_s, gsem, ssem, zsem):
    i = pl.program_id(0)
    n_blocks = pl.num_programs(0)
    n_used = nused_ref[0]
    slot = lax.rem(i, 2)

    def gather_row(idx_ref, buf, r):
        return pltpu.make_async_copy(hf_hbm.at[pl.ds(idx_ref[0, 0, r], 1), :], xg.at[buf, pl.ds(r, 1), :],
                                     gsem.at[buf])

    def scatter_row(idx_ref, buf, r):
        return pltpu.make_async_copy(yb.at[buf, pl.ds(r, 1), :], y2_hbm.at[pl.ds(idx_ref[0, 0, r], 1), :],
                                     ssem.at[buf])

    def wait_gather(buf):
        pltpu.make_async_copy(hf_hbm.at[pl.ds(0, MOE_BLOCK), :], xg.at[buf], gsem.at[buf]).wait()

    def wait_scatter(buf):
        pltpu.make_async_copy(yb.at[buf], y2_hbm.at[pl.ds(0, MOE_BLOCK), :], ssem.at[buf]).wait()

    def spare_block(b, sem):
        start = pl.multiple_of(b * MOE_BLOCK, MOE_BLOCK)
        return pltpu.make_async_copy(yb.at[1], y2_hbm.at[pl.ds(start, MOE_BLOCK), :], sem)

    @pl.when(i == 0)
    def _():
        yb[...] = jnp.zeros_like(yb)

        def zero_block(b, carry):
            spare_block(b, zsem).start()
            return carry
        lax.fori_loop(n_used, n_blocks, zero_block, 0)

        def wait_block(b, carry):
            spare_block(b, zsem).wait()
            return carry
        lax.fori_loop(n_used, n_blocks, wait_block, 0)
        spare_block(n_blocks, ssem.at[0]).start()

        def first(g, carry):
            for u in range(DMA_UNROLL):
                gather_row(src_ref, 0, g * DMA_UNROLL + u).start()
            return carry
        lax.fori_loop(0, MOE_BLOCK // DMA_UNROLL, first, 0)

    active = i < n_used

    @pl.when(active)
    def _():
        wait_gather(slot)
        wait_scatter(slot)

    @pl.when(active & ((i == 0) | (bexp_ref[i] != bexp_ref[jnp.maximum(i - 1, 0)])))
    def _():
        wg_s[...] = wg_ref[0].astype(BF16)
        wu_s[...] = wu_ref[0].astype(BF16)
        wd_s[...] = wd_ref[0].astype(BF16)

    @pl.when(active)
    def _():
        for r in range(MOE_BLOCK):
            gather_row(srcn_ref, 1 - slot, r).start()
        for r in range(MOE_BLOCK):
            scatter_row(dstp_ref, 1 - slot, r).start()
        x = xg[slot].astype(BF16)
        hg = jnp.dot(x, wg_s[...], preferred_element_type=F32)
        hu = jnp.dot(x, wu_s[...], preferred_element_type=F32)
        hb = (_silu(hg) * hu).astype(BF16)
        yb[slot] = jnp.dot(hb, wd_s[...], preferred_element_type=F32)

    @pl.when(i == n_used - 1)
    def _():
        def last(g, carry):
            for u in range(DMA_UNROLL):
                scatter_row(dst_ref, slot, g * DMA_UNROLL + u).start()
            return carry
        lax.fori_loop(0, MOE_BLOCK // DMA_UNROLL, last, 0)
        wait_scatter(1 - slot)
        wait_scatter(slot)
        wait_gather(1 - slot)


def _experts(hf, plan, layer, wg, wu, wd):
    n, d = hf.shape
    n_blocks = plan["n_blocks"]
    used = lambda i, nu: jnp.minimum(i, nu[0] - 1)
    w_idx = lambda i, be, nu: (layer, be[used(i, nu)], 0, 0)
    smem = lambda fn: pl.BlockSpec((1, 1, MOE_BLOCK), fn, memory_space=pltpu.SMEM)
    grid_spec = pltpu.PrefetchScalarGridSpec(
        num_scalar_prefetch=2,
        grid=(n_blocks,),
        in_specs=[smem(lambda i, be, nu: (i, 0, 0)),
                  smem(lambda i, be, nu: (jnp.minimum(i + 1, nu[0] - 1), 0, 0)),
                  smem(lambda i, be, nu: (jnp.maximum(i - 1, 0), 0, 0)),
                  smem(lambda i, be, nu: (i, 0, 0)),
                  pl.BlockSpec(memory_space=pl.ANY),
                  pl.BlockSpec((None, 1, d, EXPERT_HIDDEN), w_idx),
                  pl.BlockSpec((None, 1, d, EXPERT_HIDDEN), w_idx),
                  pl.BlockSpec((None, 1, EXPERT_HIDDEN, d), w_idx)],
        out_specs=pl.BlockSpec(memory_space=pl.ANY),
        scratch_shapes=[pltpu.VMEM((2, MOE_BLOCK, d), F32), pltpu.VMEM((2, MOE_BLOCK, d), F32),
                        pltpu.VMEM((d, EXPERT_HIDDEN), BF16), pltpu.VMEM((d, EXPERT_HIDDEN), BF16),
                        pltpu.VMEM((EXPERT_HIDDEN, d), BF16),
                        pltpu.SemaphoreType.DMA((2,)), pltpu.SemaphoreType.DMA((2,)),
                        pltpu.SemaphoreType.DMA(())])
    return pl.pallas_call(
        _experts_kernel,
        grid_spec=grid_spec,
        out_shape=jax.ShapeDtypeStruct(((n_blocks + 1) * MOE_BLOCK, d), F32),
        compiler_params=_cparams(("arbitrary",)),
        name="moe_experts",
    )(plan["block_expert"], plan["n_used"], plan["src3"], plan["src3"], plan["dst3"], plan["dst3"], hf, wg, wu, wd)


def _swa_kernel(q_ref, kvp_ref, kvc_ref, qg_ref, kg_ref, slope_ref, sink_ref, o_ref):
    jb = pl.program_id(1)
    blk = CHUNK
    qi = _iota2((blk, blk), 0)
    kj = _iota2((blk, blk), 1)
    from_prev = kj > qi
    deltaf = jnp.where(from_prev, qi + blk - kj, qi - kj).astype(F32)
    no_prev = jnp.where(from_prev, jnp.where(jb > 0, 0.0, NEG_BIG), 0.0)
    scale = HEAD_DIM ** -0.5

    bd = (_iota2((LANES, LANES), 0) // HEAD_DIM) == (_iota2((LANES, LANES), 1) // HEAD_DIM)
    bd_ones = jnp.where(bd, 1.0, 0.0).astype(BF16)

    def head_rms(x, gain):
        sq = x * x
        s1 = sq.astype(BF16)
        s2 = (sq - s1.astype(F32)).astype(BF16)
        ms = (jnp.dot(s1, bd_ones, preferred_element_type=F32)
              + jnp.dot(s2, bd_ones, preferred_element_type=F32)) * (1.0 / HEAD_DIM)
        return x * lax.rsqrt(ms + NORM_EPS) * gain

    n_kv_blk = KV_DIM // LANES
    kv = jnp.concatenate([kvp_ref[...], kvc_ref[...]], axis=0)
    lane_kv = _iota2((kv.shape[0], LANES), 1)
    kv_half = (lane_kv < HEAD_DIM, lane_kv >= HEAD_DIM)
    lane_q = _iota2((blk, LANES), 1)
    q_half = (lane_q < HEAD_DIM, lane_q >= HEAD_DIM)
    kn = [head_rms(kv[:, j * LANES:(j + 1) * LANES], kg_ref[...]) for j in range(n_kv_blk)]
    vb = [kv[:, KV_DIM + j * LANES:KV_DIM + (j + 1) * LANES] for j in range(n_kv_blk)]
    kn_sw = [pltpu.roll(x, HEAD_DIM, 1) for x in kn]
    vb_sw = [pltpu.roll(x, HEAD_DIM, 1) for x in vb]
    v_same = [jnp.where(kv_half[g % 2], vb[g // 2], 0.0) for g in range(KV_HEADS)]
    v_swap = [jnp.where(kv_half[1 - g % 2], vb_sw[g // 2], 0.0) for g in range(KV_HEADS)]
    qn = [head_rms(q_ref[:, j * LANES:(j + 1) * LANES], qg_ref[...]) for j in range(Q_DIM // LANES)]

    rep = lambda x: jnp.concatenate([x] * Q_PER_KV, axis=0)
    from_prev4, delta4 = rep(from_prev), rep(deltaf)
    neg4 = [rep(no_prev)] + [None] * (SWA_QBLOCKS - 1)
    col = lambda ref, hs: jnp.concatenate([jnp.broadcast_to(ref[:, h:h + 1], (blk, 1)) for h in hs], axis=0)
    order = lambda g: [g * Q_PER_KV + g % 2, g * Q_PER_KV + g % 2 + 2,
                       g * Q_PER_KV + 1 - g % 2, g * Q_PER_KV + 3 - g % 2]
    units = [(u, g) for u in range(SWA_QBLOCKS) for g in range(KV_HEADS)]
    keys = lambda x, u: x[u * blk:(u + 2) * blk]
    qrow = lambda x, u: x[u * blk:(u + 1) * blk]

    qms = [[jnp.where(q_half[h % 2], qrow(qn[h // 2], u), 0.0) for h in order(g)] for u, g in units]
    scs = [jnp.concatenate([_dot_nt(jnp.concatenate(qm[:2], axis=0), keys(kn[g // 2], u)),
                            _dot_nt(jnp.concatenate(qm[2:], axis=0), keys(kn_sw[g // 2], u))], axis=0)
           for qm, (u, g) in zip(qms, units)]
    sinks = [col(sink_ref, order(g)) for u, g in units]
    ss = []
    for sc, (u, g) in zip(scs, units):
        s = jnp.where(from_prev4, sc[:, :blk], sc[:, blk:]) * scale - col(slope_ref, order(g)) * delta4
        ss.append(s if neg4[u] is None else s + neg4[u])
    ms = [jnp.maximum(jnp.max(s, axis=-1, keepdims=True), sk) for s, sk in zip(ss, sinks)]
    ps = [jnp.exp(s - m) for s, m in zip(ss, ms)]
    invs = [1.0 / (jnp.sum(p, axis=-1, keepdims=True) + jnp.exp(sk - m)) for p, sk, m in zip(ps, sinks, ms)]
    pcats = [jnp.concatenate([jnp.where(from_prev4, p, 0.0), jnp.where(from_prev4, 0.0, p)], axis=1) for p in ps]
    outs = [jnp.concatenate([_dot(pc[:2 * blk], keys(v_same[g], u)), _dot(pc[2 * blk:], keys(v_swap[g], u))],
                            axis=0) * inv
            for pc, inv, (u, g) in zip(pcats, invs, units)]
    for o, (u, g) in zip(outs, units):
        hs = order(g)
        for j in sorted({h // 2 for h in hs}):
            pair = sum(o[idx * blk:(idx + 1) * blk] for idx, h in enumerate(hs) if h // 2 == j)
            o_ref[u * blk:(u + 1) * blk, j * LANES:(j + 1) * LANES] = pair.astype(o_ref.dtype)


def _swa(proj, batch, seq, q_gain, k_gain, sinks):
    nbk = seq // CHUNK
    qrows = SWA_QBLOCKS * CHUNK
    nsteps = seq // qrows
    n = batch * seq
    slopes = (2.0 ** (-8.0 * jnp.arange(1, ATT_HEADS + 1, dtype=F32) / ATT_HEADS))
    pad = lambda v: jnp.pad(v.astype(F32), (0, LANES - v.shape[0])).reshape(1, LANES)
    pair = lambda v: jnp.tile(v.astype(F32), LANES // HEAD_DIM).reshape(1, LANES)
    full = lambda shape: pl.BlockSpec(shape, lambda b, j: (0, 0))
    kvw = 2 * KV_DIM
    return pl.pallas_call(
        _swa_kernel,
        grid=(batch, nsteps),
        in_specs=[pl.BlockSpec((qrows, Q_DIM), lambda b, j: (b * nsteps + j, 0)),
                  pl.BlockSpec((CHUNK, kvw),
                               lambda b, j: (b * nbk + jnp.maximum(SWA_QBLOCKS * j - 1, 0), Q_DIM // kvw)),
                  pl.BlockSpec((qrows, kvw), lambda b, j: (b * nsteps + j, Q_DIM // kvw)),
                  full((1, LANES)), full((1, LANES)), full((1, LANES)), full((1, LANES))],
        out_specs=pl.BlockSpec((qrows, Q_DIM), lambda b, j: (b * nsteps + j, 0)),
        out_shape=jax.ShapeDtypeStruct((n, Q_DIM), BF16),
        compiler_params=_cparams(("arbitrary", "arbitrary")),
        name="swa",
    )(proj, proj, proj, pair(q_gain), pair(k_gain), pad(slopes), pad(sinks))


def _even_in_weight(w):
    rw = SSD_COLS
    cols = jnp.concatenate([w[:, :SSD_INNER + SSD_CONV_DIM], w[:, rw:],
                            w[:, SSD_INNER + SSD_CONV_DIM:SSD_COLS]], axis=1)
    return jnp.pad(cols, ((0, 0), (0, EVEN_COLS_PAD - cols.shape[1]))).astype(BF16)


def kernel(x, ln_mix, ln_ffn, e_w_in, e_w_out, ssd_conv_w, ssd_conv_b, ssd_dt_bias, ssd_a_log, ssd_d, ssd_norm,
           rwkv_mu, rwkv_w0, rwkv_w2, rwkv_a0, rwkv_a2, rwkv_g2, rwkv_k_k, rwkv_k_a, rwkv_r_k, rwkv_ln_w,
           rwkv_ln_b, o_w_in, o_w_out, attn_q_norm, attn_k_norm, attn_sinks, moe_w_coarse, moe_b_coarse,
           moe_w_fine, moe_b_fine, moe_w_gate, moe_w_up, moe_w_down):
    batch, seq, d = x.shape
    n = batch * seq
    tm = min(512, n)
    h = x.reshape(n, d)

    def moe(layer, hf, route):
        return _experts(hf, _moe_plan(route), layer, moe_w_gate, moe_w_up, moe_w_down)

    def router_w(layer):
        return _router_weights(moe_w_coarse[layer], moe_b_coarse[layer], moe_w_fine[layer], moe_b_fine[layer])

    proj = _norm_proj(h, ln_mix[0], _even_in_weight(e_w_in[0]), tm, EVEN_COLS_PAD // 3)
    y_ssd, y_rwkv = _mixers(
        proj, batch, seq,
        (ssd_conv_w[0], ssd_conv_b[0], ssd_dt_bias[0], ssd_a_log[0], ssd_d[0], ssd_norm[0]),
        (rwkv_mu[0], rwkv_w0[0], rwkv_w2[0], rwkv_a0[0], rwkv_a2[0], rwkv_g2[0], rwkv_k_k[0], rwkv_k_a[0],
         rwkv_r_k[0].reshape(-1), rwkv_ln_w[0], rwkv_ln_b[0]))
    w_out = e_w_out[0].astype(BF16)
    wr, br = router_w(0)
    h, hf, route = _outproj_router([y_ssd, y_rwkv], [w_out[:SSD_INNER], w_out[SSD_INNER:]], h, ln_ffn[0],
                                   wr, br, tm)
    y2 = moe(0, hf, route)

    h, proj = _combine_norm_proj(h, y2, route, ln_mix[1], o_w_in[0].astype(BF16), tm)
    att = _swa(proj, batch, seq, attn_q_norm[0], attn_k_norm[0], attn_sinks[0])
    wr, br = router_w(1)
    h, hf, route = _outproj_router([att], [o_w_out[0].astype(BF16)], h, ln_ffn[1], wr, br, tm)
    y2 = moe(1, hf, route)
    out = _combine(h, y2, route, tm)
    return out.reshape(batch, seq, d)
```

```python
import functools
import math

import jax
import jax.numpy as jnp
from jax import lax
from jax.experimental import pallas as pl
from jax.experimental.pallas import tpu as pltpu

F32 = jnp.float32
BF16 = jnp.bfloat16

D_MODEL = 1024
SSD_HEADS = 16
SSD_HEAD_DIM = 64
SSD_INNER = 1024
SSD_GROUPS = 4
SSD_STATE = 128
SSD_CONV = 4
SSD_CONV_DIM = 2048
SSD_COLS = 3088
RWKV_HEADS = 16
RWKV_HEAD_DIM = 64
RWKV_DIM = 1024
DECAY_LORA = 64
AAA_LORA = 64
GATE_LORA = 128
RWKV_GN_EPS = 64e-5
ATT_HEADS = 16
KV_HEADS = 4
Q_PER_KV = 4
HEAD_DIM = 64
Q_DIM = 1024
KV_DIM = 256
WINDOW = 128
EXPERT_GROUPS = 4
EXPERTS_PER_GROUP = 8
N_EXPERTS = 32
TOP_K = 2
EXPERT_HIDDEN = 512
MOE_BLOCK = 256
NORM_EPS = 1e-6

LANES = 128
CHUNK = 128
TAIL = 8
DMA_UNROLL = 8
SWA_QBLOCKS = 4
EVEN_COLS_PAD = 6528
VMEM_LIMIT = 56 * 1024 * 1024
NEG_BIG = -1e30
assert WINDOW == CHUNK


def _cparams(sem):
    return pltpu.CompilerParams(dimension_semantics=sem, vmem_limit_bytes=VMEM_LIMIT)


def _dot(a, b):
    return jnp.dot(a.astype(BF16), b.astype(BF16), preferred_element_type=F32)


def _dot_nt(a, b):
    return lax.dot_general(a.astype(BF16), b.astype(BF16), (((1,), (1,)), ((), ())),
                           preferred_element_type=F32)


def _dot_tn(a, b):
    return lax.dot_general(a.astype(BF16), b.astype(BF16), (((0,), (0,)), ((), ())),
                           preferred_element_type=F32)


def _split3(x):
    x1 = x.astype(BF16)
    r1 = x - x1.astype(F32)
    x2 = r1.astype(BF16)
    x3 = (r1 - x2.astype(F32)).astype(BF16)
    return x1, x2, x3


def _dot_exact_lhs(m_bf16, x):
    x1, x2, x3 = _split3(x)
    f = lambda p: jnp.dot(m_bf16, p, preferred_element_type=F32)
    return f(x1) + f(x2) + f(x3)


def _dot_exact_rhs(x, m_bf16):
    x1, x2, x3 = _split3(x)
    f = lambda p: jnp.dot(p, m_bf16, preferred_element_type=F32)
    return f(x1) + f(x2) + f(x3)


def _dot_f32(x, w):
    x1, x2, _ = _split3(x)
    w1, w2, _ = _split3(w)
    f = lambda p, q: jnp.dot(p, q, preferred_element_type=F32)
    return f(x1, w1) + (f(x1, w2) + f(x2, w1))


def _sigmoid(x):
    return 1.0 / (1.0 + jnp.exp(-x))


def _silu(x):
    return x * _sigmoid(x)


def _softplus(x):
    return jnp.maximum(x, 0.0) + jnp.log(1.0 + jnp.exp(-jnp.abs(x)))


def _rms(x, gain):
    return x * lax.rsqrt(jnp.mean(x * x, axis=-1, keepdims=True) + NORM_EPS) * gain


def _iota2(shape, dim):
    return lax.broadcasted_iota(jnp.int32, shape, dim)


def _tri_incl_bf16(n):
    return jnp.where(_iota2((n, n), 0) >= _iota2((n, n), 1), 1.0, 0.0).astype(BF16)


def _norm_proj_kernel(h_ref, g_ref, w_ref, o_ref):
    o_ref[...] = _dot(_rms(h_ref[...], g_ref[...]), w_ref[...])


def _norm_proj(h, gain, w_bf16, tm, tn):
    n, d = h.shape
    c = w_bf16.shape[1]
    return pl.pallas_call(
        _norm_proj_kernel,
        grid=(c // tn, n // tm),
        in_specs=[pl.BlockSpec((tm, d), lambda j, i: (i, 0)),
                  pl.BlockSpec((1, d), lambda j, i: (0, 0)),
                  pl.BlockSpec((d, tn), lambda j, i: (0, j))],
        out_specs=pl.BlockSpec((tm, tn), lambda j, i: (i, j)),
        out_shape=jax.ShapeDtypeStruct((n, c), F32),
        compiler_params=_cparams(("arbitrary", "arbitrary")),
        name="norm_proj",
    )(h, gain.reshape(1, d), w_bf16)


def _gathered_combine(h_ref, route_ref, dest_ref, destn_ref, ys_hbm, yg, gsem):
    i = pl.program_id(0)
    nt = pl.num_programs(0)
    slot = lax.rem(i, 2)
    tm = h_ref.shape[0]

    def row_copy(idx_ref, buf_slot, t, c):
        return pltpu.make_async_copy(ys_hbm.at[pl.ds(idx_ref[0, 0, TOP_K * t + c], 1), :],
                                     yg.at[buf_slot, c, pl.ds(t, 1), :], gsem.at[buf_slot])

    def wait_tile(buf_slot):
        for c in range(TOP_K):
            pltpu.make_async_copy(ys_hbm.at[pl.ds(0, tm), :], yg.at[buf_slot, c], gsem.at[buf_slot]).wait()

    @pl.when(i == 0)
    def _():
        def body(g, carry):
            for u in range(DMA_UNROLL):
                for c in range(TOP_K):
                    row_copy(dest_ref, 0, g * DMA_UNROLL + u, c).start()
            return carry
        lax.fori_loop(0, tm // DMA_UNROLL, body, 0)

    wait_tile(slot)
    for t in range(tm):
        for c in range(TOP_K):
            row_copy(destn_ref, 1 - slot, t, c).start()
    r = route_ref[...]
    x = h_ref[...] + r[:, 2:3] * yg[slot, 0] + r[:, 3:4] * yg[slot, 1]

    def finalize():
        @pl.when(i == nt - 1)
        def _():
            wait_tile(1 - slot)
    return x, finalize


def _combine_norm_proj_kernel(h_ref, route_ref, dest_ref, destn_ref, g_ref, w_ref, ys_hbm, hnew_ref, o_ref,
                              yg, gsem):
    x, finalize = _gathered_combine(h_ref, route_ref, dest_ref, destn_ref, ys_hbm, yg, gsem)
    hnew_ref[...] = x
    o_ref[...] = _dot(_rms(x, g_ref[...]), w_ref[...])
    finalize()


def _combine_kernel(h_ref, route_ref, dest_ref, destn_ref, ys_hbm, o_ref, yg, gsem):
    x, finalize = _gathered_combine(h_ref, route_ref, dest_ref, destn_ref, ys_hbm, yg, gsem)
    o_ref[...] = x
    finalize()


def _combine_specs(n, d, tm):
    nt = n // tm
    smem = lambda fn: pl.BlockSpec((1, 1, TOP_K * tm), fn, memory_space=pltpu.SMEM)
    in_specs = [pl.BlockSpec((tm, d), lambda i: (i, 0)),
                pl.BlockSpec((tm, LANES), lambda i: (i, 0)),
                smem(lambda i: (i, 0, 0)),
                smem(lambda i: (jnp.minimum(i + 1, nt - 1), 0, 0))]
    scratch = [pltpu.VMEM((2, TOP_K, tm, d), F32), pltpu.SemaphoreType.DMA((2,))]
    return in_specs, scratch


def _combine_norm_proj(h, ys, route, dest3, gain, w_bf16, tm):
    n, d = h.shape
    c = w_bf16.shape[1]
    in_specs, scratch = _combine_specs(n, d, tm)
    return pl.pallas_call(
        _combine_norm_proj_kernel,
        grid=(n // tm,),
        in_specs=in_specs + [pl.BlockSpec((1, d), lambda i: (0, 0)),
                             pl.BlockSpec((d, c), lambda i: (0, 0)),
                             pl.BlockSpec(memory_space=pl.ANY)],
        out_specs=[pl.BlockSpec((tm, d), lambda i: (i, 0)),
                   pl.BlockSpec((tm, c), lambda i: (i, 0))],
        out_shape=[jax.ShapeDtypeStruct((n, d), F32), jax.ShapeDtypeStruct((n, c), F32)],
        scratch_shapes=scratch,
        compiler_params=_cparams(("arbitrary",)),
        name="combine_norm_proj",
    )(h, route, dest3, dest3, gain.reshape(1, d), w_bf16, ys)


def _combine(h, ys, route, dest3, tm):
    n, d = h.shape
    in_specs, scratch = _combine_specs(n, d, tm)
    return pl.pallas_call(
        _combine_kernel,
        grid=(n // tm,),
        in_specs=in_specs + [pl.BlockSpec(memory_space=pl.ANY)],
        out_specs=pl.BlockSpec((tm, d), lambda i: (i, 0)),
        out_shape=jax.ShapeDtypeStruct((n, d), F32),
        scratch_shapes=scratch,
        compiler_params=_cparams(("arbitrary",)),
        name="combine",
    )(h, route, dest3, dest3, ys)


def _shifted_taps(buf, u_ref, n_taps):
    buf[TAIL:TAIL + CHUNK, :] = u_ref[...]
    taps = [buf[TAIL - j:TAIL - j + CHUNK, :] for j in range(n_taps)]
    return taps


def _carry_tail(buf):
    buf[0:TAIL, :] = buf[CHUNK:CHUNK + TAIL, :]


def _ssd_stages(z_ref, x_ref, bc_ref, dt_ref, cwx_ref, cbx_ref, cwbc_ref, cbbc_ref, dtb_ref, alog_ref,
                dskip_ref, nw_ref, hexp_ref, o_ref, xbuf, bcbuf, state):
    st = {}
    gw = SSD_INNER // SSD_GROUPS

    def init():
        @pl.when(pl.program_id(1) == 0)
        def _():
            xbuf[0:TAIL, :] = jnp.zeros((TAIL, SSD_INNER), F32)
            bcbuf[0:TAIL, :] = jnp.zeros((TAIL, SSD_INNER), F32)
            state[...] = jnp.zeros_like(state)

    def conv(buf, u_ref, w_ref, b_ref):
        taps = _shifted_taps(buf, u_ref, SSD_CONV)
        acc = b_ref[...] + taps[0] * w_ref[3:4, :]
        for j in range(1, SSD_CONV):
            acc = acc + taps[j] * w_ref[3 - j:4 - j, :]
        _carry_tail(buf)
        return _silu(acc)

    def convs():
        st["xs"] = conv(xbuf, x_ref, cwx_ref, cbx_ref)
        st["bc"] = conv(bcbuf, bc_ref, cwbc_ref, cbbc_ref)

    def decays():
        lane = _iota2((CHUNK, LANES), 1)
        dt = _softplus(dt_ref[...] + dtb_ref[...])
        adt = jnp.where(lane < SSD_HEADS, -jnp.exp(alog_ref[...]) * dt, 0.0)
        cum = _dot_exact_lhs(_tri_incl_bf16(CHUNK), adt)
        hexp = hexp_ref[...]
        cum_full = _dot_exact_rhs(cum, hexp)
        tot_full = cum_full[CHUNK - 1:CHUNK, :]
        xd = st["xs"] * _dot_exact_rhs(dt, hexp)
        st.update(cum=cum, cum_t=cum.T, tot_full=tot_full, xd=xd, xds=xd * jnp.exp(tot_full - cum_full),
                  eac=jnp.exp(cum_full), lane_lo=lane < SSD_HEAD_DIM,
                  causal=_iota2((CHUNK, CHUNK), 0) >= _iota2((CHUNK, CHUNK), 1), y_parts=[])

    def group(g):
        bc, cum, cum_t, xd = st["bc"], st["cum"], st["cum_t"], st["xd"]
        bg = bc[:, g * SSD_STATE:(g + 1) * SSD_STATE]
        cg = bc[:, (SSD_GROUPS + g) * SSD_STATE:(SSD_GROUPS + g + 1) * SSD_STATE]
        cb = _dot_nt(cg, bg)
        s_prev = state[:, g * gw:(g + 1) * gw]
        y_off = _dot(cg, s_prev) * st["eac"][:, g * gw:(g + 1) * gw]
        s_new = _dot(bg.T, st["xds"][:, g * gw:(g + 1) * gw])
        state[:, g * gw:(g + 1) * gw] = jnp.exp(st["tot_full"][:, g * gw:(g + 1) * gw]) * s_prev + s_new
        for pr in range(2):
            lo = g * gw + pr * LANES
            xd_pair = xd[:, lo:lo + LANES]
            yd = jnp.zeros((CHUNK, LANES), F32)
            for k in range(2):
                h = (lo // SSD_HEAD_DIM) + k
                diff = cum[:, h:h + 1] - cum_t[h:h + 1, :]
                decay = jnp.exp(jnp.where(st["causal"], diff, NEG_BIG))
                keep = st["lane_lo"] if k == 0 else jnp.logical_not(st["lane_lo"])
                yd = yd + _dot(cb * decay, jnp.where(keep, xd_pair, 0.0))
            st["y_parts"].append(yd + y_off[:, pr * LANES:(pr + 1) * LANES])

    def finish():
        y = jnp.concatenate(st["y_parts"], axis=1) + dskip_ref[...] * st["xs"]
        y = y * _silu(z_ref[...])
        outs = []
        for g in range(SSD_GROUPS):
            yg = y[:, g * gw:(g + 1) * gw]
            outs.append(yg * lax.rsqrt(jnp.mean(yg * yg, axis=-1, keepdims=True) + 1e-5))
        o_ref[...] = (jnp.concatenate(outs, axis=1) * nw_ref[...]).astype(o_ref.dtype)

    return [init, convs, decays] + [functools.partial(group, g) for g in range(SSD_GROUPS)] + [finish]


def _head_expand(n_heads, head_dim):
    h = jnp.arange(LANES)[:, None]
    l = jnp.arange(n_heads * head_dim)[None, :]
    return (l // head_dim == h).astype(BF16)


def _ssd_operands(proj, nc, conv_w, conv_b, dt_bias, a_log, d_skip, norm_w):
    pad16 = lambda v: jnp.pad(v.astype(F32), (0, LANES - v.shape[0])).reshape(1, LANES)
    row_spec = lambda w, blk: pl.BlockSpec((CHUNK, w), lambda b, c: (b * nc + c, blk))
    full = lambda shape: pl.BlockSpec(shape, lambda b, c: (0, 0))
    in_specs = [row_spec(SSD_INNER, 0),
                row_spec(SSD_INNER, 1),
                row_spec(SSD_INNER, 2),
                row_spec(LANES, 50),
                full((SSD_CONV, SSD_INNER)), full((1, SSD_INNER)),
                full((SSD_CONV, SSD_INNER)), full((1, SSD_INNER)),
                full((1, LANES)), full((1, LANES)),
                full((1, SSD_INNER)), full((1, SSD_INNER)),
                full((LANES, SSD_INNER))]
    operands = [proj, proj, proj, proj,
                conv_w[:, :SSD_INNER], conv_b[:SSD_INNER].reshape(1, -1),
                conv_w[:, SSD_INNER:], conv_b[SSD_INNER:].reshape(1, -1),
                pad16(dt_bias), pad16(a_log),
                jnp.repeat(d_skip.astype(F32), SSD_HEAD_DIM).reshape(1, -1), norm_w.reshape(1, -1),
                _head_expand(SSD_HEADS, SSD_HEAD_DIM)]
    scratch = [pltpu.VMEM((TAIL + CHUNK, SSD_INNER), F32),
               pltpu.VMEM((TAIL + CHUNK, SSD_INNER), F32),
               pltpu.VMEM((SSD_STATE, SSD_INNER), F32)]
    return in_specs, operands, scratch


def _rwkv_kernel(r_ref, k_ref, v_ref, lo_ref, mur_ref, muk_ref, muv_ref, mulo_ref, w0_ref, w2_ref, a0_ref,
                 a2_ref, g2_ref, kk_ref, ka_ref, rk_ref, lnw_ref, lnb_ref, o_ref,
                 rbuf, kbuf, vbuf, lobuf, state, side_work=()):
    side = iter(side_work)
    run_side = lambda: next(side, lambda: None)()
    run_side()
    c = pl.program_id(1)

    @pl.when(c == 0)
    def _():
        rbuf[0:TAIL, :] = jnp.zeros((TAIL, RWKV_DIM), F32)
        kbuf[0:TAIL, :] = jnp.zeros((TAIL, RWKV_DIM), F32)
        vbuf[0:TAIL, :] = jnp.zeros((TAIL, RWKV_DIM), F32)
        lobuf[0:TAIL, :] = jnp.zeros((TAIL, 2 * LANES), F32)
        state[...] = jnp.zeros_like(state)

    def shift(buf, u_ref, mu_ref):
        cur, prev = _shifted_taps(buf, u_ref, 2)
        _carry_tail(buf)
        return cur + mu_ref[...] * (prev - cur)

    r = shift(rbuf, r_ref, mur_ref)
    k = shift(kbuf, k_ref, muk_ref)
    v = shift(vbuf, v_ref, muv_ref)
    lo = shift(lobuf, lo_ref, mulo_ref)
    wa = lo[:, :LANES]
    w = -_softplus(-(w0_ref[...] + _dot(jnp.tanh(wa), w2_ref[...]))) - 0.5
    logw = -jnp.exp(w)
    a = _sigmoid(a0_ref[...] + _dot(wa, a2_ref[...]))
    g = _dot(_sigmoid(lo[:, LANES:]), g2_ref[...])

    lane = _iota2((CHUNK, LANES), 1)
    head0 = lane < RWKV_HEAD_DIM
    bd = (_iota2((LANES, LANES), 0) // RWKV_HEAD_DIM) == (_iota2((LANES, LANES), 1) // RWKV_HEAD_DIM)
    bd_ones = jnp.where(bd, 1.0, 0.0).astype(BF16)

    def head_sum(x):
        return jnp.dot(x.astype(BF16), bd_ones, preferred_element_type=F32)

    tri = _tri_incl_bf16(CHUNK)
    cum = _dot_exact_lhs(tri, logw)
    cume = cum - logw
    cmid = cum[CHUNK // 2 - 1:CHUNK // 2, :]
    cend = cum[CHUNK - 1:CHUNK, :]
    e_in_mid = jnp.exp(cum - cmid)
    e_ex_mid = jnp.exp(cume - cmid)
    e_mid_in = jnp.exp(cmid - cum)
    e_ex = jnp.exp(cume)
    e_in = jnp.exp(cum)
    e_end = jnp.exp(cend - cum)
    e_tot = jnp.exp(cend)

    row = _iota2((CHUNK, CHUNK), 0)
    col = _iota2((CHUNK, CHUNK), 1)
    strict = row > col
    incl = row >= col
    zeros = jnp.zeros((CHUNK, LANES), F32)

    n_blocks = RWKV_DIM // LANES
    blk = []
    for p in range(n_blocks):
        sl = slice(p * LANES, (p + 1) * LANES)
        rp, vp, ap = r[:, sl], v[:, sl], a[:, sl]
        kkp = k[:, sl] * kk_ref[:, sl]
        kkp = kkp / jnp.maximum(jnp.sqrt(head_sum(kkp * kkp)), 1e-12)
        kp = k[:, sl] * (1.0 + (ap - 1.0) * ka_ref[:, sl])
        aap = -kkp
        bp = kkp * ap
        a_mid = aap * e_ex_mid[:, sl]
        r_mid = rp * e_in_mid[:, sl]
        bf = lambda x: x.astype(BF16)
        lhs = jnp.concatenate([bf(jnp.where(head0, a_mid, 0.0)), bf(jnp.where(head0, 0.0, a_mid)),
                               bf(jnp.where(head0, r_mid, 0.0)), bf(jnp.where(head0, 0.0, r_mid))], axis=0)
        rhs = jnp.concatenate([bf(bp * e_mid_in[:, sl]), bf(kp * e_mid_in[:, sl])], axis=0)
        a_abs = aap * e_ex[:, sl]
        blk.append(dict(sl=sl, rp=rp, vp=vp, kp=kp, r_abs=rp * e_in[:, sl],
                        a_abs=(bf(jnp.where(head0, a_abs, 0.0)), bf(jnp.where(head0, 0.0, a_abs))),
                        vm=(bf(jnp.where(head0, vp, 0.0)), bf(jnp.where(head0, 0.0, vp))),
                        b_end=bf(bp * e_end[:, sl]), k_end=bf(kp * e_end[:, sl]), prod=_dot_nt(lhs, rhs)))
    run_side()

    heads = []
    for p in range(n_blocks):
        prod = blk[p]["prod"]
        for hh in range(2):
            heads.append(dict(
                vm=blk[p]["vm"][hh], a_abs=blk[p]["a_abs"][hh],
                a_ab=jnp.where(strict, prod[hh * CHUNK:(hh + 1) * CHUNK, :CHUNK], 0.0),
                a_ak=jnp.where(strict, prod[hh * CHUNK:(hh + 1) * CHUNK, CHUNK:], 0.0).astype(BF16),
                m_rb=jnp.where(incl, prod[(2 + hh) * CHUNK:(3 + hh) * CHUNK, :CHUNK], 0.0).astype(BF16),
                m_rk=jnp.where(incl, prod[(2 + hh) * CHUNK:(3 + hh) * CHUNK, CHUNK:], 0.0).astype(BF16)))

    eye = jnp.where(row == col, 1.0, 0.0)
    n_levels = int(math.log2(CHUNK)) - 1
    ts = [eye + hd["a_ab"] for hd in heads]
    xs = [hd["a_ab"].astype(BF16) for hd in heads]
    avs = [_dot(hd["a_ak"], hd["vm"]).astype(BF16) for hd in heads]
    run_side()
    xs = [jnp.dot(x, x, preferred_element_type=F32).astype(BF16) for x in xs]
    for _ in range(n_levels - 1):
        run_side()
        zs = [jnp.dot(x, jnp.concatenate([x, t.astype(BF16)], axis=1), preferred_element_type=F32)
              for x, t in zip(xs, ts)]
        xs = [z[:, :CHUNK].astype(BF16) for z in zs]
        ts = [t + z[:, CHUNK:] for t, z in zip(ts, zs)]
    ts = [t + jnp.dot(x, t.astype(BF16), preferred_element_type=F32) for x, t in zip(xs, ts)]
    for _ in side:
        _()

    wmats = [_dot(t, jnp.concatenate([hd["a_abs"], av], axis=1)).astype(BF16)
             for t, hd, av in zip(ts, heads, avs)]
    zeros_bf = zeros.astype(BF16)
    outs = [_dot(jnp.concatenate([hd["m_rb"], hd["m_rk"]], axis=1),
                 jnp.concatenate([jnp.concatenate([wm[:, LANES:], wm[:, :LANES]], axis=1),
                                  jnp.concatenate([hd["vm"], zeros_bf], axis=1)], axis=0))
            for wm, hd in zip(wmats, heads)]

    zts = []
    for p in range(n_blocks):
        w0h, w1h = wmats[2 * p], wmats[2 * p + 1]
        ui = w0h[:, LANES:] + w1h[:, LANES:]
        a_eff = w0h[:, :LANES] + w1h[:, :LANES]
        zts.append(_dot_tn(jnp.concatenate([jnp.concatenate([ui, a_eff], axis=1),
                                            jnp.concatenate([blk[p]["vp"].astype(BF16), zeros_bf], axis=1)],
                                           axis=0),
                           jnp.concatenate([blk[p]["b_end"], blk[p]["k_end"]], axis=0)))

    ys = []
    for p in range(n_blocks):
        s0 = state[p]
        o0, o1 = outs[2 * p], outs[2 * p + 1]
        r_eff = blk[p]["r_abs"] + o0[:, LANES:] + o1[:, LANES:]
        ys.append(_dot_nt(r_eff, s0) + o0[:, :LANES] + o1[:, :LANES])
        h_intra = jnp.where(bd, zts[p][:LANES, :], 0.0)
        g_corr = jnp.where(bd, zts[p][LANES:, :], 0.0)
        state[p] = s0 * e_tot[:, blk[p]["sl"]] + _dot(s0, g_corr) + h_intra

    means = [head_sum(y) * (1.0 / RWKV_HEAD_DIM) for y in ys]
    devs = [y - m for y, m in zip(ys, means)]
    vars_ = [head_sum(dv * dv) * (1.0 / RWKV_HEAD_DIM) for dv in devs]
    for p in range(n_blocks):
        sl = blk[p]["sl"]
        yn = devs[p] * lax.rsqrt(vars_[p] + RWKV_GN_EPS) * lnw_ref[:, sl] + lnb_ref[:, sl]
        bonus = head_sum(blk[p]["rp"] * blk[p]["kp"] * rk_ref[:, sl])
        o_ref[:, sl] = ((yn + bonus * blk[p]["vp"]) * g[:, sl]).astype(o_ref.dtype)


def _rwkv_operands(proj, nc, mu, w0, w2, a0, a2, g2, k_k, k_a, r_k, ln_w, ln_b):
    d = RWKV_DIM
    row_spec = lambda w, blk: pl.BlockSpec((CHUNK, w), lambda b, c: (b * nc + c, blk))
    full = lambda shape: pl.BlockSpec(shape, lambda b, c: (0,) * len(shape))
    vec = lambda x: x.astype(F32).reshape(1, -1)
    w2p = jnp.concatenate([w2, jnp.zeros((AAA_LORA, d), w2.dtype)], axis=0).astype(BF16)
    a2p = jnp.concatenate([jnp.zeros((DECAY_LORA, d), a2.dtype), a2], axis=0).astype(BF16)
    in_specs = [row_spec(d, 3), row_spec(d, 4), row_spec(d, 5), row_spec(2 * LANES, 24),
                full((1, d)), full((1, d)), full((1, d)), full((1, 2 * LANES)),
                full((1, d)), full((LANES, d)), full((1, d)), full((LANES, d)), full((LANES, d)),
                full((1, d)), full((1, d)), full((1, d)), full((1, d)), full((1, d))]
    operands = [proj, proj, proj, proj,
                vec(mu[:d]), vec(mu[d:2 * d]), vec(mu[2 * d:3 * d]), vec(mu[3 * d:]),
                vec(w0), w2p, vec(a0), a2p, g2.astype(BF16),
                vec(k_k), vec(k_a), vec(r_k), vec(ln_w), vec(ln_b)]
    scratch = [pltpu.VMEM((TAIL + CHUNK, d), F32), pltpu.VMEM((TAIL + CHUNK, d), F32),
               pltpu.VMEM((TAIL + CHUNK, d), F32), pltpu.VMEM((TAIL + CHUNK, 2 * LANES), F32),
               pltpu.VMEM((d // LANES, LANES, LANES), F32)]
    return in_specs, operands, scratch


def _mixers_kernel(*refs, n_ssd_in, n_rwkv_in, n_ssd_scratch):
    ssd_in = refs[:n_ssd_in]
    rwkv_in = refs[n_ssd_in:n_ssd_in + n_rwkv_in]
    o_ssd, o_rwkv = refs[n_ssd_in + n_rwkv_in:n_ssd_in + n_rwkv_in + 2]
    scratch = refs[n_ssd_in + n_rwkv_in + 2:]
    _rwkv_kernel(*rwkv_in, o_rwkv, *scratch[n_ssd_scratch:],
                 side_work=_ssd_stages(*ssd_in, o_ssd, *scratch[:n_ssd_scratch]))


def _mixers(proj, batch, seq, ssd_params, rwkv_params):
    nc = seq // CHUNK
    n = batch * seq
    s_specs, s_ops, s_scratch = _ssd_operands(proj, nc, *ssd_params)
    r_specs, r_ops, r_scratch = _rwkv_operands(proj, nc, *rwkv_params)
    out_spec = lambda w: pl.BlockSpec((CHUNK, w), lambda b, c: (b * nc + c, 0))
    return pl.pallas_call(
        functools.partial(_mixers_kernel, n_ssd_in=len(s_specs), n_rwkv_in=len(r_specs),
                          n_ssd_scratch=len(s_scratch)),
        grid=(batch, nc),
        in_specs=s_specs + r_specs,
        out_specs=[out_spec(SSD_INNER), out_spec(RWKV_DIM)],
        out_shape=[jax.ShapeDtypeStruct((n, SSD_INNER), BF16), jax.ShapeDtypeStruct((n, RWKV_DIM), BF16)],
        scratch_shapes=s_scratch + r_scratch,
        compiler_params=_cparams(("arbitrary", "arbitrary")),
        name="mixers",
    )(*s_ops, *r_ops)


def _route(logits):
    lane = _iota2(logits.shape, 1)
    lanef = lane.astype(F32)
    big = float(LANES)

    def first_max(x):
        m = jnp.max(x, axis=-1, keepdims=True)
        idx = jnp.min(jnp.where(x == m, lanef, big), axis=-1, keepdims=True)
        return m, idx

    cl = jnp.where(lane < EXPERT_GROUPS, logits, NEG_BIG)
    cmax, grp = first_max(cl)
    p_group = 1.0 / jnp.sum(jnp.exp(cl - cmax), axis=-1, keepdims=True)
    lo = EXPERT_GROUPS + grp * EXPERTS_PER_GROUP
    fl = jnp.where((lanef >= lo) & (lanef < lo + EXPERTS_PER_GROUP), logits, NEG_BIG)
    m0, i0 = first_max(fl)
    m1, i1 = first_max(jnp.where(lanef == i0, NEG_BIG, fl))
    e1 = jnp.exp(m1 - m0)
    g0 = p_group / (1.0 + e1)
    g1 = p_group * e1 / (1.0 + e1)
    return jnp.where(lane == 0, i0 - EXPERT_GROUPS,
                     jnp.where(lane == 1, i1 - EXPERT_GROUPS,
                               jnp.where(lane == 2, g0, jnp.where(lane == 3, g1, 0.0))))


def _outproj_router_kernel(*refs, n_in):
    ys = refs[:n_in]
    ws = refs[n_in:2 * n_in]
    h_ref, g_ref, wr_ref, br_ref, hnew_ref, hf_ref, route_ref = refs[2 * n_in:]
    acc = h_ref[...]
    for y_ref, w_ref in zip(ys, ws):
        acc = acc + jnp.dot(y_ref[...], w_ref[...], preferred_element_type=F32)
    hnew_ref[...] = acc
    hf = _rms(acc, g_ref[...])
    hf_ref[...] = hf
    route_ref[...] = _route(_dot_f32(hf, wr_ref[...]) + br_ref[...])


def _outproj_router(ys, ws, h, gain, w_router, b_router, tm):
    n, d = h.shape
    n_in = len(ys)
    in_specs = ([pl.BlockSpec((tm, y.shape[1]), lambda i: (i, 0)) for y in ys]
                + [pl.BlockSpec(w.shape, lambda i: (0, 0)) for w in ws]
                + [pl.BlockSpec((tm, d), lambda i: (i, 0)),
                   pl.BlockSpec((1, d), lambda i: (0, 0)),
                   pl.BlockSpec((d, LANES), lambda i: (0, 0)),
                   pl.BlockSpec((1, LANES), lambda i: (0, 0))])
    return pl.pallas_call(
        functools.partial(_outproj_router_kernel, n_in=n_in),
        grid=(n // tm,),
        in_specs=in_specs,
        out_specs=[pl.BlockSpec((tm, d), lambda i: (i, 0)),
                   pl.BlockSpec((tm, d), lambda i: (i, 0)),
                   pl.BlockSpec((tm, LANES), lambda i: (i, 0))],
        out_shape=[jax.ShapeDtypeStruct((n, d), F32), jax.ShapeDtypeStruct((n, d), F32),
                   jax.ShapeDtypeStruct((n, LANES), F32)],
        compiler_params=_cparams(("arbitrary",)),
        name="outproj_router",
    )(*ys, *ws, h, gain.reshape(1, d), w_router, b_router)


def _router_weights(w_coarse, b_coarse, w_fine, b_fine):
    d = w_coarse.shape[0]
    wf = jnp.transpose(w_fine, (1, 0, 2)).reshape(d, N_EXPERTS)
    w = jnp.concatenate([w_coarse, wf], axis=1).astype(F32)
    b = jnp.concatenate([b_coarse, b_fine.reshape(N_EXPERTS)]).astype(F32)
    pad = LANES - w.shape[1]
    return jnp.pad(w, ((0, 0), (0, pad))), jnp.pad(b, (0, pad)).reshape(1, LANES)


def _moe_plan(route, tm):
    n = route.shape[0]
    a = n * TOP_K
    n_blocks = a // MOE_BLOCK + N_EXPERTS
    e_flat = route[:, :TOP_K].astype(jnp.int32).reshape(a)
    seg = MOE_BLOCK
    onehot = (e_flat[:, None] == jnp.arange(N_EXPERTS, dtype=jnp.int32)[None, :]).astype(F32)
    onehot = onehot.reshape(a // seg, seg, N_EXPERTS)
    tri = jnp.tril(jnp.ones((seg, seg), F32))
    within = jnp.einsum("ij,bjk->bik", tri, onehot)
    tot = within[:, -1, :]
    offs = jnp.cumsum(tot, axis=0) - tot
    rank = (jnp.sum(onehot * (within + offs[:, None, :]), axis=-1) - 1.0).astype(jnp.int32).reshape(a)
    counts = (offs[-1] + tot[-1]).astype(jnp.int32)
    padded = (counts + MOE_BLOCK - 1) // MOE_BLOCK * MOE_BLOCK
    pad_end = jnp.cumsum(padded)
    pad_start = pad_end - padded
    start_of = jnp.sum(jnp.where(e_flat[:, None] == jnp.arange(N_EXPERTS, dtype=jnp.int32)[None, :],
                                 pad_start[None, :], 0), axis=1)
    dest = (start_of + rank).astype(jnp.int32)
    block_start = jnp.arange(n_blocks, dtype=jnp.int32) * MOE_BLOCK
    block_expert = jnp.minimum(jnp.sum((pad_end[None, :] <= block_start[:, None]).astype(jnp.int32), axis=1),
                               N_EXPERTS - 1).astype(jnp.int32)
    n_used = (pad_end[-1] // MOE_BLOCK).astype(jnp.int32).reshape(1)
    pad_lo = (pad_start + counts).astype(jnp.int32)
    return dict(dest3=dest.reshape(n // tm, 1, TOP_K * tm), block_expert=block_expert, n_used=n_used,
                pad_lo=pad_lo, pad_hi=pad_end.astype(jnp.int32), n_blocks=n_blocks)


def _dispatch_kernel(padlo_ref, padhi_ref, nused_ref, hf_ref, dest_ref, xs_hbm, zblk, sem, zsem):
    i = pl.program_id(0)
    tm = hf_ref.shape[0]

    @pl.when(i == 0)
    def _():
        zblk[...] = jnp.zeros_like(zblk)

        def per_expert(e, carry):
            def zero_row(rw, c2):
                pltpu.make_async_copy(zblk.at[pl.ds(0, 1), :], xs_hbm.at[pl.ds(rw, 1), :], zsem).start()
                return c2
            lax.fori_loop(padlo_ref[e], padhi_ref[e], zero_row, 0)

            def wait_row(rw, c2):
                pltpu.make_async_copy(zblk.at[pl.ds(0, 1), :], xs_hbm.at[pl.ds(rw, 1), :], zsem).wait()
                return c2
            lax.fori_loop(padlo_ref[e], padhi_ref[e], wait_row, 0)
            return carry
        lax.fori_loop(0, N_EXPERTS, per_expert, 0)

        n_blocks = xs_hbm.shape[0] // MOE_BLOCK

        def block_copy(b):
            return pltpu.make_async_copy(
                zblk, xs_hbm.at[pl.ds(pl.multiple_of(b * MOE_BLOCK, MOE_BLOCK), MOE_BLOCK), :], zsem)

        def zero_block(b, carry):
            block_copy(b).start()
            return carry
        lax.fori_loop(nused_ref[0], n_blocks, zero_block, 0)

        def wait_block(b, carry):
            block_copy(b).wait()
            return carry
        lax.fori_loop(nused_ref[0], n_blocks, wait_block, 0)

    for t in range(tm):
        for c in range(TOP_K):
            pltpu.make_async_copy(hf_ref.at[pl.ds(t, 1), :],
                                  xs_hbm.at[pl.ds(dest_ref[0, 0, TOP_K * t + c], 1), :], sem).start()
    for c in range(TOP_K):
        pltpu.make_async_copy(hf_ref, xs_hbm.at[pl.ds(0, tm), :], sem).wait()


def _dispatch(hf, plan, tm):
    n, d = hf.shape
    slots = plan["n_blocks"] * MOE_BLOCK
    grid_spec = pltpu.PrefetchScalarGridSpec(
        num_scalar_prefetch=3,
        grid=(n // tm,),
        in_specs=[pl.BlockSpec((tm, d), lambda i, lo, hi, nu: (i, 0)),
                  pl.BlockSpec((1, 1, TOP_K * tm), lambda i, lo, hi, nu: (i, 0, 0), memory_space=pltpu.SMEM)],
        out_specs=pl.BlockSpec(memory_space=pl.ANY),
        scratch_shapes=[pltpu.VMEM((MOE_BLOCK, d), F32), pltpu.SemaphoreType.DMA(()),
                        pltpu.SemaphoreType.DMA(())])
    return pl.pallas_call(
        _dispatch_kernel,
        grid_spec=grid_spec,
        out_shape=jax.ShapeDtypeStruct((slots, d), F32),
        compiler_params=_cparams(("arbitrary",)),
        name="moe_dispatch",
    )(plan["pad_lo"], plan["pad_hi"], plan["n_used"], hf, plan["dest3"])


def _experts_kernel(bexp_ref, nused_ref, x_ref, wg_ref, wu_ref, wd_ref, y_ref, wg_s, wu_s, wd_s):
    i = pl.program_id(0)
    active = i < nused_ref[0]

    @pl.when(active & ((i == 0) | (bexp_ref[i] != bexp_ref[jnp.maximum(i - 1, 0)])))
    def _():
        wg_s[...] = wg_ref[0].astype(BF16)
        wu_s[...] = wu_ref[0].astype(BF16)
        wd_s[...] = wd_ref[0].astype(BF16)

    @pl.when(active)
    def _():
        x = x_ref[...].astype(BF16)
        hg = jnp.dot(x, wg_s[...], preferred_element_type=F32)
        hu = jnp.dot(x, wu_s[...], preferred_element_type=F32)
        hb = (_silu(hg) * hu).astype(BF16)
        y_ref[...] = jnp.dot(hb, wd_s[...], preferred_element_type=F32)

    @pl.when(i >= nused_ref[0])
    def _():
        y_ref[...] = jnp.zeros_like(y_ref)


def _experts(xs, plan, layer, wg, wu, wd):
    slots, d = xs.shape
    n_blocks = plan["n_blocks"]
    used = lambda i, nu: jnp.minimum(i, nu[0] - 1)
    w_idx = lambda i, be, nu: (layer, be[used(i, nu)], 0, 0)
    grid_spec = pltpu.PrefetchScalarGridSpec(
        num_scalar_prefetch=2,
        grid=(n_blocks,),
        in_specs=[pl.BlockSpec((MOE_BLOCK, d), lambda i, be, nu: (used(i, nu), 0)),
                  pl.BlockSpec((None, 1, d, EXPERT_HIDDEN), w_idx),
                  pl.BlockSpec((None, 1, d, EXPERT_HIDDEN), w_idx),
                  pl.BlockSpec((None, 1, EXPERT_HIDDEN, d), w_idx)],
        out_specs=pl.BlockSpec((MOE_BLOCK, d), lambda i, be, nu: (i, 0)),
        scratch_shapes=[pltpu.VMEM((d, EXPERT_HIDDEN), BF16), pltpu.VMEM((d, EXPERT_HIDDEN), BF16),
                        pltpu.VMEM((EXPERT_HIDDEN, d), BF16)])
    return pl.pallas_call(
        _experts_kernel,
        grid_spec=grid_spec,
        out_shape=jax.ShapeDtypeStruct((slots, d), F32),
        compiler_params=_cparams(("arbitrary",)),
        name="moe_experts",
    )(plan["block_expert"], plan["n_used"], xs, wg, wu, wd)


def _swa_kernel(q_ref, kvp_ref, kvc_ref, qg_ref, kg_ref, slope_ref, sink_ref, o_ref):
    jb = pl.program_id(1)
    blk = CHUNK
    qi = _iota2((blk, blk), 0)
    kj = _iota2((blk, blk), 1)
    from_prev = kj > qi
    deltaf = jnp.where(from_prev, qi + blk - kj, qi - kj).astype(F32)
    no_prev = jnp.where(from_prev, jnp.where(jb > 0, 0.0, NEG_BIG), 0.0)
    scale = HEAD_DIM ** -0.5

    bd = (_iota2((LANES, LANES), 0) // HEAD_DIM) == (_iota2((LANES, LANES), 1) // HEAD_DIM)
    bd_ones = jnp.where(bd, 1.0, 0.0).astype(BF16)

    def head_rms(x, gain):
        sq = x * x
        s1 = sq.astype(BF16)
        s2 = (sq - s1.astype(F32)).astype(BF16)
        ms = (jnp.dot(s1, bd_ones, preferred_element_type=F32)
              + jnp.dot(s2, bd_ones, preferred_element_type=F32)) * (1.0 / HEAD_DIM)
        return x * lax.rsqrt(ms + NORM_EPS) * gain

    n_kv_blk = KV_DIM // LANES
    kv = jnp.concatenate([kvp_ref[...], kvc_ref[...]], axis=0)
    lane_kv = _iota2((kv.shape[0], LANES), 1)
    kv_half = (lane_kv < HEAD_DIM, lane_kv >= HEAD_DIM)
    lane_q = _iota2((blk, LANES), 1)
    q_half = (lane_q < HEAD_DIM, lane_q >= HEAD_DIM)
    kn = [head_rms(kv[:, j * LANES:(j + 1) * LANES], kg_ref[...]) for j in range(n_kv_blk)]
    vb = [kv[:, KV_DIM + j * LANES:KV_DIM + (j + 1) * LANES] for j in range(n_kv_blk)]
    kn_sw = [pltpu.roll(x, HEAD_DIM, 1) for x in kn]
    vb_sw = [pltpu.roll(x, HEAD_DIM, 1) for x in vb]
    v_same = [jnp.where(kv_half[g % 2], vb[g // 2], 0.0) for g in range(KV_HEADS)]
    v_swap = [jnp.where(kv_half[1 - g % 2], vb_sw[g // 2], 0.0) for g in range(KV_HEADS)]
    qn = [head_rms(q_ref[:, j * LANES:(j + 1) * LANES], qg_ref[...]) for j in range(Q_DIM // LANES)]

    rep = lambda x: jnp.concatenate([x] * Q_PER_KV, axis=0)
    from_prev4, delta4 = rep(from_prev), rep(deltaf)
    neg4 = [rep(no_prev)] + [None] * (SWA_QBLOCKS - 1)
    col = lambda ref, hs: jnp.concatenate([jnp.broadcast_to(ref[:, h:h + 1], (blk, 1)) for h in hs], axis=0)
    order = lambda g: [g * Q_PER_KV + g % 2, g * Q_PER_KV + g % 2 + 2,
                       g * Q_PER_KV + 1 - g % 2, g * Q_PER_KV + 3 - g % 2]
    units = [(u, g) for u in range(SWA_QBLOCKS) for g in range(KV_HEADS)]
    keys = lambda x, u: x[u * blk:(u + 2) * blk]
    qrow = lambda x, u: x[u * blk:(u + 1) * blk]

    qms = [[jnp.where(q_half[h % 2], qrow(qn[h // 2], u), 0.0) for h in order(g)] for u, g in units]
    scs = [jnp.concatenate([_dot_nt(jnp.concatenate(qm[:2], axis=0), keys(kn[g // 2], u)),
                            _dot_nt(jnp.concatenate(qm[2:], axis=0), keys(kn_sw[g // 2], u))], axis=0)
           for qm, (u, g) in zip(qms, units)]
    sinks = [col(sink_ref, order(g)) for u, g in units]
    ss = []
    for sc, (u, g) in zip(scs, units):
        s = jnp.where(from_prev4, sc[:, :blk], sc[:, blk:]) * scale - col(slope_ref, order(g)) * delta4
        ss.append(s if neg4[u] is None else s + neg4[u])
    ms = [jnp.maximum(jnp.max(s, axis=-1, keepdims=True), sk) for s, sk in zip(ss, sinks)]
    ps = [jnp.exp(s - m) for s, m in zip(ss, ms)]
    invs = [1.0 / (jnp.sum(p, axis=-1, keepdims=True) + jnp.exp(sk - m)) for p, sk, m in zip(ps, sinks, ms)]
    pcats = [jnp.concatenate([jnp.where(from_prev4, p, 0.0), jnp.where(from_prev4, 0.0, p)], axis=1) for p in ps]
    outs = [jnp.concatenate([_dot(pc[:2 * blk], keys(v_same[g], u)), _dot(pc[2 * blk:], keys(v_swap[g], u))],
                            axis=0) * inv
            for pc, inv, (u, g) in zip(pcats, invs, units)]
    for o, (u, g) in zip(outs, units):
        hs = order(g)
        for j in sorted({h // 2 for h in hs}):
            pair = sum(o[idx * blk:(idx + 1) * blk] for idx, h in enumerate(hs) if h // 2 == j)
            o_ref[u * blk:(u + 1) * blk, j * LANES:(j + 1) * LANES] = pair.astype(o_ref.dtype)


def _swa(proj, batch, seq, q_gain, k_gain, sinks):
    nbk = seq // CHUNK
    qrows = SWA_QBLOCKS * CHUNK
    nsteps = seq // qrows
    n = batch * seq
    slopes = (2.0 ** (-8.0 * jnp.arange(1, ATT_HEADS + 1, dtype=F32) / ATT_HEADS))
    pad = lambda v: jnp.pad(v.astype(F32), (0, LANES - v.shape[0])).reshape(1, LANES)
    pair = lambda v: jnp.tile(v.astype(F32), LANES // HEAD_DIM).reshape(1, LANES)
    full = lambda shape: pl.BlockSpec(shape, lambda b, j: (0, 0))
    kvw = 2 * KV_DIM
    return pl.pallas_call(
        _swa_kernel,
        grid=(batch, nsteps),
        in_specs=[pl.BlockSpec((qrows, Q_DIM), lambda b, j: (b * nsteps + j, 0)),
                  pl.BlockSpec((CHUNK, kvw),
                               lambda b, j: (b * nbk + jnp.maximum(SWA_QBLOCKS * j - 1, 0), Q_DIM // kvw)),
                  pl.BlockSpec((qrows, kvw), lambda b, j: (b * nsteps + j, Q_DIM // kvw)),
                  full((1, LANES)), full((1, LANES)), full((1, LANES)), full((1, LANES))],
        out_specs=pl.BlockSpec((qrows, Q_DIM), lambda b, j: (b * nsteps + j, 0)),
        out_shape=jax.ShapeDtypeStruct((n, Q_DIM), BF16),
        compiler_params=_cparams(("arbitrary", "arbitrary")),
        name="swa",
    )(proj, proj, proj, pair(q_gain), pair(k_gain), pad(slopes), pad(sinks))


def _even_in_weight(w):
    rw = SSD_COLS
    cols = jnp.concatenate([w[:, :SSD_INNER + SSD_CONV_DIM], w[:, rw:],
                            w[:, SSD_INNER + SSD_CONV_DIM:SSD_COLS]], axis=1)
    return jnp.pad(cols, ((0, 0), (0, EVEN_COLS_PAD - cols.shape[1]))).astype(BF16)


def kernel(x, ln_mix, ln_ffn, e_w_in, e_w_out, ssd_conv_w, ssd_conv_b, ssd_dt_bias, ssd_a_log, ssd_d, ssd_norm,
           rwkv_mu, rwkv_w0, rwkv_w2, rwkv_a0, rwkv_a2, rwkv_g2, rwkv_k_k, rwkv_k_a, rwkv_r_k, rwkv_ln_w,
           rwkv_ln_b, o_w_in, o_w_out, attn_q_norm, attn_k_norm, attn_sinks, moe_w_coarse, moe_b_coarse,
           moe_w_fine, moe_b_fine, moe_w_gate, moe_w_up, moe_w_down):
    batch, seq, d = x.shape
    n = batch * seq
    tm = min(512, n)
    tm_mm = min(1024, n)
    h = x.reshape(n, d)

    def moe(layer, hf, route):
        plan = _moe_plan(route, tm)
        xs = _dispatch(hf, plan, tm)
        ys = _experts(xs, plan, layer, moe_w_gate, moe_w_up, moe_w_down)
        return ys, plan["dest3"]

    def router_w(layer):
        return _router_weights(moe_w_coarse[layer], moe_b_coarse[layer], moe_w_fine[layer], moe_b_fine[layer])

    proj = _norm_proj(h, ln_mix[0], _even_in_weight(e_w_in[0]), tm_mm, EVEN_COLS_PAD // 3)
    y_ssd, y_rwkv = _mixers(
        proj, batch, seq,
        (ssd_conv_w[0], ssd_conv_b[0], ssd_dt_bias[0], ssd_a_log[0], ssd_d[0], ssd_norm[0]),
        (rwkv_mu[0], rwkv_w0[0], rwkv_w2[0], rwkv_a0[0], rwkv_a2[0], rwkv_g2[0], rwkv_k_k[0], rwkv_k_a[0],
         rwkv_r_k[0].reshape(-1), rwkv_ln_w[0], rwkv_ln_b[0]))
    w_out = e_w_out[0].astype(BF16)
    wr, br = router_w(0)
    h, hf, route = _outproj_router([y_ssd, y_rwkv], [w_out[:SSD_INNER], w_out[SSD_INNER:]], h, ln_ffn[0],
                                   wr, br, tm_mm)
    ys, dest3 = moe(0, hf, route)

    h, proj = _combine_norm_proj(h, ys, route, dest3, ln_mix[1], o_w_in[0].astype(BF16), tm)
    att = _swa(proj, batch, seq, attn_q_norm[0], attn_k_norm[0], attn_sinks[0])
    wr, br = router_w(1)
    h, hf, route = _outproj_router([att], [o_w_out[0].astype(BF16)], h, ln_ffn[1], wr, br, tm_mm)
    ys, dest3 = moe(1, hf, route)
    out = _combine(h, ys, route, dest3, tm)
    return out.reshape(batch, seq, d)
```

```python
import functools
import math

import jax
import jax.numpy as jnp
from jax import lax
from jax.experimental import pallas as pl
from jax.experimental.pallas import tpu as pltpu

F32 = jnp.float32
BF16 = jnp.bfloat16

D_MODEL = 1024
SSD_HEADS = 16
SSD_HEAD_DIM = 64
SSD_INNER = 1024
SSD_GROUPS = 4
SSD_STATE = 128
SSD_CONV = 4
SSD_CONV_DIM = 2048
SSD_COLS = 3088
RWKV_HEADS = 16
RWKV_HEAD_DIM = 64
RWKV_DIM = 1024
DECAY_LORA = 64
AAA_LORA = 64
GATE_LORA = 128
RWKV_GN_EPS = 64e-5
ATT_HEADS = 16
KV_HEADS = 4
Q_PER_KV = 4
HEAD_DIM = 64
Q_DIM = 1024
KV_DIM = 256
WINDOW = 128
EXPERT_GROUPS = 4
EXPERTS_PER_GROUP = 8
N_EXPERTS = 32
TOP_K = 2
EXPERT_HIDDEN = 512
MOE_BLOCK = 256
NORM_EPS = 1e-6

LANES = 128
CHUNK = 128
TAIL = 8
DMA_UNROLL = 8
SWA_QBLOCKS = 8
EVEN_COLS_PAD = 6528
VMEM_LIMIT = 56 * 1024 * 1024
NEG_BIG = -1e30
assert WINDOW == CHUNK


def _cparams(sem):
    return pltpu.CompilerParams(dimension_semantics=sem, vmem_limit_bytes=VMEM_LIMIT)


def _dot(a, b):
    return jnp.dot(a.astype(BF16), b.astype(BF16), preferred_element_type=F32)


def _dot_nt(a, b):
    return lax.dot_general(a.astype(BF16), b.astype(BF16), (((1,), (1,)), ((), ())),
                           preferred_element_type=F32)


def _dot_tn(a, b):
    return lax.dot_general(a.astype(BF16), b.astype(BF16), (((0,), (0,)), ((), ())),
                           preferred_element_type=F32)


def _split3(x):
    x1 = x.astype(BF16)
    r1 = x - x1.astype(F32)
    x2 = r1.astype(BF16)
    x3 = (r1 - x2.astype(F32)).astype(BF16)
    return x1, x2, x3


def _dot_exact_lhs(m_bf16, x):
    x1, x2, x3 = _split3(x)
    f = lambda p: jnp.dot(m_bf16, p, preferred_element_type=F32)
    return f(x1) + f(x2) + f(x3)


def _dot_exact_rhs(x, m_bf16):
    x1, x2, x3 = _split3(x)
    f = lambda p: jnp.dot(p, m_bf16, preferred_element_type=F32)
    return f(x1) + f(x2) + f(x3)


def _dot_f32(x, w):
    x1, x2, _ = _split3(x)
    w1, w2, _ = _split3(w)
    f = lambda p, q: jnp.dot(p, q, preferred_element_type=F32)
    return f(x1, w1) + (f(x1, w2) + f(x2, w1))


def _sigmoid(x):
    return 1.0 / (1.0 + jnp.exp(-x))


def _silu(x):
    return x * _sigmoid(x)


def _softplus(x):
    return jnp.maximum(x, 0.0) + jnp.log(1.0 + jnp.exp(-jnp.abs(x)))


def _rms(x, gain):
    return x * lax.rsqrt(jnp.mean(x * x, axis=-1, keepdims=True) + NORM_EPS) * gain


def _iota2(shape, dim):
    return lax.broadcasted_iota(jnp.int32, shape, dim)


def _tri_incl_bf16(n):
    return jnp.where(_iota2((n, n), 0) >= _iota2((n, n), 1), 1.0, 0.0).astype(BF16)


def _norm_proj_kernel(h_ref, g_ref, w_ref, o_ref):
    o_ref[...] = _dot(_rms(h_ref[...], g_ref[...]), w_ref[...])


def _norm_proj(h, gain, w_bf16, tm, tn):
    n, d = h.shape
    c = w_bf16.shape[1]
    return pl.pallas_call(
        _norm_proj_kernel,
        grid=(c // tn, n // tm),
        in_specs=[pl.BlockSpec((tm, d), lambda j, i: (i, 0)),
                  pl.BlockSpec((1, d), lambda j, i: (0, 0)),
                  pl.BlockSpec((d, tn), lambda j, i: (0, j))],
        out_specs=pl.BlockSpec((tm, tn), lambda j, i: (i, j)),
        out_shape=jax.ShapeDtypeStruct((n, c), F32),
        compiler_params=_cparams(("arbitrary", "arbitrary")),
        name="norm_proj",
    )(h, gain.reshape(1, d), w_bf16)


def _gathered_combine(h_ref, route_ref, dest_ref, destn_ref, ys_hbm, yg, gsem):
    i = pl.program_id(0)
    nt = pl.num_programs(0)
    slot = lax.rem(i, 2)
    tm = h_ref.shape[0]

    def row_copy(idx_ref, buf_slot, t, c):
        return pltpu.make_async_copy(ys_hbm.at[pl.ds(idx_ref[0, 0, TOP_K * t + c], 1), :],
                                     yg.at[buf_slot, c, pl.ds(t, 1), :], gsem.at[buf_slot])

    def wait_tile(buf_slot):
        for c in range(TOP_K):
            pltpu.make_async_copy(ys_hbm.at[pl.ds(0, tm), :], yg.at[buf_slot, c], gsem.at[buf_slot]).wait()

    @pl.when(i == 0)
    def _():
        def body(g, carry):
            for u in range(DMA_UNROLL):
                for c in range(TOP_K):
                    row_copy(dest_ref, 0, g * DMA_UNROLL + u, c).start()
            return carry
        lax.fori_loop(0, tm // DMA_UNROLL, body, 0)

    wait_tile(slot)
    for t in range(tm):
        for c in range(TOP_K):
            row_copy(destn_ref, 1 - slot, t, c).start()
    r = route_ref[...]
    x = h_ref[...] + r[:, 2:3] * yg[slot, 0] + r[:, 3:4] * yg[slot, 1]

    def finalize():
        @pl.when(i == nt - 1)
        def _():
            wait_tile(1 - slot)
    return x, finalize


def _combine_norm_proj_kernel(h_ref, route_ref, dest_ref, destn_ref, g_ref, w_ref, ys_hbm, hnew_ref, o_ref,
                              yg, gsem):
    x, finalize = _gathered_combine(h_ref, route_ref, dest_ref, destn_ref, ys_hbm, yg, gsem)
    hnew_ref[...] = x
    o_ref[...] = _dot(_rms(x, g_ref[...]), w_ref[...])
    finalize()


def _combine_kernel(h_ref, route_ref, dest_ref, destn_ref, ys_hbm, o_ref, yg, gsem):
    x, finalize = _gathered_combine(h_ref, route_ref, dest_ref, destn_ref, ys_hbm, yg, gsem)
    o_ref[...] = x
    finalize()


def _combine_specs(n, d, tm):
    nt = n // tm
    smem = lambda fn: pl.BlockSpec((1, 1, TOP_K * tm), fn, memory_space=pltpu.SMEM)
    in_specs = [pl.BlockSpec((tm, d), lambda i: (i, 0)),
                pl.BlockSpec((tm, LANES), lambda i: (i, 0)),
                smem(lambda i: (i, 0, 0)),
                smem(lambda i: (jnp.minimum(i + 1, nt - 1), 0, 0))]
    scratch = [pltpu.VMEM((2, TOP_K, tm, d), F32), pltpu.SemaphoreType.DMA((2,))]
    return in_specs, scratch


def _combine_norm_proj(h, ys, route, dest3, gain, w_bf16, tm):
    n, d = h.shape
    c = w_bf16.shape[1]
    in_specs, scratch = _combine_specs(n, d, tm)
    return pl.pallas_call(
        _combine_norm_proj_kernel,
        grid=(n // tm,),
        in_specs=in_specs + [pl.BlockSpec((1, d), lambda i: (0, 0)),
                             pl.BlockSpec((d, c), lambda i: (0, 0)),
                             pl.BlockSpec(memory_space=pl.ANY)],
        out_specs=[pl.BlockSpec((tm, d), lambda i: (i, 0)),
                   pl.BlockSpec((tm, c), lambda i: (i, 0))],
        out_shape=[jax.ShapeDtypeStruct((n, d), F32), jax.ShapeDtypeStruct((n, c), F32)],
        scratch_shapes=scratch,
        compiler_params=_cparams(("arbitrary",)),
        name="combine_norm_proj",
    )(h, route, dest3, dest3, gain.reshape(1, d), w_bf16, ys)


def _combine(h, ys, route, dest3, tm):
    n, d = h.shape
    in_specs, scratch = _combine_specs(n, d, tm)
    return pl.pallas_call(
        _combine_kernel,
        grid=(n // tm,),
        in_specs=in_specs + [pl.BlockSpec(memory_space=pl.ANY)],
        out_specs=pl.BlockSpec((tm, d), lambda i: (i, 0)),
        out_shape=jax.ShapeDtypeStruct((n, d), F32),
        scratch_shapes=scratch,
        compiler_params=_cparams(("arbitrary",)),
        name="combine",
    )(h, route, dest3, dest3, ys)


def _shifted_taps(buf, u_ref, n_taps):
    buf[TAIL:TAIL + CHUNK, :] = u_ref[...]
    taps = [buf[TAIL - j:TAIL - j + CHUNK, :] for j in range(n_taps)]
    return taps


def _carry_tail(buf):
    buf[0:TAIL, :] = buf[CHUNK:CHUNK + TAIL, :]


def _ssd_stages(z_ref, x_ref, bc_ref, dt_ref, cwx_ref, cbx_ref, cwbc_ref, cbbc_ref, dtb_ref, alog_ref,
                dskip_ref, nw_ref, hexp_ref, o_ref, xbuf, bcbuf, state):
    st = {}
    gw = SSD_INNER // SSD_GROUPS

    def init():
        @pl.when(pl.program_id(1) == 0)
        def _():
            xbuf[0:TAIL, :] = jnp.zeros((TAIL, SSD_INNER), F32)
            bcbuf[0:TAIL, :] = jnp.zeros((TAIL, SSD_INNER), F32)
            state[...] = jnp.zeros_like(state)

    def conv(buf, u_ref, w_ref, b_ref):
        taps = _shifted_taps(buf, u_ref, SSD_CONV)
        acc = b_ref[...] + taps[0] * w_ref[3:4, :]
        for j in range(1, SSD_CONV):
            acc = acc + taps[j] * w_ref[3 - j:4 - j, :]
        _carry_tail(buf)
        return _silu(acc)

    def convs():
        st["xs"] = conv(xbuf, x_ref, cwx_ref, cbx_ref)
        st["bc"] = conv(bcbuf, bc_ref, cwbc_ref, cbbc_ref)

    def decays():
        lane = _iota2((CHUNK, LANES), 1)
        dt = _softplus(dt_ref[...] + dtb_ref[...])
        adt = jnp.where(lane < SSD_HEADS, -jnp.exp(alog_ref[...]) * dt, 0.0)
        cum = _dot_exact_lhs(_tri_incl_bf16(CHUNK), adt)
        hexp = hexp_ref[...]
        cum_full = _dot_exact_rhs(cum, hexp)
        tot_full = cum_full[CHUNK - 1:CHUNK, :]
        xd = st["xs"] * _dot_exact_rhs(dt, hexp)
        st.update(cum=cum, cum_t=cum.T, tot_full=tot_full, xd=xd, xds=xd * jnp.exp(tot_full - cum_full),
                  eac=jnp.exp(cum_full), lane_lo=lane < SSD_HEAD_DIM,
                  causal=_iota2((CHUNK, CHUNK), 0) >= _iota2((CHUNK, CHUNK), 1), y_parts=[])

    def group(g):
        bc, cum, cum_t, xd = st["bc"], st["cum"], st["cum_t"], st["xd"]
        bg = bc[:, g * SSD_STATE:(g + 1) * SSD_STATE]
        cg = bc[:, (SSD_GROUPS + g) * SSD_STATE:(SSD_GROUPS + g + 1) * SSD_STATE]
        cb = _dot_nt(cg, bg)
        s_prev = state[:, g * gw:(g + 1) * gw]
        y_off = _dot(cg, s_prev) * st["eac"][:, g * gw:(g + 1) * gw]
        s_new = _dot(bg.T, st["xds"][:, g * gw:(g + 1) * gw])
        state[:, g * gw:(g + 1) * gw] = jnp.exp(st["tot_full"][:, g * gw:(g + 1) * gw]) * s_prev + s_new
        for pr in range(2):
            lo = g * gw + pr * LANES
            xd_pair = xd[:, lo:lo + LANES]
            yd = jnp.zeros((CHUNK, LANES), F32)
            for k in range(2):
                h = (lo // SSD_HEAD_DIM) + k
                diff = cum[:, h:h + 1] - cum_t[h:h + 1, :]
                decay = jnp.exp(jnp.where(st["causal"], diff, NEG_BIG))
                keep = st["lane_lo"] if k == 0 else jnp.logical_not(st["lane_lo"])
                yd = yd + _dot(cb * decay, jnp.where(keep, xd_pair, 0.0))
            st["y_parts"].append(yd + y_off[:, pr * LANES:(pr + 1) * LANES])

    def finish():
        y = jnp.concatenate(st["y_parts"], axis=1) + dskip_ref[...] * st["xs"]
        y = y * _silu(z_ref[...])
        outs = []
        for g in range(SSD_GROUPS):
            yg = y[:, g * gw:(g + 1) * gw]
            outs.append(yg * lax.rsqrt(jnp.mean(yg * yg, axis=-1, keepdims=True) + 1e-5))
        o_ref[...] = (jnp.concatenate(outs, axis=1) * nw_ref[...]).astype(o_ref.dtype)

    return [init, convs, decays] + [functools.partial(group, g) for g in range(SSD_GROUPS)] + [finish]


def _head_expand(n_heads, head_dim):
    h = jnp.arange(LANES)[:, None]
    l = jnp.arange(n_heads * head_dim)[None, :]
    return (l // head_dim == h).astype(BF16)


def _ssd_operands(proj, nc, conv_w, conv_b, dt_bias, a_log, d_skip, norm_w):
    pad16 = lambda v: jnp.pad(v.astype(F32), (0, LANES - v.shape[0])).reshape(1, LANES)
    row_spec = lambda w, blk: pl.BlockSpec((CHUNK, w), lambda b, c: (b * nc + c, blk))
    full = lambda shape: pl.BlockSpec(shape, lambda b, c: (0, 0))
    in_specs = [row_spec(SSD_INNER, 0),
                row_spec(SSD_INNER, 1),
                row_spec(SSD_INNER, 2),
                row_spec(LANES, 50),
                full((SSD_CONV, SSD_INNER)), full((1, SSD_INNER)),
                full((SSD_CONV, SSD_INNER)), full((1, SSD_INNER)),
                full((1, LANES)), full((1, LANES)),
                full((1, SSD_INNER)), full((1, SSD_INNER)),
                full((LANES, SSD_INNER))]
    operands = [proj, proj, proj, proj,
                conv_w[:, :SSD_INNER], conv_b[:SSD_INNER].reshape(1, -1),
                conv_w[:, SSD_INNER:], conv_b[SSD_INNER:].reshape(1, -1),
                pad16(dt_bias), pad16(a_log),
                jnp.repeat(d_skip.astype(F32), SSD_HEAD_DIM).reshape(1, -1), norm_w.reshape(1, -1),
                _head_expand(SSD_HEADS, SSD_HEAD_DIM)]
    scratch = [pltpu.VMEM((TAIL + CHUNK, SSD_INNER), F32),
               pltpu.VMEM((TAIL + CHUNK, SSD_INNER), F32),
               pltpu.VMEM((SSD_STATE, SSD_INNER), F32)]
    return in_specs, operands, scratch


def _rwkv_kernel(r_ref, k_ref, v_ref, lo_ref, mur_ref, muk_ref, muv_ref, mulo_ref, w0_ref, w2_ref, a0_ref,
                 a2_ref, g2_ref, kk_ref, ka_ref, rk_ref, lnw_ref, lnb_ref, o_ref,
                 rbuf, kbuf, vbuf, lobuf, state, side_work=()):
    side = iter(side_work)
    run_side = lambda: next(side, lambda: None)()
    run_side()
    c = pl.program_id(1)

    @pl.when(c == 0)
    def _():
        rbuf[0:TAIL, :] = jnp.zeros((TAIL, RWKV_DIM), F32)
        kbuf[0:TAIL, :] = jnp.zeros((TAIL, RWKV_DIM), F32)
        vbuf[0:TAIL, :] = jnp.zeros((TAIL, RWKV_DIM), F32)
        lobuf[0:TAIL, :] = jnp.zeros((TAIL, 2 * LANES), F32)
        state[...] = jnp.zeros_like(state)

    def shift(buf, u_ref, mu_ref):
        cur, prev = _shifted_taps(buf, u_ref, 2)
        _carry_tail(buf)
        return cur + mu_ref[...] * (prev - cur)

    r = shift(rbuf, r_ref, mur_ref)
    k = shift(kbuf, k_ref, muk_ref)
    v = shift(vbuf, v_ref, muv_ref)
    lo = shift(lobuf, lo_ref, mulo_ref)
    wa = lo[:, :LANES]
    logw = -math.exp(-0.5) * _sigmoid(w0_ref[...] + _dot(jnp.tanh(wa), w2_ref[...]))
    a = _sigmoid(a0_ref[...] + _dot(wa, a2_ref[...]))
    g = _dot(_sigmoid(lo[:, LANES:]), g2_ref[...])

    lane = _iota2((CHUNK, LANES), 1)
    head0 = lane < RWKV_HEAD_DIM
    bd = (_iota2((LANES, LANES), 0) // RWKV_HEAD_DIM) == (_iota2((LANES, LANES), 1) // RWKV_HEAD_DIM)
    bd_ones = jnp.where(bd, 1.0, 0.0).astype(BF16)

    def head_sum(x):
        return jnp.dot(x.astype(BF16), bd_ones, preferred_element_type=F32)

    tri = _tri_incl_bf16(CHUNK)
    cum = _dot_exact_lhs(tri, logw)
    cume = cum - logw
    cmid = cum[CHUNK // 2 - 1:CHUNK // 2, :]
    cend = cum[CHUNK - 1:CHUNK, :]
    e_in_mid = jnp.exp(cum - cmid)
    e_ex_mid = jnp.exp(cume - cmid)
    e_mid_in = jnp.exp(cmid - cum)
    e_ex = jnp.exp(cume)
    e_in = jnp.exp(cum)
    e_end = jnp.exp(cend - cum)
    e_tot = jnp.exp(cend)

    row = _iota2((CHUNK, CHUNK), 0)
    col = _iota2((CHUNK, CHUNK), 1)
    strict = row > col
    incl = row >= col
    zeros = jnp.zeros((CHUNK, LANES), F32)

    n_blocks = RWKV_DIM // LANES
    blk = []
    for p in range(n_blocks):
        sl = slice(p * LANES, (p + 1) * LANES)
        rp, vp, ap = r[:, sl], v[:, sl], a[:, sl]
        kkp = k[:, sl] * kk_ref[:, sl]
        kkp = kkp * lax.rsqrt(jnp.maximum(head_sum(kkp * kkp), 1e-24))
        kp = k[:, sl] * (1.0 + (ap - 1.0) * ka_ref[:, sl])
        aap = -kkp
        bp = kkp * ap
        a_mid = aap * e_ex_mid[:, sl]
        r_mid = rp * e_in_mid[:, sl]
        bf = lambda x: x.astype(BF16)
        lhs = jnp.concatenate([bf(jnp.where(head0, a_mid, 0.0)), bf(jnp.where(head0, 0.0, a_mid)),
                               bf(jnp.where(head0, r_mid, 0.0)), bf(jnp.where(head0, 0.0, r_mid))], axis=0)
        rhs = jnp.concatenate([bf(bp * e_mid_in[:, sl]), bf(kp * e_mid_in[:, sl])], axis=0)
        a_abs = aap * e_ex[:, sl]
        blk.append(dict(sl=sl, rp=rp, vp=vp, kp=kp, r_abs=rp * e_in[:, sl],
                        a_abs=(bf(jnp.where(head0, a_abs, 0.0)), bf(jnp.where(head0, 0.0, a_abs))),
                        vm=(bf(jnp.where(head0, vp, 0.0)), bf(jnp.where(head0, 0.0, vp))),
                        b_end=bf(bp * e_end[:, sl]), k_end=bf(kp * e_end[:, sl]), prod=_dot_nt(lhs, rhs)))
    run_side()

    heads = []
    for p in range(n_blocks):
        prod = blk[p]["prod"]
        for hh in range(2):
            heads.append(dict(
                vm=blk[p]["vm"][hh], a_abs=blk[p]["a_abs"][hh],
                a_ab=jnp.where(strict, prod[hh * CHUNK:(hh + 1) * CHUNK, :CHUNK], 0.0),
                a_ak=jnp.where(strict, prod[hh * CHUNK:(hh + 1) * CHUNK, CHUNK:], 0.0).astype(BF16),
                m_rb=jnp.where(incl, prod[(2 + hh) * CHUNK:(3 + hh) * CHUNK, :CHUNK], 0.0).astype(BF16),
                m_rk=jnp.where(incl, prod[(2 + hh) * CHUNK:(3 + hh) * CHUNK, CHUNK:], 0.0).astype(BF16)))

    eye = jnp.where(row == col, 1.0, 0.0)
    n_levels = int(math.log2(CHUNK)) - 1
    ts = [eye + hd["a_ab"] for hd in heads]
    xs = [hd["a_ab"].astype(BF16) for hd in heads]
    avs = [_dot(hd["a_ak"], hd["vm"]).astype(BF16) for hd in heads]
    run_side()
    xs = [jnp.dot(x, x, preferred_element_type=F32).astype(BF16) for x in xs]
    for _ in range(n_levels - 1):
        run_side()
        zs = [jnp.dot(x, jnp.concatenate([x, t.astype(BF16)], axis=1), preferred_element_type=F32)
              for x, t in zip(xs, ts)]
        xs = [z[:, :CHUNK].astype(BF16) for z in zs]
        ts = [t + z[:, CHUNK:] for t, z in zip(ts, zs)]
    ts = [t + jnp.dot(x, t.astype(BF16), preferred_element_type=F32) for x, t in zip(xs, ts)]
    for _ in side:
        _()

    wmats = [_dot(t, jnp.concatenate([hd["a_abs"], av], axis=1)).astype(BF16)
             for t, hd, av in zip(ts, heads, avs)]
    zeros_bf = zeros.astype(BF16)
    outs = [_dot(jnp.concatenate([hd["m_rb"], hd["m_rk"]], axis=1),
                 jnp.concatenate([jnp.concatenate([wm[:, LANES:], wm[:, :LANES]], axis=1),
                                  jnp.concatenate([hd["vm"], zeros_bf], axis=1)], axis=0))
            for wm, hd in zip(wmats, heads)]

    zts = []
    for p in range(n_blocks):
        w0h, w1h = wmats[2 * p], wmats[2 * p + 1]
        ui = w0h[:, LANES:] + w1h[:, LANES:]
        a_eff = w0h[:, :LANES] + w1h[:, :LANES]
        zts.append(_dot_tn(jnp.concatenate([jnp.concatenate([ui, a_eff], axis=1),
                                            jnp.concatenate([blk[p]["vp"].astype(BF16), zeros_bf], axis=1)],
                                           axis=0),
                           jnp.concatenate([blk[p]["b_end"], blk[p]["k_end"]], axis=0)))

    ys = []
    for p in range(n_blocks):
        s0 = state[p]
        o0, o1 = outs[2 * p], outs[2 * p + 1]
        r_eff = blk[p]["r_abs"] + o0[:, LANES:] + o1[:, LANES:]
        ys.append(_dot_nt(r_eff, s0) + o0[:, :LANES] + o1[:, :LANES])
        h_intra = jnp.where(bd, zts[p][:LANES, :], 0.0)
        g_corr = jnp.where(bd, zts[p][LANES:, :], 0.0)
        state[p] = s0 * e_tot[:, blk[p]["sl"]] + _dot(s0, g_corr) + h_intra

    means = [head_sum(y) * (1.0 / RWKV_HEAD_DIM) for y in ys]
    devs = [y - m for y, m in zip(ys, means)]
    vars_ = [head_sum(dv * dv) * (1.0 / RWKV_HEAD_DIM) for dv in devs]
    for p in range(n_blocks):
        sl = blk[p]["sl"]
        yn = devs[p] * lax.rsqrt(vars_[p] + RWKV_GN_EPS) * lnw_ref[:, sl] + lnb_ref[:, sl]
        bonus = head_sum(blk[p]["rp"] * blk[p]["kp"] * rk_ref[:, sl])
        o_ref[:, sl] = ((yn + bonus * blk[p]["vp"]) * g[:, sl]).astype(o_ref.dtype)


def _rwkv_operands(proj, nc, mu, w0, w2, a0, a2, g2, k_k, k_a, r_k, ln_w, ln_b):
    d = RWKV_DIM
    row_spec = lambda w, blk: pl.BlockSpec((CHUNK, w), lambda b, c: (b * nc + c, blk))
    full = lambda shape: pl.BlockSpec(shape, lambda b, c: (0,) * len(shape))
    vec = lambda x: x.astype(F32).reshape(1, -1)
    w2p = jnp.concatenate([w2, jnp.zeros((AAA_LORA, d), w2.dtype)], axis=0).astype(BF16)
    a2p = jnp.concatenate([jnp.zeros((DECAY_LORA, d), a2.dtype), a2], axis=0).astype(BF16)
    in_specs = [row_spec(d, 3), row_spec(d, 4), row_spec(d, 5), row_spec(2 * LANES, 24),
                full((1, d)), full((1, d)), full((1, d)), full((1, 2 * LANES)),
                full((1, d)), full((LANES, d)), full((1, d)), full((LANES, d)), full((LANES, d)),
                full((1, d)), full((1, d)), full((1, d)), full((1, d)), full((1, d))]
    operands = [proj, proj, proj, proj,
                vec(mu[:d]), vec(mu[d:2 * d]), vec(mu[2 * d:3 * d]), vec(mu[3 * d:]),
                vec(w0), w2p, vec(a0), a2p, g2.astype(BF16),
                vec(k_k), vec(k_a), vec(r_k), vec(ln_w), vec(ln_b)]
    scratch = [pltpu.VMEM((TAIL + CHUNK, d), F32), pltpu.VMEM((TAIL + CHUNK, d), F32),
               pltpu.VMEM((TAIL + CHUNK, d), F32), pltpu.VMEM((TAIL + CHUNK, 2 * LANES), F32),
               pltpu.VMEM((d // LANES, LANES, LANES), F32)]
    return in_specs, operands, scratch


def _mixers_kernel(*refs, n_ssd_in, n_rwkv_in, n_ssd_scratch):
    ssd_in = refs[:n_ssd_in]
    rwkv_in = refs[n_ssd_in:n_ssd_in + n_rwkv_in]
    o_ssd, o_rwkv = refs[n_ssd_in + n_rwkv_in:n_ssd_in + n_rwkv_in + 2]
    scratch = refs[n_ssd_in + n_rwkv_in + 2:]
    _rwkv_kernel(*rwkv_in, o_rwkv, *scratch[n_ssd_scratch:],
                 side_work=_ssd_stages(*ssd_in, o_ssd, *scratch[:n_ssd_scratch]))


def _mixers(proj, batch, seq, ssd_params, rwkv_params):
    nc = seq // CHUNK
    n = batch * seq
    s_specs, s_ops, s_scratch = _ssd_operands(proj, nc, *ssd_params)
    r_specs, r_ops, r_scratch = _rwkv_operands(proj, nc, *rwkv_params)
    out_spec = lambda w: pl.BlockSpec((CHUNK, w), lambda b, c: (b * nc + c, 0))
    return pl.pallas_call(
        functools.partial(_mixers_kernel, n_ssd_in=len(s_specs), n_rwkv_in=len(r_specs),
                          n_ssd_scratch=len(s_scratch)),
        grid=(batch, nc),
        in_specs=s_specs + r_specs,
        out_specs=[out_spec(SSD_INNER), out_spec(RWKV_DIM)],
        out_shape=[jax.ShapeDtypeStruct((n, SSD_INNER), BF16), jax.ShapeDtypeStruct((n, RWKV_DIM), BF16)],
        scratch_shapes=s_scratch + r_scratch,
        compiler_params=_cparams(("arbitrary", "arbitrary")),
        name="mixers",
    )(*s_ops, *r_ops)


def _route(logits):
    lane = _iota2(logits.shape, 1)
    lanef = lane.astype(F32)
    big = float(LANES)

    def first_max(x):
        m = jnp.max(x, axis=-1, keepdims=True)
        idx = jnp.min(jnp.where(x == m, lanef, big), axis=-1, keepdims=True)
        return m, idx

    cl = jnp.where(lane < EXPERT_GROUPS, logits, NEG_BIG)
    cmax, grp = first_max(cl)
    p_group = 1.0 / jnp.sum(jnp.exp(cl - cmax), axis=-1, keepdims=True)
    lo = EXPERT_GROUPS + grp * EXPERTS_PER_GROUP
    fl = jnp.where((lanef >= lo) & (lanef < lo + EXPERTS_PER_GROUP), logits, NEG_BIG)
    m0, i0 = first_max(fl)
    m1, i1 = first_max(jnp.where(lanef == i0, NEG_BIG, fl))
    e1 = jnp.exp(m1 - m0)
    g0 = p_group / (1.0 + e1)
    g1 = p_group * e1 / (1.0 + e1)
    return jnp.where(lane == 0, i0 - EXPERT_GROUPS,
                     jnp.where(lane == 1, i1 - EXPERT_GROUPS,
                               jnp.where(lane == 2, g0, jnp.where(lane == 3, g1, 0.0))))


def _outproj_router_kernel(*refs, n_in):
    ys = refs[:n_in]
    ws = refs[n_in:2 * n_in]
    h_ref, g_ref, wr_ref, br_ref, hnew_ref, hf_ref, route_ref = refs[2 * n_in:]
    acc = h_ref[...]
    for y_ref, w_ref in zip(ys, ws):
        acc = acc + jnp.dot(y_ref[...], w_ref[...], preferred_element_type=F32)
    hnew_ref[...] = acc
    hf = _rms(acc, g_ref[...])
    hf_ref[...] = hf
    route_ref[...] = _route(_dot_f32(hf, wr_ref[...]) + br_ref[...])


def _outproj_router(ys, ws, h, gain, w_router, b_router, tm):
    n, d = h.shape
    n_in = len(ys)
    in_specs = ([pl.BlockSpec((tm, y.shape[1]), lambda i: (i, 0)) for y in ys]
                + [pl.BlockSpec(w.shape, lambda i: (0, 0)) for w in ws]
                + [pl.BlockSpec((tm, d), lambda i: (i, 0)),
                   pl.BlockSpec((1, d), lambda i: (0, 0)),
                   pl.BlockSpec((d, LANES), lambda i: (0, 0)),
                   pl.BlockSpec((1, LANES), lambda i: (0, 0))])
    return pl.pallas_call(
        functools.partial(_outproj_router_kernel, n_in=n_in),
        grid=(n // tm,),
        in_specs=in_specs,
        out_specs=[pl.BlockSpec((tm, d), lambda i: (i, 0)),
                   pl.BlockSpec((tm, d), lambda i: (i, 0)),
                   pl.BlockSpec((tm, LANES), lambda i: (i, 0))],
        out_shape=[jax.ShapeDtypeStruct((n, d), F32), jax.ShapeDtypeStruct((n, d), F32),
                   jax.ShapeDtypeStruct((n, LANES), F32)],
        compiler_params=_cparams(("arbitrary",)),
        name="outproj_router",
    )(*ys, *ws, h, gain.reshape(1, d), w_router, b_router)


def _router_weights(w_coarse, b_coarse, w_fine, b_fine):
    d = w_coarse.shape[0]
    wf = jnp.transpose(w_fine, (1, 0, 2)).reshape(d, N_EXPERTS)
    w = jnp.concatenate([w_coarse, wf], axis=1).astype(F32)
    b = jnp.concatenate([b_coarse, b_fine.reshape(N_EXPERTS)]).astype(F32)
    pad = LANES - w.shape[1]
    return jnp.pad(w, ((0, 0), (0, pad))), jnp.pad(b, (0, pad)).reshape(1, LANES)


def _moe_plan(route, tm):
    n = route.shape[0]
    a = n * TOP_K
    n_blocks = a // MOE_BLOCK + N_EXPERTS
    e_flat = route[:, :TOP_K].astype(jnp.int32).reshape(a)
    seg = MOE_BLOCK
    onehot = (e_flat[:, None] == jnp.arange(N_EXPERTS, dtype=jnp.int32)[None, :]).astype(F32)
    onehot = onehot.reshape(a // seg, seg, N_EXPERTS)
    tri = jnp.tril(jnp.ones((seg, seg), F32))
    within = jnp.einsum("ij,bjk->bik", tri, onehot)
    tot = within[:, -1, :]
    offs = jnp.cumsum(tot, axis=0) - tot
    rank = (jnp.sum(onehot * (within + offs[:, None, :]), axis=-1) - 1.0).astype(jnp.int32).reshape(a)
    counts = (offs[-1] + tot[-1]).astype(jnp.int32)
    padded = (counts + MOE_BLOCK - 1) // MOE_BLOCK * MOE_BLOCK
    pad_end = jnp.cumsum(padded)
    pad_start = pad_end - padded
    start_of = jnp.sum(jnp.where(e_flat[:, None] == jnp.arange(N_EXPERTS, dtype=jnp.int32)[None, :],
                                 pad_start[None, :], 0), axis=1)
    dest = (start_of + rank).astype(jnp.int32)
    block_start = jnp.arange(n_blocks, dtype=jnp.int32) * MOE_BLOCK
    block_expert = jnp.minimum(jnp.sum((pad_end[None, :] <= block_start[:, None]).astype(jnp.int32), axis=1),
                               N_EXPERTS - 1).astype(jnp.int32)
    n_used = (pad_end[-1] // MOE_BLOCK).astype(jnp.int32).reshape(1)
    pad_lo = (pad_start + counts).astype(jnp.int32)
    return dict(dest3=dest.reshape(n // tm, 1, TOP_K * tm), block_expert=block_expert, n_used=n_used,
                pad_lo=pad_lo, pad_hi=pad_end.astype(jnp.int32), n_blocks=n_blocks)


def _dispatch_kernel(padlo_ref, padhi_ref, nused_ref, hf_ref, dest_ref, xs_hbm, zblk, sem, zsem):
    i = pl.program_id(0)
    tm = hf_ref.shape[0]

    @pl.when(i == 0)
    def _():
        zblk[...] = jnp.zeros_like(zblk)

        def per_expert(e, carry):
            def zero_row(rw, c2):
                pltpu.make_async_copy(zblk.at[pl.ds(0, 1), :], xs_hbm.at[pl.ds(rw, 1), :], zsem).start()
                return c2
            lax.fori_loop(padlo_ref[e], padhi_ref[e], zero_row, 0)

            def wait_row(rw, c2):
                pltpu.make_async_copy(zblk.at[pl.ds(0, 1), :], xs_hbm.at[pl.ds(rw, 1), :], zsem).wait()
                return c2
            lax.fori_loop(padlo_ref[e], padhi_ref[e], wait_row, 0)
            return carry
        lax.fori_loop(0, N_EXPERTS, per_expert, 0)

        n_blocks = xs_hbm.shape[0] // MOE_BLOCK

        def block_copy(b):
            return pltpu.make_async_copy(
                zblk, xs_hbm.at[pl.ds(pl.multiple_of(b * MOE_BLOCK, MOE_BLOCK), MOE_BLOCK), :], zsem)

        def zero_block(b, carry):
            block_copy(b).start()
            return carry
        lax.fori_loop(nused_ref[0], n_blocks, zero_block, 0)

        def wait_block(b, carry):
            block_copy(b).wait()
            return carry
        lax.fori_loop(nused_ref[0], n_blocks, wait_block, 0)

    for t in range(tm):
        for c in range(TOP_K):
            pltpu.make_async_copy(hf_ref.at[pl.ds(t, 1), :],
                                  xs_hbm.at[pl.ds(dest_ref[0, 0, TOP_K * t + c], 1), :], sem).start()
    for c in range(TOP_K):
        pltpu.make_async_copy(hf_ref, xs_hbm.at[pl.ds(0, tm), :], sem).wait()


def _dispatch(hf, plan, tm):
    n, d = hf.shape
    slots = plan["n_blocks"] * MOE_BLOCK
    grid_spec = pltpu.PrefetchScalarGridSpec(
        num_scalar_prefetch=3,
        grid=(n // tm,),
        in_specs=[pl.BlockSpec((tm, d), lambda i, lo, hi, nu: (i, 0)),
                  pl.BlockSpec((1, 1, TOP_K * tm), lambda i, lo, hi, nu: (i, 0, 0), memory_space=pltpu.SMEM)],
        out_specs=pl.BlockSpec(memory_space=pl.ANY),
        scratch_shapes=[pltpu.VMEM((MOE_BLOCK, d), F32), pltpu.SemaphoreType.DMA(()),
                        pltpu.SemaphoreType.DMA(())])
    return pl.pallas_call(
        _dispatch_kernel,
        grid_spec=grid_spec,
        out_shape=jax.ShapeDtypeStruct((slots, d), F32),
        compiler_params=_cparams(("arbitrary",)),
        name="moe_dispatch",
    )(plan["pad_lo"], plan["pad_hi"], plan["n_used"], hf, plan["dest3"])


def _experts_kernel(bexp_ref, nused_ref, x_ref, wg_ref, wu_ref, wd_ref, y_ref, wg_s, wu_s, wd_s):
    i = pl.program_id(0)
    active = i < nused_ref[0]

    @pl.when(active & ((i == 0) | (bexp_ref[i] != bexp_ref[jnp.maximum(i - 1, 0)])))
    def _():
        wg_s[...] = wg_ref[0].astype(BF16)
        wu_s[...] = wu_ref[0].astype(BF16)
        wd_s[...] = wd_ref[0].astype(BF16)

    @pl.when(active)
    def _():
        x = x_ref[...].astype(BF16)
        hg = jnp.dot(x, wg_s[...], preferred_element_type=F32)
        hu = jnp.dot(x, wu_s[...], preferred_element_type=F32)
        hb = (_silu(hg) * hu).astype(BF16)
        y_ref[...] = jnp.dot(hb, wd_s[...], preferred_element_type=F32)

    @pl.when(i >= nused_ref[0])
    def _():
        y_ref[...] = jnp.zeros_like(y_ref)


def _experts(xs, plan, layer, wg, wu, wd):
    slots, d = xs.shape
    n_blocks = plan["n_blocks"]
    used = lambda i, nu: jnp.minimum(i, nu[0] - 1)
    w_idx = lambda i, be, nu: (layer, be[used(i, nu)], 0, 0)
    grid_spec = pltpu.PrefetchScalarGridSpec(
        num_scalar_prefetch=2,
        grid=(n_blocks,),
        in_specs=[pl.BlockSpec((MOE_BLOCK, d), lambda i, be, nu: (used(i, nu), 0)),
                  pl.BlockSpec((None, 1, d, EXPERT_HIDDEN), w_idx),
                  pl.BlockSpec((None, 1, d, EXPERT_HIDDEN), w_idx),
                  pl.BlockSpec((None, 1, EXPERT_HIDDEN, d), w_idx)],
        out_specs=pl.BlockSpec((MOE_BLOCK, d), lambda i, be, nu: (i, 0)),
        scratch_shapes=[pltpu.VMEM((d, EXPERT_HIDDEN), BF16), pltpu.VMEM((d, EXPERT_HIDDEN), BF16),
                        pltpu.VMEM((EXPERT_HIDDEN, d), BF16)])
    return pl.pallas_call(
        _experts_kernel,
        grid_spec=grid_spec,
        out_shape=jax.ShapeDtypeStruct((slots, d), F32),
        compiler_params=_cparams(("arbitrary",)),
        name="moe_experts",
    )(plan["block_expert"], plan["n_used"], xs, wg, wu, wd)


def _swa_kernel(q_ref, kvp_ref, kvc_ref, qg_ref, kg_ref, slope_ref, sink_ref, o_ref):
    jb = pl.program_id(1)
    blk = CHUNK
    qi = _iota2((blk, blk), 0)
    kj = _iota2((blk, blk), 1)
    from_prev = kj > qi
    deltaf = jnp.where(from_prev, qi + blk - kj, qi - kj).astype(F32)
    no_prev = jnp.where(from_prev, jnp.where(jb > 0, 0.0, NEG_BIG), 0.0)
    scale = HEAD_DIM ** -0.5

    bd = (_iota2((LANES, LANES), 0) // HEAD_DIM) == (_iota2((LANES, LANES), 1) // HEAD_DIM)
    bd_ones = jnp.where(bd, 1.0, 0.0).astype(BF16)

    def head_rms(x, gain):
        sq = x * x
        s1 = sq.astype(BF16)
        s2 = (sq - s1.astype(F32)).astype(BF16)
        ms = (jnp.dot(s1, bd_ones, preferred_element_type=F32)
              + jnp.dot(s2, bd_ones, preferred_element_type=F32)) * (1.0 / HEAD_DIM)
        return x * lax.rsqrt(ms + NORM_EPS) * gain

    n_kv_blk = KV_DIM // LANES
    kv = jnp.concatenate([kvp_ref[...], kvc_ref[...]], axis=0)
    lane_kv = _iota2((kv.shape[0], LANES), 1)
    kv_half = (lane_kv < HEAD_DIM, lane_kv >= HEAD_DIM)
    lane_q = _iota2((blk, LANES), 1)
    q_half = (lane_q < HEAD_DIM, lane_q >= HEAD_DIM)
    kn = [head_rms(kv[:, j * LANES:(j + 1) * LANES], kg_ref[...]) for j in range(n_kv_blk)]
    vb = [kv[:, KV_DIM + j * LANES:KV_DIM + (j + 1) * LANES] for j in range(n_kv_blk)]
    kn_sw = [pltpu.roll(x, HEAD_DIM, 1) for x in kn]
    vb_sw = [pltpu.roll(x, HEAD_DIM, 1) for x in vb]
    v_same = [jnp.where(kv_half[g % 2], vb[g // 2], 0.0) for g in range(KV_HEADS)]
    v_swap = [jnp.where(kv_half[1 - g % 2], vb_sw[g // 2], 0.0) for g in range(KV_HEADS)]
    qn = [head_rms(q_ref[:, j * LANES:(j + 1) * LANES], qg_ref[...]) for j in range(Q_DIM // LANES)]

    rep = lambda x: jnp.concatenate([x] * Q_PER_KV, axis=0)
    from_prev4, delta4 = rep(from_prev), rep(deltaf)
    neg4 = [rep(no_prev)] + [None] * (SWA_QBLOCKS - 1)
    col = lambda ref, hs: jnp.concatenate([jnp.broadcast_to(ref[:, h:h + 1], (blk, 1)) for h in hs], axis=0)
    order = lambda g: [g * Q_PER_KV + g % 2, g * Q_PER_KV + g % 2 + 2,
                       g * Q_PER_KV + 1 - g % 2, g * Q_PER_KV + 3 - g % 2]
    units = [(u, g) for u in range(SWA_QBLOCKS) for g in range(KV_HEADS)]
    keys = lambda x, u: x[u * blk:(u + 2) * blk]
    qrow = lambda x, u: x[u * blk:(u + 1) * blk]

    qms = [[jnp.where(q_half[h % 2], qrow(qn[h // 2], u), 0.0) for h in order(g)] for u, g in units]
    scs = [jnp.concatenate([_dot_nt(jnp.concatenate(qm[:2], axis=0), keys(kn[g // 2], u)),
                            _dot_nt(jnp.concatenate(qm[2:], axis=0), keys(kn_sw[g // 2], u))], axis=0)
           for qm, (u, g) in zip(qms, units)]
    sinks = [col(sink_ref, order(g)) for u, g in units]
    ss = []
    for sc, (u, g) in zip(scs, units):
        s = jnp.where(from_prev4, sc[:, :blk], sc[:, blk:]) * scale - col(slope_ref, order(g)) * delta4
        ss.append(s if neg4[u] is None else s + neg4[u])
    ms = [jnp.maximum(jnp.max(s, axis=-1, keepdims=True), sk) for s, sk in zip(ss, sinks)]
    ps = [jnp.exp(s - m) for s, m in zip(ss, ms)]
    invs = [1.0 / (jnp.sum(p, axis=-1, keepdims=True) + jnp.exp(sk - m)) for p, sk, m in zip(ps, sinks, ms)]
    pcats = [jnp.concatenate([jnp.where(from_prev4, p, 0.0), jnp.where(from_prev4, 0.0, p)], axis=1) for p in ps]
    outs = [jnp.concatenate([_dot(pc[:2 * blk], keys(v_same[g], u)), _dot(pc[2 * blk:], keys(v_swap[g], u))],
                            axis=0) * inv
            for pc, inv, (u, g) in zip(pcats, invs, units)]
    for o, (u, g) in zip(outs, units):
        hs = order(g)
        for j in sorted({h // 2 for h in hs}):
            pair = sum(o[idx * blk:(idx + 1) * blk] for idx, h in enumerate(hs) if h // 2 == j)
            o_ref[u * blk:(u + 1) * blk, j * LANES:(j + 1) * LANES] = pair.astype(o_ref.dtype)


def _swa(proj, batch, seq, q_gain, k_gain, sinks):
    nbk = seq // CHUNK
    qrows = SWA_QBLOCKS * CHUNK
    nsteps = seq // qrows
    n = batch * seq
    slopes = (2.0 ** (-8.0 * jnp.arange(1, ATT_HEADS + 1, dtype=F32) / ATT_HEADS))
    pad = lambda v: jnp.pad(v.astype(F32), (0, LANES - v.shape[0])).reshape(1, LANES)
    pair = lambda v: jnp.tile(v.astype(F32), LANES // HEAD_DIM).reshape(1, LANES)
    full = lambda shape: pl.BlockSpec(shape, lambda b, j: (0, 0))
    kvw = 2 * KV_DIM
    return pl.pallas_call(
        _swa_kernel,
        grid=(batch, nsteps),
        in_specs=[pl.BlockSpec((qrows, Q_DIM), lambda b, j: (b * nsteps + j, 0)),
                  pl.BlockSpec((CHUNK, kvw),
                               lambda b, j: (b * nbk + jnp.maximum(SWA_QBLOCKS * j - 1, 0), Q_DIM // kvw)),
                  pl.BlockSpec((qrows, kvw), lambda b, j: (b * nsteps + j, Q_DIM // kvw)),
                  full((1, LANES)), full((1, LANES)), full((1, LANES)), full((1, LANES))],
        out_specs=pl.BlockSpec((qrows, Q_DIM), lambda b, j: (b * nsteps + j, 0)),
        out_shape=jax.ShapeDtypeStruct((n, Q_DIM), BF16),
        compiler_params=_cparams(("arbitrary", "arbitrary")),
        name="swa",
    )(proj, proj, proj, pair(q_gain), pair(k_gain), pad(slopes), pad(sinks))


def _even_in_weight(w):
    rw = SSD_COLS
    cols = jnp.concatenate([w[:, :SSD_INNER + SSD_CONV_DIM], w[:, rw:],
                            w[:, SSD_INNER + SSD_CONV_DIM:SSD_COLS]], axis=1)
    return jnp.pad(cols, ((0, 0), (0, EVEN_COLS_PAD - cols.shape[1]))).astype(BF16)


def kernel(x, ln_mix, ln_ffn, e_w_in, e_w_out, ssd_conv_w, ssd_conv_b, ssd_dt_bias, ssd_a_log, ssd_d, ssd_norm,
           rwkv_mu, rwkv_w0, rwkv_w2, rwkv_a0, rwkv_a2, rwkv_g2, rwkv_k_k, rwkv_k_a, rwkv_r_k, rwkv_ln_w,
           rwkv_ln_b, o_w_in, o_w_out, attn_q_norm, attn_k_norm, attn_sinks, moe_w_coarse, moe_b_coarse,
           moe_w_fine, moe_b_fine, moe_w_gate, moe_w_up, moe_w_down):
    batch, seq, d = x.shape
    n = batch * seq
    tm = min(512, n)
    tm_mm = min(1024, n)
    h = x.reshape(n, d)

    def moe(layer, hf, route):
        plan = _moe_plan(route, tm)
        xs = _dispatch(hf, plan, tm)
        ys = _experts(xs, plan, layer, moe_w_gate, moe_w_up, moe_w_down)
        return ys, plan["dest3"]

    def router_w(layer):
        return _router_weights(moe_w_coarse[layer], moe_b_coarse[layer], moe_w_fine[layer], moe_b_fine[layer])

    proj = _norm_proj(h, ln_mix[0], _even_in_weight(e_w_in[0]), tm_mm, EVEN_COLS_PAD // 3)
    y_ssd, y_rwkv = _mixers(
        proj, batch, seq,
        (ssd_conv_w[0], ssd_conv_b[0], ssd_dt_bias[0], ssd_a_log[0], ssd_d[0], ssd_norm[0]),
        (rwkv_mu[0], rwkv_w0[0], rwkv_w2[0], rwkv_a0[0], rwkv_a2[0], rwkv_g2[0], rwkv_k_k[0], rwkv_k_a[0],
         rwkv_r_k[0].reshape(-1), rwkv_ln_w[0], rwkv_ln_b[0]))
    w_out = e_w_out[0].astype(BF16)
    wr, br = router_w(0)
    h, hf, route = _outproj_router([y_ssd, y_rwkv], [w_out[:SSD_INNER], w_out[SSD_INNER:]], h, ln_ffn[0],
                                   wr, br, tm_mm)
    ys, dest3 = moe(0, hf, route)

    h, proj = _combine_norm_proj(h, ys, route, dest3, ln_mix[1], o_w_in[0].astype(BF16), tm)
    att = _swa(proj, batch, seq, attn_q_norm[0], attn_k_norm[0], attn_sinks[0])
    wr, br = router_w(1)
    h, hf, route = _outproj_router([att], [o_w_out[0].astype(BF16)], h, ln_ffn[1], wr, br, tm_mm)
    ys, dest3 = moe(1, hf, route)
    out = _combine(h, ys, route, dest3, tm)
    return out.reshape(batch, seq, d)
```

```python
import functools
import math

import jax
import jax.numpy as jnp
from jax import lax
from jax.experimental import pallas as pl
from jax.experimental.pallas import tpu as pltpu

F32 = jnp.float32
BF16 = jnp.bfloat16

D_MODEL = 1024
SSD_HEADS = 16
SSD_HEAD_DIM = 64
SSD_INNER = 1024
SSD_GROUPS = 4
SSD_STATE = 128
SSD_CONV = 4
SSD_CONV_DIM = 2048
SSD_COLS = 3088
RWKV_HEADS = 16
RWKV_HEAD_DIM = 64
RWKV_DIM = 1024
DECAY_LORA = 64
AAA_LORA = 64
GATE_LORA = 128
RWKV_GN_EPS = 64e-5
ATT_HEADS = 16
KV_HEADS = 4
Q_PER_KV = 4
HEAD_DIM = 64
Q_DIM = 1024
KV_DIM = 256
WINDOW = 128
EXPERT_GROUPS = 4
EXPERTS_PER_GROUP = 8
N_EXPERTS = 32
TOP_K = 2
EXPERT_HIDDEN = 512
MOE_BLOCK = 512
NORM_EPS = 1e-6

LANES = 128
CHUNK = 128
TAIL = 8
DMA_UNROLL = 8
SWA_QBLOCKS = 8
EVEN_COLS_PAD = 6528
VMEM_LIMIT = 56 * 1024 * 1024
NEG_BIG = -1e30
assert WINDOW == CHUNK


def _cparams(sem):
    return pltpu.CompilerParams(dimension_semantics=sem, vmem_limit_bytes=VMEM_LIMIT)


def _dot(a, b):
    return jnp.dot(a.astype(BF16), b.astype(BF16), preferred_element_type=F32)


def _dot_nt(a, b):
    return lax.dot_general(a.astype(BF16), b.astype(BF16), (((1,), (1,)), ((), ())),
                           preferred_element_type=F32)


def _dot_tn(a, b):
    return lax.dot_general(a.astype(BF16), b.astype(BF16), (((0,), (0,)), ((), ())),
                           preferred_element_type=F32)


def _split3(x):
    x1 = x.astype(BF16)
    r1 = x - x1.astype(F32)
    x2 = r1.astype(BF16)
    x3 = (r1 - x2.astype(F32)).astype(BF16)
    return x1, x2, x3


def _dot_exact_lhs(m_bf16, x):
    x1, x2, x3 = _split3(x)
    f = lambda p: jnp.dot(m_bf16, p, preferred_element_type=F32)
    return f(x1) + f(x2) + f(x3)


def _dot_exact_rhs(x, m_bf16):
    x1, x2, x3 = _split3(x)
    f = lambda p: jnp.dot(p, m_bf16, preferred_element_type=F32)
    return f(x1) + f(x2) + f(x3)


def _dot_f32(x, w):
    x1, x2, _ = _split3(x)
    w1, w2, _ = _split3(w)
    f = lambda p, q: jnp.dot(p, q, preferred_element_type=F32)
    return f(x1, w1) + (f(x1, w2) + f(x2, w1))


def _sigmoid(x):
    return 1.0 / (1.0 + jnp.exp(-x))


def _silu(x):
    return x * _sigmoid(x)


def _softplus(x):
    return jnp.maximum(x, 0.0) + jnp.log(1.0 + jnp.exp(-jnp.abs(x)))


def _rms(x, gain):
    return x * lax.rsqrt(jnp.mean(x * x, axis=-1, keepdims=True) + NORM_EPS) * gain


def _iota2(shape, dim):
    return lax.broadcasted_iota(jnp.int32, shape, dim)


def _tri_incl_bf16(n):
    return jnp.where(_iota2((n, n), 0) >= _iota2((n, n), 1), 1.0, 0.0).astype(BF16)


def _norm_proj_kernel(h_ref, g_ref, w_ref, o_ref):
    o_ref[...] = _dot(_rms(h_ref[...], g_ref[...]), w_ref[...])


def _norm_proj(h, gain, w_bf16, tm, tn):
    n, d = h.shape
    c = w_bf16.shape[1]
    return pl.pallas_call(
        _norm_proj_kernel,
        grid=(c // tn, n // tm),
        in_specs=[pl.BlockSpec((tm, d), lambda j, i: (i, 0)),
                  pl.BlockSpec((1, d), lambda j, i: (0, 0)),
                  pl.BlockSpec((d, tn), lambda j, i: (0, j))],
        out_specs=pl.BlockSpec((tm, tn), lambda j, i: (i, j)),
        out_shape=jax.ShapeDtypeStruct((n, c), F32),
        compiler_params=_cparams(("arbitrary", "arbitrary")),
        name="norm_proj",
    )(h, gain.reshape(1, d), w_bf16)


def _gathered_combine(h_ref, route_ref, dest_ref, destn_ref, ys_hbm, yg, gsem):
    i = pl.program_id(0)
    nt = pl.num_programs(0)
    slot = lax.rem(i, 2)
    tm = h_ref.shape[0]

    def row_copy(idx_ref, buf_slot, t, c):
        return pltpu.make_async_copy(ys_hbm.at[pl.ds(idx_ref[0, 0, TOP_K * t + c], 1), :],
                                     yg.at[buf_slot, c, pl.ds(t, 1), :], gsem.at[buf_slot])

    def wait_tile(buf_slot):
        for c in range(TOP_K):
            pltpu.make_async_copy(ys_hbm.at[pl.ds(0, tm), :], yg.at[buf_slot, c], gsem.at[buf_slot]).wait()

    @pl.when(i == 0)
    def _():
        def body(g, carry):
            for u in range(DMA_UNROLL):
                for c in range(TOP_K):
                    row_copy(dest_ref, 0, g * DMA_UNROLL + u, c).start()
            return carry
        lax.fori_loop(0, tm // DMA_UNROLL, body, 0)

    wait_tile(slot)
    for t in range(tm):
        for c in range(TOP_K):
            row_copy(destn_ref, 1 - slot, t, c).start()
    r = route_ref[...]
    x = h_ref[...] + r[:, 2:3] * yg[slot, 0] + r[:, 3:4] * yg[slot, 1]

    def finalize():
        @pl.when(i == nt - 1)
        def _():
            wait_tile(1 - slot)
    return x, finalize


def _combine_norm_proj_kernel(h_ref, route_ref, dest_ref, destn_ref, g_ref, w_ref, ys_hbm, hnew_ref, o_ref,
                              yg, gsem):
    x, finalize = _gathered_combine(h_ref, route_ref, dest_ref, destn_ref, ys_hbm, yg, gsem)
    hnew_ref[...] = x
    o_ref[...] = _dot(_rms(x, g_ref[...]), w_ref[...])
    finalize()


def _combine_kernel(h_ref, route_ref, dest_ref, destn_ref, ys_hbm, o_ref, yg, gsem):
    x, finalize = _gathered_combine(h_ref, route_ref, dest_ref, destn_ref, ys_hbm, yg, gsem)
    o_ref[...] = x
    finalize()


def _combine_specs(n, d, tm):
    nt = n // tm
    smem = lambda fn: pl.BlockSpec((1, 1, TOP_K * tm), fn, memory_space=pltpu.SMEM)
    in_specs = [pl.BlockSpec((tm, d), lambda i: (i, 0)),
                pl.BlockSpec((tm, LANES), lambda i: (i, 0)),
                smem(lambda i: (i, 0, 0)),
                smem(lambda i: (jnp.minimum(i + 1, nt - 1), 0, 0))]
    scratch = [pltpu.VMEM((2, TOP_K, tm, d), F32), pltpu.SemaphoreType.DMA((2,))]
    return in_specs, scratch


def _combine_norm_proj(h, ys, route, dest3, gain, w_bf16, tm):
    n, d = h.shape
    c = w_bf16.shape[1]
    in_specs, scratch = _combine_specs(n, d, tm)
    return pl.pallas_call(
        _combine_norm_proj_kernel,
        grid=(n // tm,),
        in_specs=in_specs + [pl.BlockSpec((1, d), lambda i: (0, 0)),
                             pl.BlockSpec((d, c), lambda i: (0, 0)),
                             pl.BlockSpec(memory_space=pl.ANY)],
        out_specs=[pl.BlockSpec((tm, d), lambda i: (i, 0)),
                   pl.BlockSpec((tm, c), lambda i: (i, 0))],
        out_shape=[jax.ShapeDtypeStruct((n, d), F32), jax.ShapeDtypeStruct((n, c), F32)],
        scratch_shapes=scratch,
        compiler_params=_cparams(("arbitrary",)),
        name="combine_norm_proj",
    )(h, route, dest3, dest3, gain.reshape(1, d), w_bf16, ys)


def _combine(h, ys, route, dest3, tm):
    n, d = h.shape
    in_specs, scratch = _combine_specs(n, d, tm)
    return pl.pallas_call(
        _combine_kernel,
        grid=(n // tm,),
        in_specs=in_specs + [pl.BlockSpec(memory_space=pl.ANY)],
        out_specs=pl.BlockSpec((tm, d), lambda i: (i, 0)),
        out_shape=jax.ShapeDtypeStruct((n, d), F32),
        scratch_shapes=scratch,
        compiler_params=_cparams(("arbitrary",)),
        name="combine",
    )(h, route, dest3, dest3, ys)


def _shifted_taps(buf, u_ref, n_taps):
    buf[TAIL:TAIL + CHUNK, :] = u_ref[...]
    taps = [buf[TAIL - j:TAIL - j + CHUNK, :] for j in range(n_taps)]
    return taps


def _carry_tail(buf):
    buf[0:TAIL, :] = buf[CHUNK:CHUNK + TAIL, :]


def _ssd_stages(z_ref, x_ref, bc_ref, dt_ref, cwx_ref, cbx_ref, cwbc_ref, cbbc_ref, dtb_ref, alog_ref,
                dskip_ref, nw_ref, hexp_ref, o_ref, xbuf, bcbuf, state):
    st = {}
    gw = SSD_INNER // SSD_GROUPS

    def init():
        @pl.when(pl.program_id(1) == 0)
        def _():
            xbuf[0:TAIL, :] = jnp.zeros((TAIL, SSD_INNER), F32)
            bcbuf[0:TAIL, :] = jnp.zeros((TAIL, SSD_INNER), F32)
            state[...] = jnp.zeros_like(state)

    def conv(buf, u_ref, w_ref, b_ref):
        taps = _shifted_taps(buf, u_ref, SSD_CONV)
        acc = b_ref[...] + taps[0] * w_ref[3:4, :]
        for j in range(1, SSD_CONV):
            acc = acc + taps[j] * w_ref[3 - j:4 - j, :]
        _carry_tail(buf)
        return _silu(acc)

    def convs():
        st["xs"] = conv(xbuf, x_ref, cwx_ref, cbx_ref)
        st["bc"] = conv(bcbuf, bc_ref, cwbc_ref, cbbc_ref)

    def decays():
        lane = _iota2((CHUNK, LANES), 1)
        dt = _softplus(dt_ref[...] + dtb_ref[...])
        adt = jnp.where(lane < SSD_HEADS, -jnp.exp(alog_ref[...]) * dt, 0.0)
        cum = _dot_exact_lhs(_tri_incl_bf16(CHUNK), adt)
        hexp = hexp_ref[...]
        cum_full = _dot_exact_rhs(cum, hexp)
        tot_full = cum_full[CHUNK - 1:CHUNK, :]
        xd = st["xs"] * _dot_exact_rhs(dt, hexp)
        st.update(cum=cum, cum_t=cum.T, tot_full=tot_full, xd=xd, xds=xd * jnp.exp(tot_full - cum_full),
                  eac=jnp.exp(cum_full), lane_lo=lane < SSD_HEAD_DIM,
                  causal=_iota2((CHUNK, CHUNK), 0) >= _iota2((CHUNK, CHUNK), 1), y_parts=[])

    def group(g):
        bc, cum, cum_t, xd = st["bc"], st["cum"], st["cum_t"], st["xd"]
        bg = bc[:, g * SSD_STATE:(g + 1) * SSD_STATE]
        cg = bc[:, (SSD_GROUPS + g) * SSD_STATE:(SSD_GROUPS + g + 1) * SSD_STATE]
        cb = _dot_nt(cg, bg)
        s_prev = state[:, g * gw:(g + 1) * gw]
        y_off = _dot(cg, s_prev) * st["eac"][:, g * gw:(g + 1) * gw]
        s_new = _dot(bg.T, st["xds"][:, g * gw:(g + 1) * gw])
        state[:, g * gw:(g + 1) * gw] = jnp.exp(st["tot_full"][:, g * gw:(g + 1) * gw]) * s_prev + s_new
        for pr in range(2):
            lo = g * gw + pr * LANES
            xd_pair = xd[:, lo:lo + LANES]
            yd = jnp.zeros((CHUNK, LANES), F32)
            for k in range(2):
                h = (lo // SSD_HEAD_DIM) + k
                diff = cum[:, h:h + 1] - cum_t[h:h + 1, :]
                decay = jnp.exp(jnp.where(st["causal"], diff, NEG_BIG))
                keep = st["lane_lo"] if k == 0 else jnp.logical_not(st["lane_lo"])
                yd = yd + _dot(cb * decay, jnp.where(keep, xd_pair, 0.0))
            st["y_parts"].append(yd + y_off[:, pr * LANES:(pr + 1) * LANES])

    def finish():
        y = jnp.concatenate(st["y_parts"], axis=1) + dskip_ref[...] * st["xs"]
        y = y * _silu(z_ref[...])
        outs = []
        for g in range(SSD_GROUPS):
            yg = y[:, g * gw:(g + 1) * gw]
            outs.append(yg * lax.rsqrt(jnp.mean(yg * yg, axis=-1, keepdims=True) + 1e-5))
        o_ref[...] = (jnp.concatenate(outs, axis=1) * nw_ref[...]).astype(o_ref.dtype)

    return [init, convs, decays] + [functools.partial(group, g) for g in range(SSD_GROUPS)] + [finish]


def _head_expand(n_heads, head_dim):
    h = jnp.arange(LANES)[:, None]
    l = jnp.arange(n_heads * head_dim)[None, :]
    return (l // head_dim == h).astype(BF16)


def _ssd_operands(proj, nc, conv_w, conv_b, dt_bias, a_log, d_skip, norm_w):
    pad16 = lambda v: jnp.pad(v.astype(F32), (0, LANES - v.shape[0])).reshape(1, LANES)
    row_spec = lambda w, blk: pl.BlockSpec((CHUNK, w), lambda b, c: (b * nc + c, blk))
    full = lambda shape: pl.BlockSpec(shape, lambda b, c: (0, 0))
    in_specs = [row_spec(SSD_INNER, 0),
                row_spec(SSD_INNER, 1),
                row_spec(SSD_INNER, 2),
                row_spec(LANES, 50),
                full((SSD_CONV, SSD_INNER)), full((1, SSD_INNER)),
                full((SSD_CONV, SSD_INNER)), full((1, SSD_INNER)),
                full((1, LANES)), full((1, LANES)),
                full((1, SSD_INNER)), full((1, SSD_INNER)),
                full((LANES, SSD_INNER))]
    operands = [proj, proj, proj, proj,
                conv_w[:, :SSD_INNER], conv_b[:SSD_INNER].reshape(1, -1),
                conv_w[:, SSD_INNER:], conv_b[SSD_INNER:].reshape(1, -1),
                pad16(dt_bias), pad16(a_log),
                jnp.repeat(d_skip.astype(F32), SSD_HEAD_DIM).reshape(1, -1), norm_w.reshape(1, -1),
                _head_expand(SSD_HEADS, SSD_HEAD_DIM)]
    scratch = [pltpu.VMEM((TAIL + CHUNK, SSD_INNER), F32),
               pltpu.VMEM((TAIL + CHUNK, SSD_INNER), F32),
               pltpu.VMEM((SSD_STATE, SSD_INNER), F32)]
    return in_specs, operands, scratch


def _rwkv_kernel(r_ref, k_ref, v_ref, lo_ref, mur_ref, muk_ref, muv_ref, mulo_ref, w0_ref, w2_ref, a0_ref,
                 a2_ref, g2_ref, kk_ref, ka_ref, rk_ref, lnw_ref, lnb_ref, o_ref,
                 rbuf, kbuf, vbuf, lobuf, state, side_work=()):
    side = iter(side_work)
    run_side = lambda: next(side, lambda: None)()
    run_side()
    c = pl.program_id(1)

    @pl.when(c == 0)
    def _():
        rbuf[0:TAIL, :] = jnp.zeros((TAIL, RWKV_DIM), F32)
        kbuf[0:TAIL, :] = jnp.zeros((TAIL, RWKV_DIM), F32)
        vbuf[0:TAIL, :] = jnp.zeros((TAIL, RWKV_DIM), F32)
        lobuf[0:TAIL, :] = jnp.zeros((TAIL, 2 * LANES), F32)
        state[...] = jnp.zeros_like(state)

    def shift(buf, u_ref, mu_ref):
        cur, prev = _shifted_taps(buf, u_ref, 2)
        _carry_tail(buf)
        return cur + mu_ref[...] * (prev - cur)

    r = shift(rbuf, r_ref, mur_ref)
    k = shift(kbuf, k_ref, muk_ref)
    v = shift(vbuf, v_ref, muv_ref)
    lo = shift(lobuf, lo_ref, mulo_ref)
    wa = lo[:, :LANES]
    logw = -math.exp(-0.5) * _sigmoid(w0_ref[...] + _dot(jnp.tanh(wa), w2_ref[...]))
    a = _sigmoid(a0_ref[...] + _dot(wa, a2_ref[...]))
    g = _dot(_sigmoid(lo[:, LANES:]), g2_ref[...])

    lane = _iota2((CHUNK, LANES), 1)
    head0 = lane < RWKV_HEAD_DIM
    bd = (_iota2((LANES, LANES), 0) // RWKV_HEAD_DIM) == (_iota2((LANES, LANES), 1) // RWKV_HEAD_DIM)
    bd_ones = jnp.where(bd, 1.0, 0.0).astype(BF16)

    def head_sum(x):
        return jnp.dot(x.astype(BF16), bd_ones, preferred_element_type=F32)

    tri = _tri_incl_bf16(CHUNK)
    cum = _dot_exact_lhs(tri, logw)
    cume = cum - logw
    cmid = cum[CHUNK // 2 - 1:CHUNK // 2, :]
    cend = cum[CHUNK - 1:CHUNK, :]
    e_in_mid = jnp.exp(cum - cmid)
    e_ex_mid = jnp.exp(cume - cmid)
    e_mid_in = jnp.exp(cmid - cum)
    e_ex = jnp.exp(cume)
    e_in = jnp.exp(cum)
    e_end = jnp.exp(cend - cum)
    e_tot = jnp.exp(cend)

    row = _iota2((CHUNK, CHUNK), 0)
    col = _iota2((CHUNK, CHUNK), 1)
    strict = row > col
    incl = row >= col
    zeros = jnp.zeros((CHUNK, LANES), F32)

    n_blocks = RWKV_DIM // LANES
    blk = []
    for p in range(n_blocks):
        sl = slice(p * LANES, (p + 1) * LANES)
        rp, vp, ap = r[:, sl], v[:, sl], a[:, sl]
        kkp = k[:, sl] * kk_ref[:, sl]
        kkp = kkp * lax.rsqrt(jnp.maximum(head_sum(kkp * kkp), 1e-24))
        kp = k[:, sl] * (1.0 + (ap - 1.0) * ka_ref[:, sl])
        aap = -kkp
        bp = kkp * ap
        a_mid = aap * e_ex_mid[:, sl]
        r_mid = rp * e_in_mid[:, sl]
        bf = lambda x: x.astype(BF16)
        lhs = jnp.concatenate([bf(jnp.where(head0, a_mid, 0.0)), bf(jnp.where(head0, 0.0, a_mid)),
                               bf(jnp.where(head0, r_mid, 0.0)), bf(jnp.where(head0, 0.0, r_mid))], axis=0)
        rhs = jnp.concatenate([bf(bp * e_mid_in[:, sl]), bf(kp * e_mid_in[:, sl])], axis=0)
        a_abs = aap * e_ex[:, sl]
        blk.append(dict(sl=sl, rp=rp, vp=vp, kp=kp, r_abs=rp * e_in[:, sl],
                        a_abs=(bf(jnp.where(head0, a_abs, 0.0)), bf(jnp.where(head0, 0.0, a_abs))),
                        vm=(bf(jnp.where(head0, vp, 0.0)), bf(jnp.where(head0, 0.0, vp))),
                        b_end=bf(bp * e_end[:, sl]), k_end=bf(kp * e_end[:, sl]), prod=_dot_nt(lhs, rhs)))
    run_side()

    heads = []
    for p in range(n_blocks):
        prod = blk[p]["prod"]
        for hh in range(2):
            heads.append(dict(
                vm=blk[p]["vm"][hh], a_abs=blk[p]["a_abs"][hh],
                a_ab=jnp.where(strict, prod[hh * CHUNK:(hh + 1) * CHUNK, :CHUNK], 0.0),
                a_ak=jnp.where(strict, prod[hh * CHUNK:(hh + 1) * CHUNK, CHUNK:], 0.0).astype(BF16),
                m_rb=jnp.where(incl, prod[(2 + hh) * CHUNK:(3 + hh) * CHUNK, :CHUNK], 0.0).astype(BF16),
                m_rk=jnp.where(incl, prod[(2 + hh) * CHUNK:(3 + hh) * CHUNK, CHUNK:], 0.0).astype(BF16)))

    eye = jnp.where(row == col, 1.0, 0.0)
    n_levels = int(math.log2(CHUNK)) - 1
    ts = [eye + hd["a_ab"] for hd in heads]
    xs = [hd["a_ab"].astype(BF16) for hd in heads]
    avs = [_dot(hd["a_ak"], hd["vm"]).astype(BF16) for hd in heads]
    run_side()
    xs = [jnp.dot(x, x, preferred_element_type=F32).astype(BF16) for x in xs]
    for _ in range(n_levels - 1):
        run_side()
        zs = [jnp.dot(x, jnp.concatenate([x, t.astype(BF16)], axis=1), preferred_element_type=F32)
              for x, t in zip(xs, ts)]
        xs = [z[:, :CHUNK].astype(BF16) for z in zs]
        ts = [t + z[:, CHUNK:] for t, z in zip(ts, zs)]
    ts = [t + jnp.dot(x, t.astype(BF16), preferred_element_type=F32) for x, t in zip(xs, ts)]
    for _ in side:
        _()

    wmats = [_dot(t, jnp.concatenate([hd["a_abs"], av], axis=1)).astype(BF16)
             for t, hd, av in zip(ts, heads, avs)]
    zeros_bf = zeros.astype(BF16)
    outs = [_dot(jnp.concatenate([hd["m_rb"], hd["m_rk"]], axis=1),
                 jnp.concatenate([jnp.concatenate([wm[:, LANES:], wm[:, :LANES]], axis=1),
                                  jnp.concatenate([hd["vm"], zeros_bf], axis=1)], axis=0))
            for wm, hd in zip(wmats, heads)]

    zts = []
    for p in range(n_blocks):
        w0h, w1h = wmats[2 * p], wmats[2 * p + 1]
        ui = w0h[:, LANES:] + w1h[:, LANES:]
        a_eff = w0h[:, :LANES] + w1h[:, :LANES]
        zts.append(_dot_tn(jnp.concatenate([jnp.concatenate([ui, a_eff], axis=1),
                                            jnp.concatenate([blk[p]["vp"].astype(BF16), zeros_bf], axis=1)],
                                           axis=0),
                           jnp.concatenate([blk[p]["b_end"], blk[p]["k_end"]], axis=0)))

    ys = []
    for p in range(n_blocks):
        s0 = state[p]
        o0, o1 = outs[2 * p], outs[2 * p + 1]
        r_eff = blk[p]["r_abs"] + o0[:, LANES:] + o1[:, LANES:]
        ys.append(_dot_nt(r_eff, s0) + o0[:, :LANES] + o1[:, :LANES])
        h_intra = jnp.where(bd, zts[p][:LANES, :], 0.0)
        g_corr = jnp.where(bd, zts[p][LANES:, :], 0.0)
        state[p] = s0 * e_tot[:, blk[p]["sl"]] + _dot(s0, g_corr) + h_intra

    means = [head_sum(y) * (1.0 / RWKV_HEAD_DIM) for y in ys]
    devs = [y - m for y, m in zip(ys, means)]
    vars_ = [head_sum(dv * dv) * (1.0 / RWKV_HEAD_DIM) for dv in devs]
    for p in range(n_blocks):
        sl = blk[p]["sl"]
        yn = devs[p] * lax.rsqrt(vars_[p] + RWKV_GN_EPS) * lnw_ref[:, sl] + lnb_ref[:, sl]
        bonus = head_sum(blk[p]["rp"] * blk[p]["kp"] * rk_ref[:, sl])
        o_ref[:, sl] = ((yn + bonus * blk[p]["vp"]) * g[:, sl]).astype(o_ref.dtype)


def _rwkv_operands(proj, nc, mu, w0, w2, a0, a2, g2, k_k, k_a, r_k, ln_w, ln_b):
    d = RWKV_DIM
    row_spec = lambda w, blk: pl.BlockSpec((CHUNK, w), lambda b, c: (b * nc + c, blk))
    full = lambda shape: pl.BlockSpec(shape, lambda b, c: (0,) * len(shape))
    vec = lambda x: x.astype(F32).reshape(1, -1)
    w2p = jnp.concatenate([w2, jnp.zeros((AAA_LORA, d), w2.dtype)], axis=0).astype(BF16)
    a2p = jnp.concatenate([jnp.zeros((DECAY_LORA, d), a2.dtype), a2], axis=0).astype(BF16)
    in_specs = [row_spec(d, 3), row_spec(d, 4), row_spec(d, 5), row_spec(2 * LANES, 24),
                full((1, d)), full((1, d)), full((1, d)), full((1, 2 * LANES)),
                full((1, d)), full((LANES, d)), full((1, d)), full((LANES, d)), full((LANES, d)),
                full((1, d)), full((1, d)), full((1, d)), full((1, d)), full((1, d))]
    operands = [proj, proj, proj, proj,
                vec(mu[:d]), vec(mu[d:2 * d]), vec(mu[2 * d:3 * d]), vec(mu[3 * d:]),
                vec(w0), w2p, vec(a0), a2p, g2.astype(BF16),
                vec(k_k), vec(k_a), vec(r_k), vec(ln_w), vec(ln_b)]
    scratch = [pltpu.VMEM((TAIL + CHUNK, d), F32), pltpu.VMEM((TAIL + CHUNK, d), F32),
               pltpu.VMEM((TAIL + CHUNK, d), F32), pltpu.VMEM((TAIL + CHUNK, 2 * LANES), F32),
               pltpu.VMEM((d // LANES, LANES, LANES), F32)]
    return in_specs, operands, scratch


def _mixers_kernel(*refs, n_ssd_in, n_rwkv_in, n_ssd_scratch):
    ssd_in = refs[:n_ssd_in]
    rwkv_in = refs[n_ssd_in:n_ssd_in + n_rwkv_in]
    o_ssd, o_rwkv = refs[n_ssd_in + n_rwkv_in:n_ssd_in + n_rwkv_in + 2]
    scratch = refs[n_ssd_in + n_rwkv_in + 2:]
    _rwkv_kernel(*rwkv_in, o_rwkv, *scratch[n_ssd_scratch:],
                 side_work=_ssd_stages(*ssd_in, o_ssd, *scratch[:n_ssd_scratch]))


def _mixers(proj, batch, seq, ssd_params, rwkv_params):
    nc = seq // CHUNK
    n = batch * seq
    s_specs, s_ops, s_scratch = _ssd_operands(proj, nc, *ssd_params)
    r_specs, r_ops, r_scratch = _rwkv_operands(proj, nc, *rwkv_params)
    out_spec = lambda w: pl.BlockSpec((CHUNK, w), lambda b, c: (b * nc + c, 0))
    return pl.pallas_call(
        functools.partial(_mixers_kernel, n_ssd_in=len(s_specs), n_rwkv_in=len(r_specs),
                          n_ssd_scratch=len(s_scratch)),
        grid=(batch, nc),
        in_specs=s_specs + r_specs,
        out_specs=[out_spec(SSD_INNER), out_spec(RWKV_DIM)],
        out_shape=[jax.ShapeDtypeStruct((n, SSD_INNER), BF16), jax.ShapeDtypeStruct((n, RWKV_DIM), BF16)],
        scratch_shapes=s_scratch + r_scratch,
        compiler_params=_cparams(("arbitrary", "arbitrary")),
        name="mixers",
    )(*s_ops, *r_ops)


def _route(logits):
    lane = _iota2(logits.shape, 1)
    lanef = lane.astype(F32)
    big = float(LANES)

    def first_max(x):
        m = jnp.max(x, axis=-1, keepdims=True)
        idx = jnp.min(jnp.where(x == m, lanef, big), axis=-1, keepdims=True)
        return m, idx

    cl = jnp.where(lane < EXPERT_GROUPS, logits, NEG_BIG)
    cmax, grp = first_max(cl)
    p_group = 1.0 / jnp.sum(jnp.exp(cl - cmax), axis=-1, keepdims=True)
    lo = EXPERT_GROUPS + grp * EXPERTS_PER_GROUP
    fl = jnp.where((lanef >= lo) & (lanef < lo + EXPERTS_PER_GROUP), logits, NEG_BIG)
    m0, i0 = first_max(fl)
    m1, i1 = first_max(jnp.where(lanef == i0, NEG_BIG, fl))
    e1 = jnp.exp(m1 - m0)
    g0 = p_group / (1.0 + e1)
    g1 = p_group * e1 / (1.0 + e1)
    return jnp.where(lane == 0, i0 - EXPERT_GROUPS,
                     jnp.where(lane == 1, i1 - EXPERT_GROUPS,
                               jnp.where(lane == 2, g0, jnp.where(lane == 3, g1, 0.0))))


def _outproj_router_kernel(*refs, n_in):
    ys = refs[:n_in]
    ws = refs[n_in:2 * n_in]
    h_ref, g_ref, wr_ref, br_ref, hnew_ref, hf_ref, route_ref = refs[2 * n_in:]
    acc = h_ref[...]
    for y_ref, w_ref in zip(ys, ws):
        acc = acc + jnp.dot(y_ref[...], w_ref[...], preferred_element_type=F32)
    hnew_ref[...] = acc
    hf = _rms(acc, g_ref[...])
    hf_ref[...] = hf
    route_ref[...] = _route(_dot_f32(hf, wr_ref[...]) + br_ref[...])


def _outproj_router(ys, ws, h, gain, w_router, b_router, tm):
    n, d = h.shape
    n_in = len(ys)
    in_specs = ([pl.BlockSpec((tm, y.shape[1]), lambda i: (i, 0)) for y in ys]
                + [pl.BlockSpec(w.shape, lambda i: (0, 0)) for w in ws]
                + [pl.BlockSpec((tm, d), lambda i: (i, 0)),
                   pl.BlockSpec((1, d), lambda i: (0, 0)),
                   pl.BlockSpec((d, LANES), lambda i: (0, 0)),
                   pl.BlockSpec((1, LANES), lambda i: (0, 0))])
    return pl.pallas_call(
        functools.partial(_outproj_router_kernel, n_in=n_in),
        grid=(n // tm,),
        in_specs=in_specs,
        out_specs=[pl.BlockSpec((tm, d), lambda i: (i, 0)),
                   pl.BlockSpec((tm, d), lambda i: (i, 0)),
                   pl.BlockSpec((tm, LANES), lambda i: (i, 0))],
        out_shape=[jax.ShapeDtypeStruct((n, d), F32), jax.ShapeDtypeStruct((n, d), F32),
                   jax.ShapeDtypeStruct((n, LANES), F32)],
        compiler_params=_cparams(("arbitrary",)),
        name="outproj_router",
    )(*ys, *ws, h, gain.reshape(1, d), w_router, b_router)


def _router_weights(w_coarse, b_coarse, w_fine, b_fine):
    d = w_coarse.shape[0]
    wf = jnp.transpose(w_fine, (1, 0, 2)).reshape(d, N_EXPERTS)
    w = jnp.concatenate([w_coarse, wf], axis=1).astype(F32)
    b = jnp.concatenate([b_coarse, b_fine.reshape(N_EXPERTS)]).astype(F32)
    pad = LANES - w.shape[1]
    return jnp.pad(w, ((0, 0), (0, pad))), jnp.pad(b, (0, pad)).reshape(1, LANES)


def _moe_plan(route, tm):
    n = route.shape[0]
    a = n * TOP_K
    n_blocks = a // MOE_BLOCK + N_EXPERTS
    e_flat = route[:, :TOP_K].astype(jnp.int32).reshape(a)
    seg = MOE_BLOCK
    onehot = (e_flat[:, None] == jnp.arange(N_EXPERTS, dtype=jnp.int32)[None, :]).astype(F32)
    onehot = onehot.reshape(a // seg, seg, N_EXPERTS)
    tri = jnp.tril(jnp.ones((seg, seg), F32))
    within = jnp.einsum("ij,bjk->bik", tri, onehot)
    tot = within[:, -1, :]
    offs = jnp.cumsum(tot, axis=0) - tot
    rank = (jnp.sum(onehot * (within + offs[:, None, :]), axis=-1) - 1.0).astype(jnp.int32).reshape(a)
    counts = (offs[-1] + tot[-1]).astype(jnp.int32)
    padded = (counts + MOE_BLOCK - 1) // MOE_BLOCK * MOE_BLOCK
    pad_end = jnp.cumsum(padded)
    pad_start = pad_end - padded
    start_of = jnp.sum(jnp.where(e_flat[:, None] == jnp.arange(N_EXPERTS, dtype=jnp.int32)[None, :],
                                 pad_start[None, :], 0), axis=1)
    dest = (start_of + rank).astype(jnp.int32)
    block_start = jnp.arange(n_blocks, dtype=jnp.int32) * MOE_BLOCK
    block_expert = jnp.minimum(jnp.sum((pad_end[None, :] <= block_start[:, None]).astype(jnp.int32), axis=1),
                               N_EXPERTS - 1).astype(jnp.int32)
    n_used = (pad_end[-1] // MOE_BLOCK).astype(jnp.int32).reshape(1)
    pad_lo = (pad_start + counts).astype(jnp.int32)
    return dict(dest3=dest.reshape(n // tm, 1, TOP_K * tm), block_expert=block_expert, n_used=n_used,
                pad_lo=pad_lo, pad_hi=pad_end.astype(jnp.int32), n_blocks=n_blocks)


def _dispatch_kernel(padlo_ref, padhi_ref, nused_ref, hf_ref, dest_ref, xs_hbm, zblk, sem, zsem):
    i = pl.program_id(0)
    tm = hf_ref.shape[0]

    @pl.when(i == 0)
    def _():
        zblk[...] = jnp.zeros_like(zblk)

        def per_expert(e, carry):
            def zero_row(rw, c2):
                pltpu.make_async_copy(zblk.at[pl.ds(0, 1), :], xs_hbm.at[pl.ds(rw, 1), :], zsem).start()
                return c2
            lax.fori_loop(padlo_ref[e], padhi_ref[e], zero_row, 0)

            def wait_row(rw, c2):
                pltpu.make_async_copy(zblk.at[pl.ds(0, 1), :], xs_hbm.at[pl.ds(rw, 1), :], zsem).wait()
                return c2
            lax.fori_loop(padlo_ref[e], padhi_ref[e], wait_row, 0)
            return carry
        lax.fori_loop(0, N_EXPERTS, per_expert, 0)

        n_blocks = xs_hbm.shape[0] // MOE_BLOCK

        def block_copy(b):
            return pltpu.make_async_copy(
                zblk, xs_hbm.at[pl.ds(pl.multiple_of(b * MOE_BLOCK, MOE_BLOCK), MOE_BLOCK), :], zsem)

        def zero_block(b, carry):
            block_copy(b).start()
            return carry
        lax.fori_loop(nused_ref[0], n_blocks, zero_block, 0)

        def wait_block(b, carry):
            block_copy(b).wait()
            return carry
        lax.fori_loop(nused_ref[0], n_blocks, wait_block, 0)

    for t in range(tm):
        for c in range(TOP_K):
            pltpu.make_async_copy(hf_ref.at[pl.ds(t, 1), :],
                                  xs_hbm.at[pl.ds(dest_ref[0, 0, TOP_K * t + c], 1), :], sem).start()
    for c in range(TOP_K):
        pltpu.make_async_copy(hf_ref, xs_hbm.at[pl.ds(0, tm), :], sem).wait()


def _dispatch(hf, plan, tm):
    n, d = hf.shape
    slots = plan["n_blocks"] * MOE_BLOCK
    grid_spec = pltpu.PrefetchScalarGridSpec(
        num_scalar_prefetch=3,
        grid=(n // tm,),
        in_specs=[pl.BlockSpec((tm, d), lambda i, lo, hi, nu: (i, 0)),
                  pl.BlockSpec((1, 1, TOP_K * tm), lambda i, lo, hi, nu: (i, 0, 0), memory_space=pltpu.SMEM)],
        out_specs=pl.BlockSpec(memory_space=pl.ANY),
        scratch_shapes=[pltpu.VMEM((MOE_BLOCK, d), F32), pltpu.SemaphoreType.DMA(()),
                        pltpu.SemaphoreType.DMA(())])
    return pl.pallas_call(
        _dispatch_kernel,
        grid_spec=grid_spec,
        out_shape=jax.ShapeDtypeStruct((slots, d), F32),
        compiler_params=_cparams(("arbitrary",)),
        name="moe_dispatch",
    )(plan["pad_lo"], plan["pad_hi"], plan["n_used"], hf, plan["dest3"])


def _experts_kernel(bexp_ref, nused_ref, x_ref, wg_ref, wu_ref, wd_ref, y_ref, wg_s, wu_s, wd_s):
    i = pl.program_id(0)
    active = i < nused_ref[0]

    @pl.when(active & ((i == 0) | (bexp_ref[i] != bexp_ref[jnp.maximum(i - 1, 0)])))
    def _():
        wg_s[...] = wg_ref[0].astype(BF16)
        wu_s[...] = wu_ref[0].astype(BF16)
        wd_s[...] = wd_ref[0].astype(BF16)

    @pl.when(active)
    def _():
        x = x_ref[...].astype(BF16)
        hg = jnp.dot(x, wg_s[...], preferred_element_type=F32)
        hu = jnp.dot(x, wu_s[...], preferred_element_type=F32)
        hb = (_silu(hg) * hu).astype(BF16)
        y_ref[...] = jnp.dot(hb, wd_s[...], preferred_element_type=F32)

    @pl.when(i >= nused_ref[0])
    def _():
        y_ref[...] = jnp.zeros_like(y_ref)


def _experts(xs, plan, layer, wg, wu, wd):
    slots, d = xs.shape
    n_blocks = plan["n_blocks"]
    used = lambda i, nu: jnp.minimum(i, nu[0] - 1)
    w_idx = lambda i, be, nu: (layer, be[used(i, nu)], 0, 0)
    grid_spec = pltpu.PrefetchScalarGridSpec(
        num_scalar_prefetch=2,
        grid=(n_blocks,),
        in_specs=[pl.BlockSpec((MOE_BLOCK, d), lambda i, be, nu: (used(i, nu), 0)),
                  pl.BlockSpec((None, 1, d, EXPERT_HIDDEN), w_idx),
                  pl.BlockSpec((None, 1, d, EXPERT_HIDDEN), w_idx),
                  pl.BlockSpec((None, 1, EXPERT_HIDDEN, d), w_idx)],
        out_specs=pl.BlockSpec((MOE_BLOCK, d), lambda i, be, nu: (i, 0)),
        scratch_shapes=[pltpu.VMEM((d, EXPERT_HIDDEN), BF16), pltpu.VMEM((d, EXPERT_HIDDEN), BF16),
                        pltpu.VMEM((EXPERT_HIDDEN, d), BF16)])
    return pl.pallas_call(
        _experts_kernel,
        grid_spec=grid_spec,
        out_shape=jax.ShapeDtypeStruct((slots, d), F32),
        compiler_params=_cparams(("arbitrary",)),
        name="moe_experts",
    )(plan["block_expert"], plan["n_used"], xs, wg, wu, wd)


def _swa_kernel(q_ref, kvp_ref, kvc_ref, qg_ref, kg_ref, slope_ref, sink_ref, o_ref):
    jb = pl.program_id(1)
    blk = CHUNK
    qi = _iota2((blk, blk), 0)
    kj = _iota2((blk, blk), 1)
    from_prev = kj > qi
    deltaf = jnp.where(from_prev, qi + blk - kj, qi - kj).astype(F32)
    no_prev = jnp.where(from_prev, jnp.where(jb > 0, 0.0, NEG_BIG), 0.0)
    scale = HEAD_DIM ** -0.5

    bd = (_iota2((LANES, LANES), 0) // HEAD_DIM) == (_iota2((LANES, LANES), 1) // HEAD_DIM)
    bd_ones = jnp.where(bd, 1.0, 0.0).astype(BF16)

    def head_rms(x, gain):
        sq = x * x
        s1 = sq.astype(BF16)
        s2 = (sq - s1.astype(F32)).astype(BF16)
        ms = (jnp.dot(s1, bd_ones, preferred_element_type=F32)
              + jnp.dot(s2, bd_ones, preferred_element_type=F32)) * (1.0 / HEAD_DIM)
        return x * lax.rsqrt(ms + NORM_EPS) * gain

    n_kv_blk = KV_DIM // LANES
    kv = jnp.concatenate([kvp_ref[...], kvc_ref[...]], axis=0)
    lane_kv = _iota2((kv.shape[0], LANES), 1)
    kv_half = (lane_kv < HEAD_DIM, lane_kv >= HEAD_DIM)
    lane_q = _iota2((blk, LANES), 1)
    q_half = (lane_q < HEAD_DIM, lane_q >= HEAD_DIM)
    kn = [head_rms(kv[:, j * LANES:(j + 1) * LANES], kg_ref[...]) for j in range(n_kv_blk)]
    vb = [kv[:, KV_DIM + j * LANES:KV_DIM + (j + 1) * LANES] for j in range(n_kv_blk)]
    kn_sw = [pltpu.roll(x, HEAD_DIM, 1) for x in kn]
    vb_sw = [pltpu.roll(x, HEAD_DIM, 1) for x in vb]
    v_same = [jnp.where(kv_half[g % 2], vb[g // 2], 0.0) for g in range(KV_HEADS)]
    v_swap = [jnp.where(kv_half[1 - g % 2], vb_sw[g // 2], 0.0) for g in range(KV_HEADS)]
    qn = [head_rms(q_ref[:, j * LANES:(j + 1) * LANES], qg_ref[...]) for j in range(Q_DIM // LANES)]

    rep = lambda x: jnp.concatenate([x] * Q_PER_KV, axis=0)
    from_prev4, delta4 = rep(from_prev), rep(deltaf)
    neg4 = [rep(no_prev)] + [None] * (SWA_QBLOCKS - 1)
    col = lambda ref, hs: jnp.concatenate([jnp.broadcast_to(ref[:, h:h + 1], (blk, 1)) for h in hs], axis=0)
    order = lambda g: [g * Q_PER_KV + g % 2, g * Q_PER_KV + g % 2 + 2,
                       g * Q_PER_KV + 1 - g % 2, g * Q_PER_KV + 3 - g % 2]
    units = [(u, g) for u in range(SWA_QBLOCKS) for g in range(KV_HEADS)]
    keys = lambda x, u: x[u * blk:(u + 2) * blk]
    qrow = lambda x, u: x[u * blk:(u + 1) * blk]

    qms = [[jnp.where(q_half[h % 2], qrow(qn[h // 2], u), 0.0) for h in order(g)] for u, g in units]
    scs = [jnp.concatenate([_dot_nt(jnp.concatenate(qm[:2], axis=0), keys(kn[g // 2], u)),
                            _dot_nt(jnp.concatenate(qm[2:], axis=0), keys(kn_sw[g // 2], u))], axis=0)
           for qm, (u, g) in zip(qms, units)]
    sinks = [col(sink_ref, order(g)) for u, g in units]
    ss = []
    for sc, (u, g) in zip(scs, units):
        s = jnp.where(from_prev4, sc[:, :blk], sc[:, blk:]) * scale - col(slope_ref, order(g)) * delta4
        ss.append(s if neg4[u] is None else s + neg4[u])
    ms = [jnp.maximum(jnp.max(s, axis=-1, keepdims=True), sk) for s, sk in zip(ss, sinks)]
    ps = [jnp.exp(s - m) for s, m in zip(ss, ms)]
    invs = [1.0 / (jnp.sum(p, axis=-1, keepdims=True) + jnp.exp(sk - m)) for p, sk, m in zip(ps, sinks, ms)]
    pcats = [jnp.concatenate([jnp.where(from_prev4, p, 0.0), jnp.where(from_prev4, 0.0, p)], axis=1) for p in ps]
    outs = [jnp.concatenate([_dot(pc[:2 * blk], keys(v_same[g], u)), _dot(pc[2 * blk:], keys(v_swap[g], u))],
                            axis=0) * inv
            for pc, inv, (u, g) in zip(pcats, invs, units)]
    for o, (u, g) in zip(outs, units):
        hs = order(g)
        for j in sorted({h // 2 for h in hs}):
            pair = sum(o[idx * blk:(idx + 1) * blk] for idx, h in enumerate(hs) if h // 2 == j)
            o_ref[u * blk:(u + 1) * blk, j * LANES:(j + 1) * LANES] = pair.astype(o_ref.dtype)


def _swa(proj, batch, seq, q_gain, k_gain, sinks):
    nbk = seq // CHUNK
    qrows = SWA_QBLOCKS * CHUNK
    nsteps = seq // qrows
    n = batch * seq
    slopes = (2.0 ** (-8.0 * jnp.arange(1, ATT_HEADS + 1, dtype=F32) / ATT_HEADS))
    pad = lambda v: jnp.pad(v.astype(F32), (0, LANES - v.shape[0])).reshape(1, LANES)
    pair = lambda v: jnp.tile(v.astype(F32), LANES // HEAD_DIM).reshape(1, LANES)
    full = lambda shape: pl.BlockSpec(shape, lambda b, j: (0, 0))
    kvw = 2 * KV_DIM
    return pl.pallas_call(
        _swa_kernel,
        grid=(batch, nsteps),
        in_specs=[pl.BlockSpec((qrows, Q_DIM), lambda b, j: (b * nsteps + j, 0)),
                  pl.BlockSpec((CHUNK, kvw),
                               lambda b, j: (b * nbk + jnp.maximum(SWA_QBLOCKS * j - 1, 0), Q_DIM // kvw)),
                  pl.BlockSpec((qrows, kvw), lambda b, j: (b * nsteps + j, Q_DIM // kvw)),
                  full((1, LANES)), full((1, LANES)), full((1, LANES)), full((1, LANES))],
        out_specs=pl.BlockSpec((qrows, Q_DIM), lambda b, j: (b * nsteps + j, 0)),
        out_shape=jax.ShapeDtypeStruct((n, Q_DIM), BF16),
        compiler_params=_cparams(("arbitrary", "arbitrary")),
        name="swa",
    )(proj, proj, proj, pair(q_gain), pair(k_gain), pad(slopes), pad(sinks))


def _even_in_weight(w):
    rw = SSD_COLS
    cols = jnp.concatenate([w[:, :SSD_INNER + SSD_CONV_DIM], w[:, rw:],
                            w[:, SSD_INNER + SSD_CONV_DIM:SSD_COLS]], axis=1)
    return jnp.pad(cols, ((0, 0), (0, EVEN_COLS_PAD - cols.shape[1]))).astype(BF16)


def kernel(x, ln_mix, ln_ffn, e_w_in, e_w_out, ssd_conv_w, ssd_conv_b, ssd_dt_bias, ssd_a_log, ssd_d, ssd_norm,
           rwkv_mu, rwkv_w0, rwkv_w2, rwkv_a0, rwkv_a2, rwkv_g2, rwkv_k_k, rwkv_k_a, rwkv_r_k, rwkv_ln_w,
           rwkv_ln_b, o_w_in, o_w_out, attn_q_norm, attn_k_norm, attn_sinks, moe_w_coarse, moe_b_coarse,
           moe_w_fine, moe_b_fine, moe_w_gate, moe_w_up, moe_w_down):
    batch, seq, d = x.shape
    n = batch * seq
    tm = min(512, n)
    tm_mm = min(1024, n)
    h = x.reshape(n, d)

    def moe(layer, hf, route):
        plan = _moe_plan(route, tm)
        xs = _dispatch(hf, plan, tm)
        ys = _experts(xs, plan, layer, moe_w_gate, moe_w_up, moe_w_down)
        return ys, plan["dest3"]

    def router_w(layer):
        return _router_weights(moe_w_coarse[layer], moe_b_coarse[layer], moe_w_fine[layer], moe_b_fine[layer])

    proj = _norm_proj(h, ln_mix[0], _even_in_weight(e_w_in[0]), tm_mm, EVEN_COLS_PAD // 3)
    y_ssd, y_rwkv = _mixers(
        proj, batch, seq,
        (ssd_conv_w[0], ssd_conv_b[0], ssd_dt_bias[0], ssd_a_log[0], ssd_d[0], ssd_norm[0]),
        (rwkv_mu[0], rwkv_w0[0], rwkv_w2[0], rwkv_a0[0], rwkv_a2[0], rwkv_g2[0], rwkv_k_k[0], rwkv_k_a[0],
         rwkv_r_k[0].reshape(-1), rwkv_ln_w[0], rwkv_ln_b[0]))
    w_out = e_w_out[0].astype(BF16)
    wr, br = router_w(0)
    h, hf, route = _outproj_router([y_ssd, y_rwkv], [w_out[:SSD_INNER], w_out[SSD_INNER:]], h, ln_ffn[0],
                                   wr, br, tm_mm)
    ys, dest3 = moe(0, hf, route)

    h, proj = _combine_norm_proj(h, ys, route, dest3, ln_mix[1], o_w_in[0].astype(BF16), tm)
    att = _swa(proj, batch, seq, attn_q_norm[0], attn_k_norm[0], attn_sinks[0])
    wr, br = router_w(1)
    h, hf, route = _outproj_router([att], [o_w_out[0].astype(BF16)], h, ln_ffn[1], wr, br, tm_mm)
    ys, dest3 = moe(1, hf, route)
    out = _combine(h, ys, route, dest3, tm)
    return out.reshape(batch, seq, d)
```

```python
import functools
import math

import jax
import jax.numpy as jnp
from jax import lax
from jax.experimental import pallas as pl
from jax.experimental.pallas import tpu as pltpu

F32 = jnp.float32
BF16 = jnp.bfloat16

D_MODEL = 1024
SSD_HEADS = 16
SSD_HEAD_DIM = 64
SSD_INNER = 1024
SSD_GROUPS = 4
SSD_STATE = 128
SSD_CONV = 4
SSD_CONV_DIM = 2048
SSD_COLS = 3088
RWKV_HEADS = 16
RWKV_HEAD_DIM = 64
RWKV_DIM = 1024
DECAY_LORA = 64
AAA_LORA = 64
GATE_LORA = 128
RWKV_GN_EPS = 64e-5
ATT_HEADS = 16
KV_HEADS = 4
Q_PER_KV = 4
HEAD_DIM = 64
Q_DIM = 1024
KV_DIM = 256
WINDOW = 128
EXPERT_GROUPS = 4
EXPERTS_PER_GROUP = 8
N_EXPERTS = 32
TOP_K = 2
EXPERT_HIDDEN = 512
MOE_BLOCK = 512
NORM_EPS = 1e-6

LANES = 128
CHUNK = 128
TAIL = 8
DMA_UNROLL = 8
SWA_QBLOCKS = 8
EVEN_COLS_PAD = 6528
VMEM_LIMIT = 56 * 1024 * 1024
NEG_BIG = -1e30
assert WINDOW == CHUNK


def _cparams(sem):
    return pltpu.CompilerParams(dimension_semantics=sem, vmem_limit_bytes=VMEM_LIMIT)


def _dot(a, b):
    return jnp.dot(a.astype(BF16), b.astype(BF16), preferred_element_type=F32)


def _dot_nt(a, b):
    return lax.dot_general(a.astype(BF16), b.astype(BF16), (((1,), (1,)), ((), ())),
                           preferred_element_type=F32)


def _dot_tn(a, b):
    return lax.dot_general(a.astype(BF16), b.astype(BF16), (((0,), (0,)), ((), ())),
                           preferred_element_type=F32)


def _split3(x):
    x1 = x.astype(BF16)
    r1 = x - x1.astype(F32)
    x2 = r1.astype(BF16)
    x3 = (r1 - x2.astype(F32)).astype(BF16)
    return x1, x2, x3


def _dot_exact_lhs(m_bf16, x):
    x1, x2, x3 = _split3(x)
    f = lambda p: jnp.dot(m_bf16, p, preferred_element_type=F32)
    return f(x1) + f(x2) + f(x3)


def _dot_exact_rhs(x, m_bf16):
    x1, x2, x3 = _split3(x)
    f = lambda p: jnp.dot(p, m_bf16, preferred_element_type=F32)
    return f(x1) + f(x2) + f(x3)


def _dot_f32(x, w):
    x1, x2, _ = _split3(x)
    w1, w2, _ = _split3(w)
    f = lambda p, q: jnp.dot(p, q, preferred_element_type=F32)
    return f(x1, w1) + (f(x1, w2) + f(x2, w1))


def _sigmoid(x):
    return 1.0 / (1.0 + jnp.exp(-x))


def _silu(x):
    return x * _sigmoid(x)


def _softplus(x):
    return jnp.maximum(x, 0.0) + jnp.log(1.0 + jnp.exp(-jnp.abs(x)))


def _rms(x, gain):
    return x * lax.rsqrt(jnp.mean(x * x, axis=-1, keepdims=True) + NORM_EPS) * gain


def _iota2(shape, dim):
    return lax.broadcasted_iota(jnp.int32, shape, dim)


def _tri_incl_bf16(n):
    return jnp.where(_iota2((n, n), 0) >= _iota2((n, n), 1), 1.0, 0.0).astype(BF16)


def _norm_proj_kernel(h_ref, g_ref, w_ref, o_ref):
    o_ref[...] = _dot(_rms(h_ref[...], g_ref[...]), w_ref[...])


def _norm_proj(h, gain, w_bf16, tm, tn):
    n, d = h.shape
    c = w_bf16.shape[1]
    return pl.pallas_call(
        _norm_proj_kernel,
        grid=(c // tn, n // tm),
        in_specs=[pl.BlockSpec((tm, d), lambda j, i: (i, 0)),
                  pl.BlockSpec((1, d), lambda j, i: (0, 0)),
                  pl.BlockSpec((d, tn), lambda j, i: (0, j))],
        out_specs=pl.BlockSpec((tm, tn), lambda j, i: (i, j)),
        out_shape=jax.ShapeDtypeStruct((n, c), F32),
        compiler_params=_cparams(("arbitrary", "arbitrary")),
        name="norm_proj",
    )(h, gain.reshape(1, d), w_bf16)


def _gathered_combine(h_ref, route_ref, dest_ref, destn_ref, ys_hbm, yg, gsem):
    i = pl.program_id(0)
    nt = pl.num_programs(0)
    slot = lax.rem(i, 2)
    tm = h_ref.shape[0]

    def row_copy(idx_ref, buf_slot, t, c):
        return pltpu.make_async_copy(ys_hbm.at[pl.ds(idx_ref[0, 0, TOP_K * t + c], 1), :],
                                     yg.at[buf_slot, c, pl.ds(t, 1), :], gsem.at[buf_slot])

    def wait_tile(buf_slot):
        for c in range(TOP_K):
            pltpu.make_async_copy(ys_hbm.at[pl.ds(0, tm), :], yg.at[buf_slot, c], gsem.at[buf_slot]).wait()

    @pl.when(i == 0)
    def _():
        def body(g, carry):
            for u in range(DMA_UNROLL):
                for c in range(TOP_K):
                    row_copy(dest_ref, 0, g * DMA_UNROLL + u, c).start()
            return carry
        lax.fori_loop(0, tm // DMA_UNROLL, body, 0)

    wait_tile(slot)
    for t in range(tm):
        for c in range(TOP_K):
            row_copy(destn_ref, 1 - slot, t, c).start()
    r = route_ref[...]
    x = h_ref[...] + r[:, 2:3] * yg[slot, 0] + r[:, 3:4] * yg[slot, 1]

    def finalize():
        @pl.when(i == nt - 1)
        def _():
            wait_tile(1 - slot)
    return x, finalize


def _combine_norm_proj_kernel(h_ref, route_ref, dest_ref, destn_ref, g_ref, w_ref, ys_hbm, hnew_ref, o_ref,
                              yg, gsem):
    x, finalize = _gathered_combine(h_ref, route_ref, dest_ref, destn_ref, ys_hbm, yg, gsem)
    hnew_ref[...] = x
    o_ref[...] = _dot(_rms(x, g_ref[...]), w_ref[...])
    finalize()


def _combine_kernel(h_ref, route_ref, dest_ref, destn_ref, ys_hbm, o_ref, yg, gsem):
    x, finalize = _gathered_combine(h_ref, route_ref, dest_ref, destn_ref, ys_hbm, yg, gsem)
    o_ref[...] = x
    finalize()


def _combine_specs(n, d, tm):
    nt = n // tm
    smem = lambda fn: pl.BlockSpec((1, 1, TOP_K * tm), fn, memory_space=pltpu.SMEM)
    in_specs = [pl.BlockSpec((tm, d), lambda i: (i, 0)),
                pl.BlockSpec((tm, LANES), lambda i: (i, 0)),
                smem(lambda i: (i, 0, 0)),
                smem(lambda i: (jnp.minimum(i + 1, nt - 1), 0, 0))]
    scratch = [pltpu.VMEM((2, TOP_K, tm, d), F32), pltpu.SemaphoreType.DMA((2,))]
    return in_specs, scratch


def _combine_norm_proj(h, ys, route, dest3, gain, w_bf16, tm):
    n, d = h.shape
    c = w_bf16.shape[1]
    in_specs, scratch = _combine_specs(n, d, tm)
    return pl.pallas_call(
        _combine_norm_proj_kernel,
        grid=(n // tm,),
        in_specs=in_specs + [pl.BlockSpec((1, d), lambda i: (0, 0)),
                             pl.BlockSpec((d, c), lambda i: (0, 0)),
                             pl.BlockSpec(memory_space=pl.ANY)],
        out_specs=[pl.BlockSpec((tm, d), lambda i: (i, 0)),
                   pl.BlockSpec((tm, c), lambda i: (i, 0))],
        out_shape=[jax.ShapeDtypeStruct((n, d), F32), jax.ShapeDtypeStruct((n, c), F32)],
        scratch_shapes=scratch,
        compiler_params=_cparams(("arbitrary",)),
        name="combine_norm_proj",
    )(h, route, dest3, dest3, gain.reshape(1, d), w_bf16, ys)


def _combine(h, ys, route, dest3, tm):
    n, d = h.shape
    in_specs, scratch = _combine_specs(n, d, tm)
    return pl.pallas_call(
        _combine_kernel,
        grid=(n // tm,),
        in_specs=in_specs + [pl.BlockSpec(memory_space=pl.ANY)],
        out_specs=pl.BlockSpec((tm, d), lambda i: (i, 0)),
        out_shape=jax.ShapeDtypeStruct((n, d), F32),
        scratch_shapes=scratch,
        compiler_params=_cparams(("arbitrary",)),
        name="combine",
    )(h, route, dest3, dest3, ys)


def _shifted_taps(buf, u_ref, n_taps):
    buf[TAIL:TAIL + CHUNK, :] = u_ref[...]
    taps = [buf[TAIL - j:TAIL - j + CHUNK, :] for j in range(n_taps)]
    return taps


def _carry_tail(buf):
    buf[0:TAIL, :] = buf[CHUNK:CHUNK + TAIL, :]


def _ssd_stages(z_ref, x_ref, bc_ref, dt_ref, cwx_ref, cbx_ref, cwbc_ref, cbbc_ref, dtb_ref, alog_ref,
                dskip_ref, nw_ref, hexp_ref, o_ref, xbuf, bcbuf, state):
    st = {}
    gw = SSD_INNER // SSD_GROUPS

    def init():
        @pl.when(pl.program_id(1) == 0)
        def _():
            xbuf[0:TAIL, :] = jnp.zeros((TAIL, SSD_INNER), F32)
            bcbuf[0:TAIL, :] = jnp.zeros((TAIL, SSD_INNER), F32)
            state[...] = jnp.zeros_like(state)

    def conv(buf, u_ref, w_ref, b_ref):
        taps = _shifted_taps(buf, u_ref, SSD_CONV)
        acc = b_ref[...] + taps[0] * w_ref[3:4, :]
        for j in range(1, SSD_CONV):
            acc = acc + taps[j] * w_ref[3 - j:4 - j, :]
        _carry_tail(buf)
        return _silu(acc)

    def convs():
        st["xs"] = conv(xbuf, x_ref, cwx_ref, cbx_ref)
        st["bc"] = conv(bcbuf, bc_ref, cwbc_ref, cbbc_ref)

    def decays():
        lane = _iota2((CHUNK, LANES), 1)
        dt = _softplus(dt_ref[...] + dtb_ref[...])
        adt = jnp.where(lane < SSD_HEADS, -jnp.exp(alog_ref[...]) * dt, 0.0)
        cum = _dot_exact_lhs(_tri_incl_bf16(CHUNK), adt)
        hexp = hexp_ref[...]
        cum_full = _dot_exact_rhs(cum, hexp)
        tot_full = cum_full[CHUNK - 1:CHUNK, :]
        xd = st["xs"] * _dot_exact_rhs(dt, hexp)
        st.update(cum=cum, cum_t=cum.T, tot_full=tot_full, xd=xd, xds=xd * jnp.exp(tot_full - cum_full),
                  eac=jnp.exp(cum_full), lane_lo=lane < SSD_HEAD_DIM,
                  causal=_iota2((CHUNK, CHUNK), 0) >= _iota2((CHUNK, CHUNK), 1), y_parts=[])

    def group(g):
        bc, cum, cum_t, xd = st["bc"], st["cum"], st["cum_t"], st["xd"]
        bg = bc[:, g * SSD_STATE:(g + 1) * SSD_STATE]
        cg = bc[:, (SSD_GROUPS + g) * SSD_STATE:(SSD_GROUPS + g + 1) * SSD_STATE]
        cb = _dot_nt(cg, bg)
        s_prev = state[:, g * gw:(g + 1) * gw]
        y_off = _dot(cg, s_prev) * st["eac"][:, g * gw:(g + 1) * gw]
        s_new = _dot(bg.T, st["xds"][:, g * gw:(g + 1) * gw])
        state[:, g * gw:(g + 1) * gw] = jnp.exp(st["tot_full"][:, g * gw:(g + 1) * gw]) * s_prev + s_new
        for pr in range(2):
            lo = g * gw + pr * LANES
            xd_pair = xd[:, lo:lo + LANES]
            yd = jnp.zeros((CHUNK, LANES), F32)
            for k in range(2):
                h = (lo // SSD_HEAD_DIM) + k
                diff = cum[:, h:h + 1] - cum_t[h:h + 1, :]
                decay = jnp.exp(jnp.where(st["causal"], diff, NEG_BIG))
                keep = st["lane_lo"] if k == 0 else jnp.logical_not(st["lane_lo"])
                yd = yd + _dot(cb * decay, jnp.where(keep, xd_pair, 0.0))
            st["y_parts"].append(yd + y_off[:, pr * LANES:(pr + 1) * LANES])

    def finish():
        y = jnp.concatenate(st["y_parts"], axis=1) + dskip_ref[...] * st["xs"]
        y = y * _silu(z_ref[...])
        outs = []
        for g in range(SSD_GROUPS):
            yg = y[:, g * gw:(g + 1) * gw]
            outs.append(yg * lax.rsqrt(jnp.mean(yg * yg, axis=-1, keepdims=True) + 1e-5))
        o_ref[...] = (jnp.concatenate(outs, axis=1) * nw_ref[...]).astype(o_ref.dtype)

    return [init, convs, decays] + [functools.partial(group, g) for g in range(SSD_GROUPS)] + [finish]


def _head_expand(n_heads, head_dim):
    h = jnp.arange(LANES)[:, None]
    l = jnp.arange(n_heads * head_dim)[None, :]
    return (l // head_dim == h).astype(BF16)


def _ssd_operands(proj, nc, conv_w, conv_b, dt_bias, a_log, d_skip, norm_w):
    pad16 = lambda v: jnp.pad(v.astype(F32), (0, LANES - v.shape[0])).reshape(1, LANES)
    row_spec = lambda w, blk: pl.BlockSpec((CHUNK, w), lambda b, c: (b * nc + c, blk))
    full = lambda shape: pl.BlockSpec(shape, lambda b, c: (0, 0))
    in_specs = [row_spec(SSD_INNER, 0),
                row_spec(SSD_INNER, 1),
                row_spec(SSD_INNER, 2),
                row_spec(LANES, 50),
                full((SSD_CONV, SSD_INNER)), full((1, SSD_INNER)),
                full((SSD_CONV, SSD_INNER)), full((1, SSD_INNER)),
                full((1, LANES)), full((1, LANES)),
                full((1, SSD_INNER)), full((1, SSD_INNER)),
                full((LANES, SSD_INNER))]
    operands = [proj, proj, proj, proj,
                conv_w[:, :SSD_INNER], conv_b[:SSD_INNER].reshape(1, -1),
                conv_w[:, SSD_INNER:], conv_b[SSD_INNER:].reshape(1, -1),
                pad16(dt_bias), pad16(a_log),
                jnp.repeat(d_skip.astype(F32), SSD_HEAD_DIM).reshape(1, -1), norm_w.reshape(1, -1),
                _head_expand(SSD_HEADS, SSD_HEAD_DIM)]
    scratch = [pltpu.VMEM((TAIL + CHUNK, SSD_INNER), F32),
               pltpu.VMEM((TAIL + CHUNK, SSD_INNER), F32),
               pltpu.VMEM((SSD_STATE, SSD_INNER), F32)]
    return in_specs, operands, scratch


def _rwkv_kernel(r_ref, k_ref, v_ref, lo_ref, mur_ref, muk_ref, muv_ref, mulo_ref, w0_ref, w2_ref, a0_ref,
                 a2_ref, g2_ref, kk_ref, ka_ref, rk_ref, lnw_ref, lnb_ref, o_ref,
                 rbuf, kbuf, vbuf, lobuf, state, side_work=()):
    side = iter(side_work)
    run_side = lambda: next(side, lambda: None)()
    run_side()
    c = pl.program_id(1)

    @pl.when(c == 0)
    def _():
        rbuf[0:TAIL, :] = jnp.zeros((TAIL, RWKV_DIM), F32)
        kbuf[0:TAIL, :] = jnp.zeros((TAIL, RWKV_DIM), F32)
        vbuf[0:TAIL, :] = jnp.zeros((TAIL, RWKV_DIM), F32)
        lobuf[0:TAIL, :] = jnp.zeros((TAIL, 2 * LANES), F32)
        state[...] = jnp.zeros_like(state)

    def shift(buf, u_ref, mu_ref):
        cur, prev = _shifted_taps(buf, u_ref, 2)
        _carry_tail(buf)
        return cur + mu_ref[...] * (prev - cur)

    r = shift(rbuf, r_ref, mur_ref)
    k = shift(kbuf, k_ref, muk_ref)
    v = shift(vbuf, v_ref, muv_ref)
    lo = shift(lobuf, lo_ref, mulo_ref)
    wa = lo[:, :LANES]
    logw = -math.exp(-0.5) * _sigmoid(w0_ref[...] + _dot(jnp.tanh(wa), w2_ref[...]))
    a = _sigmoid(a0_ref[...] + _dot(wa, a2_ref[...]))
    g = _dot(_sigmoid(lo[:, LANES:]), g2_ref[...])

    lane = _iota2((CHUNK, LANES), 1)
    head0 = lane < RWKV_HEAD_DIM
    bd = (_iota2((LANES, LANES), 0) // RWKV_HEAD_DIM) == (_iota2((LANES, LANES), 1) // RWKV_HEAD_DIM)
    bd_ones = jnp.where(bd, 1.0, 0.0).astype(BF16)

    def head_sum(x):
        return jnp.dot(x.astype(BF16), bd_ones, preferred_element_type=F32)

    tri = _tri_incl_bf16(CHUNK)
    cum = _dot_exact_lhs(tri, logw)
    cume = cum - logw
    cmid = cum[CHUNK // 2 - 1:CHUNK // 2, :]
    cend = cum[CHUNK - 1:CHUNK, :]
    e_in_mid = jnp.exp(cum - cmid)
    e_ex_mid = jnp.exp(cume - cmid)
    e_mid_in = jnp.exp(cmid - cum)
    e_ex = jnp.exp(cume)
    e_in = jnp.exp(cum)
    e_end = jnp.exp(cend - cum)
    e_tot = jnp.exp(cend)

    row = _iota2((CHUNK, CHUNK), 0)
    col = _iota2((CHUNK, CHUNK), 1)
    strict = row > col
    incl = row >= col
    zeros = jnp.zeros((CHUNK, LANES), F32)

    n_blocks = RWKV_DIM // LANES
    blk = []
    for p in range(n_blocks):
        sl = slice(p * LANES, (p + 1) * LANES)
        rp, vp, ap = r[:, sl], v[:, sl], a[:, sl]
        kkp = k[:, sl] * kk_ref[:, sl]
        kkp = kkp * lax.rsqrt(jnp.maximum(head_sum(kkp * kkp), 1e-24))
        kp = k[:, sl] * (1.0 + (ap - 1.0) * ka_ref[:, sl])
        aap = -kkp
        bp = kkp * ap
        a_mid = aap * e_ex_mid[:, sl]
        r_mid = rp * e_in_mid[:, sl]
        bf = lambda x: x.astype(BF16)
        lhs = jnp.concatenate([bf(jnp.where(head0, a_mid, 0.0)), bf(jnp.where(head0, 0.0, a_mid)),
                               bf(jnp.where(head0, r_mid, 0.0)), bf(jnp.where(head0, 0.0, r_mid))], axis=0)
        rhs = jnp.concatenate([bf(bp * e_mid_in[:, sl]), bf(kp * e_mid_in[:, sl])], axis=0)
        a_abs = aap * e_ex[:, sl]
        blk.append(dict(sl=sl, rp=rp, vp=vp, kp=kp, r_abs=rp * e_in[:, sl],
                        a_abs=(bf(jnp.where(head0, a_abs, 0.0)), bf(jnp.where(head0, 0.0, a_abs))),
                        vm=(bf(jnp.where(head0, vp, 0.0)), bf(jnp.where(head0, 0.0, vp))),
                        b_end=bf(bp * e_end[:, sl]), k_end=bf(kp * e_end[:, sl]), prod=_dot_nt(lhs, rhs)))
    run_side()

    heads = []
    for p in range(n_blocks):
        prod = blk[p]["prod"]
        for hh in range(2):
            heads.append(dict(
                vm=blk[p]["vm"][hh], a_abs=blk[p]["a_abs"][hh],
                a_ab=jnp.where(strict, prod[hh * CHUNK:(hh + 1) * CHUNK, :CHUNK], 0.0),
                a_ak=jnp.where(strict, prod[hh * CHUNK:(hh + 1) * CHUNK, CHUNK:], 0.0).astype(BF16),
                m_rb=jnp.where(incl, prod[(2 + hh) * CHUNK:(3 + hh) * CHUNK, :CHUNK], 0.0).astype(BF16),
                m_rk=jnp.where(incl, prod[(2 + hh) * CHUNK:(3 + hh) * CHUNK, CHUNK:], 0.0).astype(BF16)))

    eye = jnp.where(row == col, 1.0, 0.0)
    n_levels = int(math.log2(CHUNK)) - 1
    ts = [eye + hd["a_ab"] for hd in heads]
    xs = [hd["a_ab"].astype(BF16) for hd in heads]
    avs = [_dot(hd["a_ak"], hd["vm"]).astype(BF16) for hd in heads]
    run_side()
    xs = [jnp.dot(x, x, preferred_element_type=F32).astype(BF16) for x in xs]
    for _ in range(n_levels - 1):
        run_side()
        zs = [jnp.dot(x, jnp.concatenate([x, t.astype(BF16)], axis=1), preferred_element_type=F32)
              for x, t in zip(xs, ts)]
        xs = [z[:, :CHUNK].astype(BF16) for z in zs]
        ts = [t + z[:, CHUNK:] for t, z in zip(ts, zs)]
    ts = [t + jnp.dot(x, t.astype(BF16), preferred_element_type=F32) for x, t in zip(xs, ts)]
    for _ in side:
        _()

    wmats = [_dot(t, jnp.concatenate([hd["a_abs"], av], axis=1)).astype(BF16)
             for t, hd, av in zip(ts, heads, avs)]
    zeros_bf = zeros.astype(BF16)
    outs = [_dot(jnp.concatenate([hd["m_rb"], hd["m_rk"]], axis=1),
                 jnp.concatenate([jnp.concatenate([wm[:, LANES:], wm[:, :LANES]], axis=1),
                                  jnp.concatenate([hd["vm"], zeros_bf], axis=1)], axis=0))
            for wm, hd in zip(wmats, heads)]

    zts = []
    for p in range(n_blocks):
        w0h, w1h = wmats[2 * p], wmats[2 * p + 1]
        ui = w0h[:, LANES:] + w1h[:, LANES:]
        a_eff = w0h[:, :LANES] + w1h[:, :LANES]
        zts.append(_dot_tn(jnp.concatenate([jnp.concatenate([ui, a_eff], axis=1),
                                            jnp.concatenate([blk[p]["vp"].astype(BF16), zeros_bf], axis=1)],
                                           axis=0),
                           jnp.concatenate([blk[p]["b_end"], blk[p]["k_end"]], axis=0)))

    ys = []
    for p in range(n_blocks):
        s0 = state[p]
        o0, o1 = outs[2 * p], outs[2 * p + 1]
        r_eff = blk[p]["r_abs"] + o0[:, LANES:] + o1[:, LANES:]
        ys.append(_dot_nt(r_eff, s0) + o0[:, :LANES] + o1[:, :LANES])
        h_intra = jnp.where(bd, zts[p][:LANES, :], 0.0)
        g_corr = jnp.where(bd, zts[p][LANES:, :], 0.0)
        state[p] = s0 * e_tot[:, blk[p]["sl"]] + _dot(s0, g_corr) + h_intra

    means = [head_sum(y) * (1.0 / RWKV_HEAD_DIM) for y in ys]
    devs = [y - m for y, m in zip(ys, means)]
    vars_ = [head_sum(dv * dv) * (1.0 / RWKV_HEAD_DIM) for dv in devs]
    for p in range(n_blocks):
        sl = blk[p]["sl"]
        yn = devs[p] * lax.rsqrt(vars_[p] + RWKV_GN_EPS) * lnw_ref[:, sl] + lnb_ref[:, sl]
        bonus = head_sum(blk[p]["rp"] * blk[p]["kp"] * rk_ref[:, sl])
        o_ref[:, sl] = ((yn + bonus * blk[p]["vp"]) * g[:, sl]).astype(o_ref.dtype)


def _rwkv_operands(proj, nc, mu, w0, w2, a0, a2, g2, k_k, k_a, r_k, ln_w, ln_b):
    d = RWKV_DIM
    row_spec = lambda w, blk: pl.BlockSpec((CHUNK, w), lambda b, c: (b * nc + c, blk))
    full = lambda shape: pl.BlockSpec(shape, lambda b, c: (0,) * len(shape))
    vec = lambda x: x.astype(F32).reshape(1, -1)
    w2p = jnp.concatenate([w2, jnp.zeros((AAA_LORA, d), w2.dtype)], axis=0).astype(BF16)
    a2p = jnp.concatenate([jnp.zeros((DECAY_LORA, d), a2.dtype), a2], axis=0).astype(BF16)
    in_specs = [row_spec(d, 3), row_spec(d, 4), row_spec(d, 5), row_spec(2 * LANES, 24),
                full((1, d)), full((1, d)), full((1, d)), full((1, 2 * LANES)),
                full((1, d)), full((LANES, d)), full((1, d)), full((LANES, d)), full((LANES, d)),
                full((1, d)), full((1, d)), full((1, d)), full((1, d)), full((1, d))]
    operands = [proj, proj, proj, proj,
                vec(mu[:d]), vec(mu[d:2 * d]), vec(mu[2 * d:3 * d]), vec(mu[3 * d:]),
                vec(w0), w2p, vec(a0), a2p, g2.astype(BF16),
                vec(k_k), vec(k_a), vec(r_k), vec(ln_w), vec(ln_b)]
    scratch = [pltpu.VMEM((TAIL + CHUNK, d), F32), pltpu.VMEM((TAIL + CHUNK, d), F32),
               pltpu.VMEM((TAIL + CHUNK, d), F32), pltpu.VMEM((TAIL + CHUNK, 2 * LANES), F32),
               pltpu.VMEM((d // LANES, LANES, LANES), F32)]
    return in_specs, operands, scratch


def _mixers_kernel(*refs, n_ssd_in, n_rwkv_in, n_ssd_scratch):
    ssd_in = refs[:n_ssd_in]
    rwkv_in = refs[n_ssd_in:n_ssd_in + n_rwkv_in]
    o_ssd, o_rwkv = refs[n_ssd_in + n_rwkv_in:n_ssd_in + n_rwkv_in + 2]
    scratch = refs[n_ssd_in + n_rwkv_in + 2:]
    _rwkv_kernel(*rwkv_in, o_rwkv, *scratch[n_ssd_scratch:],
                 side_work=_ssd_stages(*ssd_in, o_ssd, *scratch[:n_ssd_scratch]))


def _mixers(proj, batch, seq, ssd_params, rwkv_params):
    nc = seq // CHUNK
    n = batch * seq
    s_specs, s_ops, s_scratch = _ssd_operands(proj, nc, *ssd_params)
    r_specs, r_ops, r_scratch = _rwkv_operands(proj, nc, *rwkv_params)
    out_spec = lambda w: pl.BlockSpec((CHUNK, w), lambda b, c: (b * nc + c, 0))
    return pl.pallas_call(
        functools.partial(_mixers_kernel, n_ssd_in=len(s_specs), n_rwkv_in=len(r_specs),
                          n_ssd_scratch=len(s_scratch)),
        grid=(batch, nc),
        in_specs=s_specs + r_specs,
        out_specs=[out_spec(SSD_INNER), out_spec(RWKV_DIM)],
        out_shape=[jax.ShapeDtypeStruct((n, SSD_INNER), BF16), jax.ShapeDtypeStruct((n, RWKV_DIM), BF16)],
        scratch_shapes=s_scratch + r_scratch,
        compiler_params=_cparams(("arbitrary", "arbitrary")),
        name="mixers",
    )(*s_ops, *r_ops)


def _route(logits):
    lane = _iota2(logits.shape, 1)
    lanef = lane.astype(F32)
    big = float(LANES)

    def first_max(x):
        m = jnp.max(x, axis=-1, keepdims=True)
        idx = jnp.min(jnp.where(x == m, lanef, big), axis=-1, keepdims=True)
        return m, idx

    cl = jnp.where(lane < EXPERT_GROUPS, logits, NEG_BIG)
    cmax, grp = first_max(cl)
    p_group = 1.0 / jnp.sum(jnp.exp(cl - cmax), axis=-1, keepdims=True)
    lo = EXPERT_GROUPS + grp * EXPERTS_PER_GROUP
    fl = jnp.where((lanef >= lo) & (lanef < lo + EXPERTS_PER_GROUP), logits, NEG_BIG)
    m0, i0 = first_max(fl)
    m1, i1 = first_max(jnp.where(lanef == i0, NEG_BIG, fl))
    e1 = jnp.exp(m1 - m0)
    g0 = p_group / (1.0 + e1)
    g1 = p_group * e1 / (1.0 + e1)
    return jnp.where(lane == 0, i0 - EXPERT_GROUPS,
                     jnp.where(lane == 1, i1 - EXPERT_GROUPS,
                               jnp.where(lane == 2, g0, jnp.where(lane == 3, g1, 0.0))))


def _outproj_router_kernel(*refs, n_in):
    ys = refs[:n_in]
    ws = refs[n_in:2 * n_in]
    h_ref, g_ref, wr_ref, br_ref, hnew_ref, hf_ref, route_ref = refs[2 * n_in:]
    acc = h_ref[...]
    for y_ref, w_ref in zip(ys, ws):
        acc = acc + jnp.dot(y_ref[...], w_ref[...], preferred_element_type=F32)
    hnew_ref[...] = acc
    hf = _rms(acc, g_ref[...])
    hf_ref[...] = hf
    route_ref[...] = _route(_dot_f32(hf, wr_ref[...]) + br_ref[...])


def _outproj_router(ys, ws, h, gain, w_router, b_router, tm):
    n, d = h.shape
    n_in = len(ys)
    in_specs = ([pl.BlockSpec((tm, y.shape[1]), lambda i: (i, 0)) for y in ys]
                + [pl.BlockSpec(w.shape, lambda i: (0, 0)) for w in ws]
                + [pl.BlockSpec((tm, d), lambda i: (i, 0)),
                   pl.BlockSpec((1, d), lambda i: (0, 0)),
                   pl.BlockSpec((d, LANES), lambda i: (0, 0)),
                   pl.BlockSpec((1, LANES), lambda i: (0, 0))])
    return pl.pallas_call(
        functools.partial(_outproj_router_kernel, n_in=n_in),
        grid=(n // tm,),
        in_specs=in_specs,
        out_specs=[pl.BlockSpec((tm, d), lambda i: (i, 0)),
                   pl.BlockSpec((tm, d), lambda i: (i, 0)),
                   pl.BlockSpec((tm, LANES), lambda i: (i, 0))],
        out_shape=[jax.ShapeDtypeStruct((n, d), F32), jax.ShapeDtypeStruct((n, d), F32),
                   jax.ShapeDtypeStruct((n, LANES), F32)],
        compiler_params=_cparams(("arbitrary",)),
        name="outproj_router",
    )(*ys, *ws, h, gain.reshape(1, d), w_router, b_router)


def _router_weights(w_coarse, b_coarse, w_fine, b_fine):
    d = w_coarse.shape[0]
    wf = jnp.transpose(w_fine, (1, 0, 2)).reshape(d, N_EXPERTS)
    w = jnp.concatenate([w_coarse, wf], axis=1).astype(F32)
    b = jnp.concatenate([b_coarse, b_fine.reshape(N_EXPERTS)]).astype(F32)
    pad = LANES - w.shape[1]
    return jnp.pad(w, ((0, 0), (0, pad))), jnp.pad(b, (0, pad)).reshape(1, LANES)


def _moe_plan(route, tm):
    n = route.shape[0]
    a = n * TOP_K
    n_blocks = a // MOE_BLOCK + N_EXPERTS
    e_flat = route[:, :TOP_K].astype(jnp.int32).reshape(a)
    seg = MOE_BLOCK
    onehot = (e_flat[:, None] == jnp.arange(N_EXPERTS, dtype=jnp.int32)[None, :]).astype(F32)
    onehot = onehot.reshape(a // seg, seg, N_EXPERTS)
    tri = jnp.tril(jnp.ones((seg, seg), F32))
    within = jnp.einsum("ij,bjk->bik", tri, onehot)
    tot = within[:, -1, :]
    offs = jnp.cumsum(tot, axis=0) - tot
    rank = (jnp.sum(onehot * (within + offs[:, None, :]), axis=-1) - 1.0).astype(jnp.int32).reshape(a)
    counts = (offs[-1] + tot[-1]).astype(jnp.int32)
    padded = (counts + MOE_BLOCK - 1) // MOE_BLOCK * MOE_BLOCK
    pad_end = jnp.cumsum(padded)
    pad_start = pad_end - padded
    start_of = jnp.sum(jnp.where(e_flat[:, None] == jnp.arange(N_EXPERTS, dtype=jnp.int32)[None, :],
                                 pad_start[None, :], 0), axis=1)
    dest = (start_of + rank).astype(jnp.int32)
    block_start = jnp.arange(n_blocks, dtype=jnp.int32) * MOE_BLOCK
    block_expert = jnp.minimum(jnp.sum((pad_end[None, :] <= block_start[:, None]).astype(jnp.int32), axis=1),
                               N_EXPERTS - 1).astype(jnp.int32)
    n_used = (pad_end[-1] // MOE_BLOCK).astype(jnp.int32).reshape(1)
    pad_lo = (pad_start + counts).astype(jnp.int32)
    return dict(dest3=dest.reshape(n // tm, 1, TOP_K * tm), block_expert=block_expert, n_used=n_used,
                pad_lo=pad_lo, pad_hi=pad_end.astype(jnp.int32), n_blocks=n_blocks)


def _dispatch_kernel(padlo_ref, padhi_ref, nused_ref, hf_ref, dest_ref, xs_hbm, zblk, sem, zsem):
    i = pl.program_id(0)
    tm = hf_ref.shape[0]

    @pl.when(i == 0)
    def _():
        zblk[...] = jnp.zeros_like(zblk)

        def row_copy(rw):
            return pltpu.make_async_copy(zblk.at[pl.ds(0, 1), :], xs_hbm.at[pl.ds(rw, 1), :], zsem)

        def group_copy(g):
            return pltpu.make_async_copy(zblk.at[pl.ds(0, TAIL), :],
                                         xs_hbm.at[pl.ds(pl.multiple_of(g * TAIL, TAIL), TAIL), :], zsem)

        def run(lo, hi, make, wait):
            def body(k, c2):
                cp = make(k)
                cp.wait() if wait else cp.start()
                return c2
            lax.fori_loop(lo, hi, body, 0)

        def per_expert(e, carry):
            lo, hi = padlo_ref[e], padhi_ref[e]
            mid = jnp.minimum((lo + TAIL - 1) // TAIL * TAIL, hi)
            for wait in (False, True):
                run(lo, mid, row_copy, wait)
                run(mid // TAIL, hi // TAIL, group_copy, wait)
            return carry
        lax.fori_loop(0, N_EXPERTS, per_expert, 0)

        n_blocks = xs_hbm.shape[0] // MOE_BLOCK

        def block_copy(b):
            return pltpu.make_async_copy(
                zblk, xs_hbm.at[pl.ds(pl.multiple_of(b * MOE_BLOCK, MOE_BLOCK), MOE_BLOCK), :], zsem)

        def zero_block(b, carry):
            block_copy(b).start()
            return carry
        lax.fori_loop(nused_ref[0], n_blocks, zero_block, 0)

        def wait_block(b, carry):
            block_copy(b).wait()
            return carry
        lax.fori_loop(nused_ref[0], n_blocks, wait_block, 0)

    for t in range(tm):
        for c in range(TOP_K):
            pltpu.make_async_copy(hf_ref.at[pl.ds(t, 1), :],
                                  xs_hbm.at[pl.ds(dest_ref[0, 0, TOP_K * t + c], 1), :], sem).start()
    for c in range(TOP_K):
        pltpu.make_async_copy(hf_ref, xs_hbm.at[pl.ds(0, tm), :], sem).wait()


def _dispatch(hf, plan, tm):
    n, d = hf.shape
    slots = plan["n_blocks"] * MOE_BLOCK
    grid_spec = pltpu.PrefetchScalarGridSpec(
        num_scalar_prefetch=3,
        grid=(n // tm,),
        in_specs=[pl.BlockSpec((tm, d), lambda i, lo, hi, nu: (i, 0)),
                  pl.BlockSpec((1, 1, TOP_K * tm), lambda i, lo, hi, nu: (i, 0, 0), memory_space=pltpu.SMEM)],
        out_specs=pl.BlockSpec(memory_space=pl.ANY),
        scratch_shapes=[pltpu.VMEM((MOE_BLOCK, d), F32), pltpu.SemaphoreType.DMA(()),
                        pltpu.SemaphoreType.DMA(())])
    return pl.pallas_call(
        _dispatch_kernel,
        grid_spec=grid_spec,
        out_shape=jax.ShapeDtypeStruct((slots, d), F32),
        compiler_params=_cparams(("arbitrary",)),
        name="moe_dispatch",
    )(plan["pad_lo"], plan["pad_hi"], plan["n_used"], hf, plan["dest3"])


def _experts_kernel(bexp_ref, nused_ref, x_ref, wg_ref, wu_ref, wd_ref, y_ref, wg_s, wu_s, wd_s):
    i = pl.program_id(0)
    active = i < nused_ref[0]

    @pl.when(active & ((i == 0) | (bexp_ref[i] != bexp_ref[jnp.maximum(i - 1, 0)])))
    def _():
        wg_s[...] = wg_ref[0].astype(BF16)
        wu_s[...] = wu_ref[0].astype(BF16)
        wd_s[...] = wd_ref[0].astype(BF16)

    @pl.when(active)
    def _():
        x = x_ref[...].astype(BF16)
        hg = jnp.dot(x, wg_s[...], preferred_element_type=F32)
        hu = jnp.dot(x, wu_s[...], preferred_element_type=F32)
        hb = (_silu(hg) * hu).astype(BF16)
        y_ref[...] = jnp.dot(hb, wd_s[...], preferred_element_type=F32)

    @pl.when(i >= nused_ref[0])
    def _():
        y_ref[...] = jnp.zeros_like(y_ref)


def _experts(xs, plan, layer, wg, wu, wd):
    slots, d = xs.shape
    n_blocks = plan["n_blocks"]
    used = lambda i, nu: jnp.minimum(i, nu[0] - 1)
    w_idx = lambda i, be, nu: (layer, be[used(i, nu)], 0, 0)
    grid_spec = pltpu.PrefetchScalarGridSpec(
        num_scalar_prefetch=2,
        grid=(n_blocks,),
        in_specs=[pl.BlockSpec((MOE_BLOCK, d), lambda i, be, nu: (used(i, nu), 0)),
                  pl.BlockSpec((None, 1, d, EXPERT_HIDDEN), w_idx),
                  pl.BlockSpec((None, 1, d, EXPERT_HIDDEN), w_idx),
                  pl.BlockSpec((None, 1, EXPERT_HIDDEN, d), w_idx)],
        out_specs=pl.BlockSpec((MOE_BLOCK, d), lambda i, be, nu: (i, 0)),
        scratch_shapes=[pltpu.VMEM((d, EXPERT_HIDDEN), BF16), pltpu.VMEM((d, EXPERT_HIDDEN), BF16),
                        pltpu.VMEM((EXPERT_HIDDEN, d), BF16)])
    return pl.pallas_call(
        _experts_kernel,
        grid_spec=grid_spec,
        out_shape=jax.ShapeDtypeStruct((slots, d), F32),
        compiler_params=_cparams(("arbitrary",)),
        name="moe_experts",
    )(plan["block_expert"], plan["n_used"], xs, wg, wu, wd)


def _swa_kernel(q_ref, kvp_ref, kvc_ref, qg_ref, kg_ref, slope_ref, sink_ref, o_ref):
    jb = pl.program_id(1)
    blk = CHUNK
    qi = _iota2((blk, blk), 0)
    kj = _iota2((blk, blk), 1)
    from_prev = kj > qi
    deltaf = jnp.where(from_prev, qi + blk - kj, qi - kj).astype(F32)
    no_prev = jnp.where(from_prev, jnp.where(jb > 0, 0.0, NEG_BIG), 0.0)
    scale = HEAD_DIM ** -0.5

    bd = (_iota2((LANES, LANES), 0) // HEAD_DIM) == (_iota2((LANES, LANES), 1) // HEAD_DIM)
    bd_ones = jnp.where(bd, 1.0, 0.0).astype(BF16)

    def head_rms(x, gain):
        sq = x * x
        s1 = sq.astype(BF16)
        s2 = (sq - s1.astype(F32)).astype(BF16)
        ms = (jnp.dot(s1, bd_ones, preferred_element_type=F32)
              + jnp.dot(s2, bd_ones, preferred_element_type=F32)) * (1.0 / HEAD_DIM)
        return x * lax.rsqrt(ms + NORM_EPS) * gain

    n_kv_blk = KV_DIM // LANES
    kv = jnp.concatenate([kvp_ref[...], kvc_ref[...]], axis=0)
    lane_kv = _iota2((kv.shape[0], LANES), 1)
    kv_half = (lane_kv < HEAD_DIM, lane_kv >= HEAD_DIM)
    lane_q = _iota2((blk, LANES), 1)
    q_half = (lane_q < HEAD_DIM, lane_q >= HEAD_DIM)
    kn = [head_rms(kv[:, j * LANES:(j + 1) * LANES], kg_ref[...]) for j in range(n_kv_blk)]
    vb = [kv[:, KV_DIM + j * LANES:KV_DIM + (j + 1) * LANES] for j in range(n_kv_blk)]
    kn_sw = [pltpu.roll(x, HEAD_DIM, 1) for x in kn]
    vb_sw = [pltpu.roll(x, HEAD_DIM, 1) for x in vb]
    v_same = [jnp.where(kv_half[g % 2], vb[g // 2], 0.0) for g in range(KV_HEADS)]
    v_swap = [jnp.where(kv_half[1 - g % 2], vb_sw[g // 2], 0.0) for g in range(KV_HEADS)]
    qn = [head_rms(q_ref[:, j * LANES:(j + 1) * LANES], qg_ref[...]) for j in range(Q_DIM // LANES)]

    rep = lambda x: jnp.concatenate([x] * Q_PER_KV, axis=0)
    from_prev4, delta4 = rep(from_prev), rep(deltaf)
    neg4 = [rep(no_prev)] + [None] * (SWA_QBLOCKS - 1)
    col = lambda ref, hs: jnp.concatenate([jnp.broadcast_to(ref[:, h:h + 1], (blk, 1)) for h in hs], axis=0)
    order = lambda g: [g * Q_PER_KV + g % 2, g * Q_PER_KV + g % 2 + 2,
                       g * Q_PER_KV + 1 - g % 2, g * Q_PER_KV + 3 - g % 2]
    units = [(u, g) for u in range(SWA_QBLOCKS) for g in range(KV_HEADS)]
    keys = lambda x, u: x[u * blk:(u + 2) * blk]
    qrow = lambda x, u: x[u * blk:(u + 1) * blk]

    qms = [[jnp.where(q_half[h % 2], qrow(qn[h // 2], u), 0.0) for h in order(g)] for u, g in units]
    scs = [jnp.concatenate([_dot_nt(jnp.concatenate(qm[:2], axis=0), keys(kn[g // 2], u)),
                            _dot_nt(jnp.concatenate(qm[2:], axis=0), keys(kn_sw[g // 2], u))], axis=0)
           for qm, (u, g) in zip(qms, units)]
    sinks = [col(sink_ref, order(g)) for u, g in units]
    ss = []
    for sc, (u, g) in zip(scs, units):
        s = jnp.where(from_prev4, sc[:, :blk], sc[:, blk:]) * scale - col(slope_ref, order(g)) * delta4
        ss.append(s if neg4[u] is None else s + neg4[u])
    ms = [jnp.maximum(jnp.max(s, axis=-1, keepdims=True), sk) for s, sk in zip(ss, sinks)]
    ps = [jnp.exp(s - m) for s, m in zip(ss, ms)]
    invs = [1.0 / (jnp.sum(p, axis=-1, keepdims=True) + jnp.exp(sk - m)) for p, sk, m in zip(ps, sinks, ms)]
    pcats = [jnp.concatenate([jnp.where(from_prev4, p, 0.0), jnp.where(from_prev4, 0.0, p)], axis=1) for p in ps]
    outs = [jnp.concatenate([_dot(pc[:2 * blk], keys(v_same[g], u)), _dot(pc[2 * blk:], keys(v_swap[g], u))],
                            axis=0) * inv
            for pc, inv, (u, g) in zip(pcats, invs, units)]
    for o, (u, g) in zip(outs, units):
        hs = order(g)
        for j in sorted({h // 2 for h in hs}):
            pair = sum(o[idx * blk:(idx + 1) * blk] for idx, h in enumerate(hs) if h // 2 == j)
            o_ref[u * blk:(u + 1) * blk, j * LANES:(j + 1) * LANES] = pair.astype(o_ref.dtype)


def _swa(proj, batch, seq, q_gain, k_gain, sinks):
    nbk = seq // CHUNK
    qrows = SWA_QBLOCKS * CHUNK
    nsteps = seq // qrows
    n = batch * seq
    slopes = (2.0 ** (-8.0 * jnp.arange(1, ATT_HEADS + 1, dtype=F32) / ATT_HEADS))
    pad = lambda v: jnp.pad(v.astype(F32), (0, LANES - v.shape[0])).reshape(1, LANES)
    pair = lambda v: jnp.tile(v.astype(F32), LANES // HEAD_DIM).reshape(1, LANES)
    full = lambda shape: pl.BlockSpec(shape, lambda b, j: (0, 0))
    kvw = 2 * KV_DIM
    return pl.pallas_call(
        _swa_kernel,
        grid=(batch, nsteps),
        in_specs=[pl.BlockSpec((qrows, Q_DIM), lambda b, j: (b * nsteps + j, 0)),
                  pl.BlockSpec((CHUNK, kvw),
                               lambda b, j: (b * nbk + jnp.maximum(SWA_QBLOCKS * j - 1, 0), Q_DIM // kvw)),
                  pl.BlockSpec((qrows, kvw), lambda b, j: (b * nsteps + j, Q_DIM // kvw)),
                  full((1, LANES)), full((1, LANES)), full((1, LANES)), full((1, LANES))],
        out_specs=pl.BlockSpec((qrows, Q_DIM), lambda b, j: (b * nsteps + j, 0)),
        out_shape=jax.ShapeDtypeStruct((n, Q_DIM), BF16),
        compiler_params=_cparams(("arbitrary", "arbitrary")),
        name="swa",
    )(proj, proj, proj, pair(q_gain), pair(k_gain), pad(slopes), pad(sinks))


def _even_in_weight(w):
    rw = SSD_COLS
    cols = jnp.concatenate([w[:, :SSD_INNER + SSD_CONV_DIM], w[:, rw:],
                            w[:, SSD_INNER + SSD_CONV_DIM:SSD_COLS]], axis=1)
    return jnp.pad(cols, ((0, 0), (0, EVEN_COLS_PAD - cols.shape[1]))).astype(BF16)


def kernel(x, ln_mix, ln_ffn, e_w_in, e_w_out, ssd_conv_w, ssd_conv_b, ssd_dt_bias, ssd_a_log, ssd_d, ssd_norm,
           rwkv_mu, rwkv_w0, rwkv_w2, rwkv_a0, rwkv_a2, rwkv_g2, rwkv_k_k, rwkv_k_a, rwkv_r_k, rwkv_ln_w,
           rwkv_ln_b, o_w_in, o_w_out, attn_q_norm, attn_k_norm, attn_sinks, moe_w_coarse, moe_b_coarse,
           moe_w_fine, moe_b_fine, moe_w_gate, moe_w_up, moe_w_down):
    batch, seq, d = x.shape
    n = batch * seq
    tm = min(512, n)
    tm_mm = min(1024, n)
    h = x.reshape(n, d)

    def moe(layer, hf, route):
        plan = _moe_plan(route, tm)
        xs = _dispatch(hf, plan, tm)
        ys = _experts(xs, plan, layer, moe_w_gate, moe_w_up, moe_w_down)
        return ys, plan["dest3"]

    def router_w(layer):
        return _router_weights(moe_w_coarse[layer], moe_b_coarse[layer], moe_w_fine[layer], moe_b_fine[layer])

    proj = _norm_proj(h, ln_mix[0], _even_in_weight(e_w_in[0]), tm_mm, EVEN_COLS_PAD // 3)
    y_ssd, y_rwkv = _mixers(
        proj, batch, seq,
        (ssd_conv_w[0], ssd_conv_b[0], ssd_dt_bias[0], ssd_a_log[0], ssd_d[0], ssd_norm[0]),
        (rwkv_mu[0], rwkv_w0[0], rwkv_w2[0], rwkv_a0[0], rwkv_a2[0], rwkv_g2[0], rwkv_k_k[0], rwkv_k_a[0],
         rwkv_r_k[0].reshape(-1), rwkv_ln_w[0], rwkv_ln_b[0]))
    w_out = e_w_out[0].astype(BF16)
    wr, br = router_w(0)
    h, hf, route = _outproj_router([y_ssd, y_rwkv], [w_out[:SSD_INNER], w_out[SSD_INNER:]], h, ln_ffn[0],
                                   wr, br, tm_mm)
    ys, dest3 = moe(0, hf, route)

    h, proj = _combine_norm_proj(h, ys, route, dest3, ln_mix[1], o_w_in[0].astype(BF16), tm)
    att = _swa(proj, batch, seq, attn_q_norm[0], attn_k_norm[0], attn_sinks[0])
    wr, br = router_w(1)
    h, hf, route = _outproj_router([att], [o_w_out[0].astype(BF16)], h, ln_ffn[1], wr, br, tm_mm)
    ys, dest3 = moe(1, hf, route)
    out = _combine(h, ys, route, dest3, tm)
    return out.reshape(batch, seq, d)
```

```python
import functools
import math

import jax
import jax.numpy as jnp
from jax import lax
from jax.experimental import pallas as pl
from jax.experimental.pallas import tpu as pltpu

F32 = jnp.float32
BF16 = jnp.bfloat16

D_MODEL = 1024
SSD_HEADS = 16
SSD_HEAD_DIM = 64
SSD_INNER = 1024
SSD_GROUPS = 4
SSD_STATE = 128
SSD_CONV = 4
SSD_CONV_DIM = 2048
SSD_COLS = 3088
RWKV_HEADS = 16
RWKV_HEAD_DIM = 64
RWKV_DIM = 1024
DECAY_LORA = 64
AAA_LORA = 64
GATE_LORA = 128
RWKV_GN_EPS = 64e-5
ATT_HEADS = 16
KV_HEADS = 4
Q_PER_KV = 4
HEAD_DIM = 64
Q_DIM = 1024
KV_DIM = 256
WINDOW = 128
EXPERT_GROUPS = 4
EXPERTS_PER_GROUP = 8
N_EXPERTS = 32
TOP_K = 2
EXPERT_HIDDEN = 512
MOE_BLOCK = 512
NORM_EPS = 1e-6

LANES = 128
CHUNK = 128
TAIL = 8
DMA_UNROLL = 8
SWA_QBLOCKS = 8
EVEN_COLS_PAD = 6528
VMEM_LIMIT = 56 * 1024 * 1024
NEG_BIG = -1e30
assert WINDOW == CHUNK


def _cparams(sem):
    return pltpu.CompilerParams(dimension_semantics=sem, vmem_limit_bytes=VMEM_LIMIT)


def _dot(a, b):
    return jnp.dot(a.astype(BF16), b.astype(BF16), preferred_element_type=F32)


def _dot_nt(a, b):
    return lax.dot_general(a.astype(BF16), b.astype(BF16), (((1,), (1,)), ((), ())),
                           preferred_element_type=F32)


def _dot_tn(a, b):
    return lax.dot_general(a.astype(BF16), b.astype(BF16), (((0,), (0,)), ((), ())),
                           preferred_element_type=F32)


def _split3(x):
    x1 = x.astype(BF16)
    r1 = x - x1.astype(F32)
    x2 = r1.astype(BF16)
    x3 = (r1 - x2.astype(F32)).astype(BF16)
    return x1, x2, x3


def _dot_exact_lhs(m_bf16, x):
    x1, x2, x3 = _split3(x)
    f = lambda p: jnp.dot(m_bf16, p, preferred_element_type=F32)
    return f(x1) + f(x2) + f(x3)


def _dot_exact_rhs(x, m_bf16):
    x1, x2, x3 = _split3(x)
    f = lambda p: jnp.dot(p, m_bf16, preferred_element_type=F32)
    return f(x1) + f(x2) + f(x3)


def _dot_f32(x, w):
    x1, x2, _ = _split3(x)
    w1, w2, _ = _split3(w)
    f = lambda p, q: jnp.dot(p, q, preferred_element_type=F32)
    return f(x1, w1) + (f(x1, w2) + f(x2, w1))


def _sigmoid(x):
    return 0.5 + 0.5 * jnp.tanh(0.5 * x)


def _silu(x):
    return x * _sigmoid(x)


def _softplus(x):
    return jnp.maximum(x, 0.0) + jnp.log(1.0 + jnp.exp(-jnp.abs(x)))


def _rms(x, gain):
    return x * lax.rsqrt(jnp.mean(x * x, axis=-1, keepdims=True) + NORM_EPS) * gain


def _iota2(shape, dim):
    return lax.broadcasted_iota(jnp.int32, shape, dim)


def _tri_incl_bf16(n):
    return jnp.where(_iota2((n, n), 0) >= _iota2((n, n), 1), 1.0, 0.0).astype(BF16)


def _norm_proj_kernel(h_ref, g_ref, w_ref, o_ref):
    o_ref[...] = _dot(_rms(h_ref[...], g_ref[...]), w_ref[...])


def _norm_proj(h, gain, w_bf16, tm, tn):
    n, d = h.shape
    c = w_bf16.shape[1]
    return pl.pallas_call(
        _norm_proj_kernel,
        grid=(c // tn, n // tm),
        in_specs=[pl.BlockSpec((tm, d), lambda j, i: (i, 0)),
                  pl.BlockSpec((1, d), lambda j, i: (0, 0)),
                  pl.BlockSpec((d, tn), lambda j, i: (0, j))],
        out_specs=pl.BlockSpec((tm, tn), lambda j, i: (i, j)),
        out_shape=jax.ShapeDtypeStruct((n, c), F32),
        compiler_params=_cparams(("arbitrary", "arbitrary")),
        name="norm_proj",
    )(h, gain.reshape(1, d), w_bf16)


def _gathered_combine(h_ref, route_ref, dest_ref, destn_ref, ys_hbm, yg, gsem):
    i = pl.program_id(0)
    nt = pl.num_programs(0)
    slot = lax.rem(i, 2)
    tm = h_ref.shape[0]

    def row_copy(idx_ref, buf_slot, t, c):
        return pltpu.make_async_copy(ys_hbm.at[pl.ds(idx_ref[0, 0, TOP_K * t + c], 1), :],
                                     yg.at[buf_slot, c, pl.ds(t, 1), :], gsem.at[buf_slot])

    def wait_tile(buf_slot):
        for c in range(TOP_K):
            pltpu.make_async_copy(ys_hbm.at[pl.ds(0, tm), :], yg.at[buf_slot, c], gsem.at[buf_slot]).wait()

    @pl.when(i == 0)
    def _():
        def body(g, carry):
            for u in range(DMA_UNROLL):
                for c in range(TOP_K):
                    row_copy(dest_ref, 0, g * DMA_UNROLL + u, c).start()
            return carry
        lax.fori_loop(0, tm // DMA_UNROLL, body, 0)

    wait_tile(slot)
    for t in range(tm):
        for c in range(TOP_K):
            row_copy(destn_ref, 1 - slot, t, c).start()
    r = route_ref[...]
    x = h_ref[...] + r[:, 2:3] * yg[slot, 0] + r[:, 3:4] * yg[slot, 1]

    def finalize():
        @pl.when(i == nt - 1)
        def _():
            wait_tile(1 - slot)
    return x, finalize


def _combine_norm_proj_kernel(h_ref, route_ref, dest_ref, destn_ref, g_ref, w_ref, ys_hbm, hnew_ref, o_ref,
                              yg, gsem):
    x, finalize = _gathered_combine(h_ref, route_ref, dest_ref, destn_ref, ys_hbm, yg, gsem)
    hnew_ref[...] = x
    o_ref[...] = _dot(_rms(x, g_ref[...]), w_ref[...])
    finalize()


def _combine_kernel(h_ref, route_ref, dest_ref, destn_ref, ys_hbm, o_ref, yg, gsem):
    x, finalize = _gathered_combine(h_ref, route_ref, dest_ref, destn_ref, ys_hbm, yg, gsem)
    o_ref[...] = x
    finalize()


def _combine_specs(n, d, tm):
    nt = n // tm
    smem = lambda fn: pl.BlockSpec((1, 1, TOP_K * tm), fn, memory_space=pltpu.SMEM)
    in_specs = [pl.BlockSpec((tm, d), lambda i: (i, 0)),
                pl.BlockSpec((tm, LANES), lambda i: (i, 0)),
                smem(lambda i: (i, 0, 0)),
                smem(lambda i: (jnp.minimum(i + 1, nt - 1), 0, 0))]
    scratch = [pltpu.VMEM((2, TOP_K, tm, d), F32), pltpu.SemaphoreType.DMA((2,))]
    return in_specs, scratch


def _combine_norm_proj(h, ys, route, dest3, gain, w_bf16, tm):
    n, d = h.shape
    c = w_bf16.shape[1]
    in_specs, scratch = _combine_specs(n, d, tm)
    return pl.pallas_call(
        _combine_norm_proj_kernel,
        grid=(n // tm,),
        in_specs=in_specs + [pl.BlockSpec((1, d), lambda i: (0, 0)),
                             pl.BlockSpec((d, c), lambda i: (0, 0)),
                             pl.BlockSpec(memory_space=pl.ANY)],
        out_specs=[pl.BlockSpec((tm, d), lambda i: (i, 0)),
                   pl.BlockSpec((tm, c), lambda i: (i, 0))],
        out_shape=[jax.ShapeDtypeStruct((n, d), F32), jax.ShapeDtypeStruct((n, c), F32)],
        scratch_shapes=scratch,
        compiler_params=_cparams(("arbitrary",)),
        name="combine_norm_proj",
    )(h, route, dest3, dest3, gain.reshape(1, d), w_bf16, ys)


def _combine(h, ys, route, dest3, tm):
    n, d = h.shape
    in_specs, scratch = _combine_specs(n, d, tm)
    return pl.pallas_call(
        _combine_kernel,
        grid=(n // tm,),
        in_specs=in_specs + [pl.BlockSpec(memory_space=pl.ANY)],
        out_specs=pl.BlockSpec((tm, d), lambda i: (i, 0)),
        out_shape=jax.ShapeDtypeStruct((n, d), F32),
        scratch_shapes=scratch,
        compiler_params=_cparams(("arbitrary",)),
        name="combine",
    )(h, route, dest3, dest3, ys)


def _shifted_taps(buf, u_ref, n_taps):
    buf[TAIL:TAIL + CHUNK, :] = u_ref[...]
    taps = [buf[TAIL - j:TAIL - j + CHUNK, :] for j in range(n_taps)]
    return taps


def _carry_tail(buf):
    buf[0:TAIL, :] = buf[CHUNK:CHUNK + TAIL, :]


def _ssd_stages(z_ref, x_ref, bc_ref, dt_ref, cwx_ref, cbx_ref, cwbc_ref, cbbc_ref, dtb_ref, alog_ref,
                dskip_ref, nw_ref, hexp_ref, o_ref, xbuf, bcbuf, state):
    st = {}
    gw = SSD_INNER // SSD_GROUPS

    def init():
        @pl.when(pl.program_id(1) == 0)
        def _():
            xbuf[0:TAIL, :] = jnp.zeros((TAIL, SSD_INNER), F32)
            bcbuf[0:TAIL, :] = jnp.zeros((TAIL, SSD_INNER), F32)
            state[...] = jnp.zeros_like(state)

    def conv(buf, u_ref, w_ref, b_ref):
        taps = _shifted_taps(buf, u_ref, SSD_CONV)
        acc = b_ref[...] + taps[0] * w_ref[3:4, :]
        for j in range(1, SSD_CONV):
            acc = acc + taps[j] * w_ref[3 - j:4 - j, :]
        _carry_tail(buf)
        return _silu(acc)

    def convs():
        st["xs"] = conv(xbuf, x_ref, cwx_ref, cbx_ref)
        st["bc"] = conv(bcbuf, bc_ref, cwbc_ref, cbbc_ref)

    def decays():
        lane = _iota2((CHUNK, LANES), 1)
        dt = _softplus(dt_ref[...] + dtb_ref[...])
        adt = jnp.where(lane < SSD_HEADS, -jnp.exp(alog_ref[...]) * dt, 0.0)
        cum = _dot_exact_lhs(_tri_incl_bf16(CHUNK), adt)
        hexp = hexp_ref[...]
        cum_full = _dot_exact_rhs(cum, hexp)
        tot_full = cum_full[CHUNK - 1:CHUNK, :]
        xd = st["xs"] * _dot_exact_rhs(dt, hexp)
        st.update(cum=cum, cum_t=cum.T, tot_full=tot_full, xd=xd, xds=xd * jnp.exp(tot_full - cum_full),
                  eac=jnp.exp(cum_full), lane_lo=lane < SSD_HEAD_DIM,
                  causal=_iota2((CHUNK, CHUNK), 0) >= _iota2((CHUNK, CHUNK), 1), y_parts=[])

    def group(g):
        bc, cum, cum_t, xd = st["bc"], st["cum"], st["cum_t"], st["xd"]
        bg = bc[:, g * SSD_STATE:(g + 1) * SSD_STATE]
        cg = bc[:, (SSD_GROUPS + g) * SSD_STATE:(SSD_GROUPS + g + 1) * SSD_STATE]
        cb = _dot_nt(cg, bg)
        s_prev = state[:, g * gw:(g + 1) * gw]
        y_off = _dot(cg, s_prev) * st["eac"][:, g * gw:(g + 1) * gw]
        s_new = _dot(bg.T, st["xds"][:, g * gw:(g + 1) * gw])
        state[:, g * gw:(g + 1) * gw] = jnp.exp(st["tot_full"][:, g * gw:(g + 1) * gw]) * s_prev + s_new
        for pr in range(2):
            lo = g * gw + pr * LANES
            xd_pair = xd[:, lo:lo + LANES]
            yd = jnp.zeros((CHUNK, LANES), F32)
            for k in range(2):
                h = (lo // SSD_HEAD_DIM) + k
                diff = cum[:, h:h + 1] - cum_t[h:h + 1, :]
                decay = jnp.exp(jnp.where(st["causal"], diff, NEG_BIG))
                keep = st["lane_lo"] if k == 0 else jnp.logical_not(st["lane_lo"])
                yd = yd + _dot(cb * decay, jnp.where(keep, xd_pair, 0.0))
            st["y_parts"].append(yd + y_off[:, pr * LANES:(pr + 1) * LANES])

    def finish():
        y = jnp.concatenate(st["y_parts"], axis=1) + dskip_ref[...] * st["xs"]
        y = y * _silu(z_ref[...])
        outs = []
        for g in range(SSD_GROUPS):
            yg = y[:, g * gw:(g + 1) * gw]
            outs.append(yg * lax.rsqrt(jnp.mean(yg * yg, axis=-1, keepdims=True) + 1e-5))
        o_ref[...] = (jnp.concatenate(outs, axis=1) * nw_ref[...]).astype(o_ref.dtype)

    return [init, convs, decays] + [functools.partial(group, g) for g in range(SSD_GROUPS)] + [finish]


def _head_expand(n_heads, head_dim):
    h = jnp.arange(LANES)[:, None]
    l = jnp.arange(n_heads * head_dim)[None, :]
    return (l // head_dim == h).astype(BF16)


def _ssd_operands(proj, nc, conv_w, conv_b, dt_bias, a_log, d_skip, norm_w):
    pad16 = lambda v: jnp.pad(v.astype(F32), (0, LANES - v.shape[0])).reshape(1, LANES)
    row_spec = lambda w, blk: pl.BlockSpec((CHUNK, w), lambda b, c: (b * nc + c, blk))
    full = lambda shape: pl.BlockSpec(shape, lambda b, c: (0, 0))
    in_specs = [row_spec(SSD_INNER, 0),
                row_spec(SSD_INNER, 1),
                row_spec(SSD_INNER, 2),
                row_spec(LANES, 50),
                full((SSD_CONV, SSD_INNER)), full((1, SSD_INNER)),
                full((SSD_CONV, SSD_INNER)), full((1, SSD_INNER)),
                full((1, LANES)), full((1, LANES)),
                full((1, SSD_INNER)), full((1, SSD_INNER)),
                full((LANES, SSD_INNER))]
    operands = [proj, proj, proj, proj,
                conv_w[:, :SSD_INNER], conv_b[:SSD_INNER].reshape(1, -1),
                conv_w[:, SSD_INNER:], conv_b[SSD_INNER:].reshape(1, -1),
                pad16(dt_bias), pad16(a_log),
                jnp.repeat(d_skip.astype(F32), SSD_HEAD_DIM).reshape(1, -1), norm_w.reshape(1, -1),
                _head_expand(SSD_HEADS, SSD_HEAD_DIM)]
    scratch = [pltpu.VMEM((TAIL + CHUNK, SSD_INNER), F32),
               pltpu.VMEM((TAIL + CHUNK, SSD_INNER), F32),
               pltpu.VMEM((SSD_STATE, SSD_INNER), F32)]
    return in_specs, operands, scratch


def _rwkv_kernel(r_ref, k_ref, v_ref, lo_ref, mur_ref, muk_ref, muv_ref, mulo_ref, w0_ref, w2_ref, a0_ref,
                 a2_ref, g2_ref, kk_ref, ka_ref, rk_ref, lnw_ref, lnb_ref, o_ref,
                 rbuf, kbuf, vbuf, lobuf, state, side_work=()):
    side = iter(side_work)
    run_side = lambda: next(side, lambda: None)()
    run_side()
    c = pl.program_id(1)

    @pl.when(c == 0)
    def _():
        rbuf[0:TAIL, :] = jnp.zeros((TAIL, RWKV_DIM), F32)
        kbuf[0:TAIL, :] = jnp.zeros((TAIL, RWKV_DIM), F32)
        vbuf[0:TAIL, :] = jnp.zeros((TAIL, RWKV_DIM), F32)
        lobuf[0:TAIL, :] = jnp.zeros((TAIL, 2 * LANES), F32)
        state[...] = jnp.zeros_like(state)

    def shift(buf, u_ref, mu_ref):
        cur, prev = _shifted_taps(buf, u_ref, 2)
        _carry_tail(buf)
        return cur + mu_ref[...] * (prev - cur)

    r = shift(rbuf, r_ref, mur_ref)
    k = shift(kbuf, k_ref, muk_ref)
    v = shift(vbuf, v_ref, muv_ref)
    lo = shift(lobuf, lo_ref, mulo_ref)
    wa = lo[:, :LANES]
    logw = -math.exp(-0.5) * _sigmoid(w0_ref[...] + _dot(jnp.tanh(wa), w2_ref[...]))
    a = _sigmoid(a0_ref[...] + _dot(wa, a2_ref[...]))
    g = _dot(_sigmoid(lo[:, LANES:]), g2_ref[...])

    lane = _iota2((CHUNK, LANES), 1)
    head0 = lane < RWKV_HEAD_DIM
    bd = (_iota2((LANES, LANES), 0) // RWKV_HEAD_DIM) == (_iota2((LANES, LANES), 1) // RWKV_HEAD_DIM)
    bd_ones = jnp.where(bd, 1.0, 0.0).astype(BF16)

    def head_sum(x):
        return jnp.dot(x.astype(BF16), bd_ones, preferred_element_type=F32)

    tri = _tri_incl_bf16(CHUNK)
    cum = _dot_exact_lhs(tri, logw)
    cume = cum - logw
    cmid = cum[CHUNK // 2 - 1:CHUNK // 2, :]
    cend = cum[CHUNK - 1:CHUNK, :]
    e_in_mid = jnp.exp(cum - cmid)
    e_ex_mid = jnp.exp(cume - cmid)
    e_mid_in = jnp.exp(cmid - cum)
    e_ex = jnp.exp(cume)
    e_in = jnp.exp(cum)
    e_end = jnp.exp(cend - cum)
    e_tot = jnp.exp(cend)

    row = _iota2((CHUNK, CHUNK), 0)
    col = _iota2((CHUNK, CHUNK), 1)
    strict = row > col
    incl = row >= col
    zeros = jnp.zeros((CHUNK, LANES), F32)

    n_blocks = RWKV_DIM // LANES
    blk = []
    for p in range(n_blocks):
        sl = slice(p * LANES, (p + 1) * LANES)
        rp, vp, ap = r[:, sl], v[:, sl], a[:, sl]
        kkp = k[:, sl] * kk_ref[:, sl]
        kkp = kkp * lax.rsqrt(jnp.maximum(head_sum(kkp * kkp), 1e-24))
        kp = k[:, sl] * (1.0 + (ap - 1.0) * ka_ref[:, sl])
        aap = -kkp
        bp = kkp * ap
        a_mid = aap * e_ex_mid[:, sl]
        r_mid = rp * e_in_mid[:, sl]
        bf = lambda x: x.astype(BF16)
        lhs = jnp.concatenate([bf(jnp.where(head0, a_mid, 0.0)), bf(jnp.where(head0, 0.0, a_mid)),
                               bf(jnp.where(head0, r_mid, 0.0)), bf(jnp.where(head0, 0.0, r_mid))], axis=0)
        rhs = jnp.concatenate([bf(bp * e_mid_in[:, sl]), bf(kp * e_mid_in[:, sl])], axis=0)
        a_abs = aap * e_ex[:, sl]
        blk.append(dict(sl=sl, rp=rp, vp=vp, kp=kp, r_abs=rp * e_in[:, sl],
                        a_abs=(bf(jnp.where(head0, a_abs, 0.0)), bf(jnp.where(head0, 0.0, a_abs))),
                        vm=(bf(jnp.where(head0, vp, 0.0)), bf(jnp.where(head0, 0.0, vp))),
                        b_end=bf(bp * e_end[:, sl]), k_end=bf(kp * e_end[:, sl]), prod=_dot_nt(lhs, rhs)))
    run_side()

    heads = []
    for p in range(n_blocks):
        prod = blk[p]["prod"]
        for hh in range(2):
            heads.append(dict(
                vm=blk[p]["vm"][hh], a_abs=blk[p]["a_abs"][hh],
                a_ab=jnp.where(strict, prod[hh * CHUNK:(hh + 1) * CHUNK, :CHUNK], 0.0),
                a_ak=jnp.where(strict, prod[hh * CHUNK:(hh + 1) * CHUNK, CHUNK:], 0.0).astype(BF16),
                m_rb=jnp.where(incl, prod[(2 + hh) * CHUNK:(3 + hh) * CHUNK, :CHUNK], 0.0).astype(BF16),
                m_rk=jnp.where(incl, prod[(2 + hh) * CHUNK:(3 + hh) * CHUNK, CHUNK:], 0.0).astype(BF16)))

    eye = jnp.where(row == col, 1.0, 0.0)
    n_levels = int(math.log2(CHUNK)) - 1
    ts = [eye + hd["a_ab"] for hd in heads]
    xs = [hd["a_ab"].astype(BF16) for hd in heads]
    avs = [_dot(hd["a_ak"], hd["vm"]).astype(BF16) for hd in heads]
    run_side()
    xs = [jnp.dot(x, x, preferred_element_type=F32).astype(BF16) for x in xs]
    for _ in range(n_levels - 1):
        run_side()
        zs = [jnp.dot(x, jnp.concatenate([x, t.astype(BF16)], axis=1), preferred_element_type=F32)
              for x, t in zip(xs, ts)]
        xs = [z[:, :CHUNK].astype(BF16) for z in zs]
        ts = [t + z[:, CHUNK:] for t, z in zip(ts, zs)]
    ts = [t + jnp.dot(x, t.astype(BF16), preferred_element_type=F32) for x, t in zip(xs, ts)]
    for _ in side:
        _()

    wmats = [_dot(t, jnp.concatenate([hd["a_abs"], av], axis=1)).astype(BF16)
             for t, hd, av in zip(ts, heads, avs)]
    zeros_bf = zeros.astype(BF16)
    outs = [_dot(jnp.concatenate([hd["m_rb"], hd["m_rk"]], axis=1),
                 jnp.concatenate([jnp.concatenate([wm[:, LANES:], wm[:, :LANES]], axis=1),
                                  jnp.concatenate([hd["vm"], zeros_bf], axis=1)], axis=0))
            for wm, hd in zip(wmats, heads)]

    zts = []
    for p in range(n_blocks):
        w0h, w1h = wmats[2 * p], wmats[2 * p + 1]
        ui = w0h[:, LANES:] + w1h[:, LANES:]
        a_eff = w0h[:, :LANES] + w1h[:, :LANES]
        zts.append(_dot_tn(jnp.concatenate([jnp.concatenate([ui, a_eff], axis=1),
                                            jnp.concatenate([blk[p]["vp"].astype(BF16), zeros_bf], axis=1)],
                                           axis=0),
                           jnp.concatenate([blk[p]["b_end"], blk[p]["k_end"]], axis=0)))

    ys = []
    for p in range(n_blocks):
        s0 = state[p]
        o0, o1 = outs[2 * p], outs[2 * p + 1]
        r_eff = blk[p]["r_abs"] + o0[:, LANES:] + o1[:, LANES:]
        ys.append(_dot_nt(r_eff, s0) + o0[:, :LANES] + o1[:, :LANES])
        h_intra = jnp.where(bd, zts[p][:LANES, :], 0.0)
        g_corr = jnp.where(bd, zts[p][LANES:, :], 0.0)
        state[p] = s0 * e_tot[:, blk[p]["sl"]] + _dot(s0, g_corr) + h_intra

    means = [head_sum(y) * (1.0 / RWKV_HEAD_DIM) for y in ys]
    devs = [y - m for y, m in zip(ys, means)]
    vars_ = [head_sum(dv * dv) * (1.0 / RWKV_HEAD_DIM) for dv in devs]
    for p in range(n_blocks):
        sl = blk[p]["sl"]
        yn = devs[p] * lax.rsqrt(vars_[p] + RWKV_GN_EPS) * lnw_ref[:, sl] + lnb_ref[:, sl]
        bonus = head_sum(blk[p]["rp"] * blk[p]["kp"] * rk_ref[:, sl])
        o_ref[:, sl] = ((yn + bonus * blk[p]["vp"]) * g[:, sl]).astype(o_ref.dtype)


def _rwkv_operands(proj, nc, mu, w0, w2, a0, a2, g2, k_k, k_a, r_k, ln_w, ln_b):
    d = RWKV_DIM
    row_spec = lambda w, blk: pl.BlockSpec((CHUNK, w), lambda b, c: (b * nc + c, blk))
    full = lambda shape: pl.BlockSpec(shape, lambda b, c: (0,) * len(shape))
    vec = lambda x: x.astype(F32).reshape(1, -1)
    w2p = jnp.concatenate([w2, jnp.zeros((AAA_LORA, d), w2.dtype)], axis=0).astype(BF16)
    a2p = jnp.concatenate([jnp.zeros((DECAY_LORA, d), a2.dtype), a2], axis=0).astype(BF16)
    in_specs = [row_spec(d, 3), row_spec(d, 4), row_spec(d, 5), row_spec(2 * LANES, 24),
                full((1, d)), full((1, d)), full((1, d)), full((1, 2 * LANES)),
                full((1, d)), full((LANES, d)), full((1, d)), full((LANES, d)), full((LANES, d)),
                full((1, d)), full((1, d)), full((1, d)), full((1, d)), full((1, d))]
    operands = [proj, proj, proj, proj,
                vec(mu[:d]), vec(mu[d:2 * d]), vec(mu[2 * d:3 * d]), vec(mu[3 * d:]),
                vec(w0), w2p, vec(a0), a2p, g2.astype(BF16),
                vec(k_k), vec(k_a), vec(r_k), vec(ln_w), vec(ln_b)]
    scratch = [pltpu.VMEM((TAIL + CHUNK, d), F32), pltpu.VMEM((TAIL + CHUNK, d), F32),
               pltpu.VMEM((TAIL + CHUNK, d), F32), pltpu.VMEM((TAIL + CHUNK, 2 * LANES), F32),
               pltpu.VMEM((d // LANES, LANES, LANES), F32)]
    return in_specs, operands, scratch


def _mixers_kernel(*refs, n_ssd_in, n_rwkv_in, n_ssd_scratch):
    ssd_in = refs[:n_ssd_in]
    rwkv_in = refs[n_ssd_in:n_ssd_in + n_rwkv_in]
    o_ssd, o_rwkv = refs[n_ssd_in + n_rwkv_in:n_ssd_in + n_rwkv_in + 2]
    scratch = refs[n_ssd_in + n_rwkv_in + 2:]
    _rwkv_kernel(*rwkv_in, o_rwkv, *scratch[n_ssd_scratch:],
                 side_work=_ssd_stages(*ssd_in, o_ssd, *scratch[:n_ssd_scratch]))


def _mixers(proj, batch, seq, ssd_params, rwkv_params):
    nc = seq // CHUNK
    n = batch * seq
    s_specs, s_ops, s_scratch = _ssd_operands(proj, nc, *ssd_params)
    r_specs, r_ops, r_scratch = _rwkv_operands(proj, nc, *rwkv_params)
    out_spec = lambda w: pl.BlockSpec((CHUNK, w), lambda b, c: (b * nc + c, 0))
    return pl.pallas_call(
        functools.partial(_mixers_kernel, n_ssd_in=len(s_specs), n_rwkv_in=len(r_specs),
                          n_ssd_scratch=len(s_scratch)),
        grid=(batch, nc),
        in_specs=s_specs + r_specs,
        out_specs=[out_spec(SSD_INNER), out_spec(RWKV_DIM)],
        out_shape=[jax.ShapeDtypeStruct((n, SSD_INNER), BF16), jax.ShapeDtypeStruct((n, RWKV_DIM), BF16)],
        scratch_shapes=s_scratch + r_scratch,
        compiler_params=_cparams(("arbitrary", "arbitrary")),
        name="mixers",
    )(*s_ops, *r_ops)


def _route(logits):
    lane = _iota2(logits.shape, 1)
    lanef = lane.astype(F32)
    big = float(LANES)

    def first_max(x):
        m = jnp.max(x, axis=-1, keepdims=True)
        idx = jnp.min(jnp.where(x == m, lanef, big), axis=-1, keepdims=True)
        return m, idx

    cl = jnp.where(lane < EXPERT_GROUPS, logits, NEG_BIG)
    cmax, grp = first_max(cl)
    p_group = 1.0 / jnp.sum(jnp.exp(cl - cmax), axis=-1, keepdims=True)
    lo = EXPERT_GROUPS + grp * EXPERTS_PER_GROUP
    fl = jnp.where((lanef >= lo) & (lanef < lo + EXPERTS_PER_GROUP), logits, NEG_BIG)
    m0, i0 = first_max(fl)
    m1, i1 = first_max(jnp.where(lanef == i0, NEG_BIG, fl))
    e1 = jnp.exp(m1 - m0)
    g0 = p_group / (1.0 + e1)
    g1 = p_group * e1 / (1.0 + e1)
    return jnp.where(lane == 0, i0 - EXPERT_GROUPS,
                     jnp.where(lane == 1, i1 - EXPERT_GROUPS,
                               jnp.where(lane == 2, g0, jnp.where(lane == 3, g1, 0.0))))


def _outproj_router_kernel(*refs, n_in):
    ys = refs[:n_in]
    ws = refs[n_in:2 * n_in]
    h_ref, g_ref, wr_ref, br_ref, hnew_ref, hf_ref, route_ref = refs[2 * n_in:]
    acc = h_ref[...]
    for y_ref, w_ref in zip(ys, ws):
        acc = acc + jnp.dot(y_ref[...], w_ref[...], preferred_element_type=F32)
    hnew_ref[...] = acc
    hf = _rms(acc, g_ref[...])
    hf_ref[...] = hf
    route_ref[...] = _route(_dot_f32(hf, wr_ref[...]) + br_ref[...])


def _outproj_router(ys, ws, h, gain, w_router, b_router, tm):
    n, d = h.shape
    n_in = len(ys)
    in_specs = ([pl.BlockSpec((tm, y.shape[1]), lambda i: (i, 0)) for y in ys]
                + [pl.BlockSpec(w.shape, lambda i: (0, 0)) for w in ws]
                + [pl.BlockSpec((tm, d), lambda i: (i, 0)),
                   pl.BlockSpec((1, d), lambda i: (0, 0)),
                   pl.BlockSpec((d, LANES), lambda i: (0, 0)),
                   pl.BlockSpec((1, LANES), lambda i: (0, 0))])
    return pl.pallas_call(
        functools.partial(_outproj_router_kernel, n_in=n_in),
        grid=(n // tm,),
        in_specs=in_specs,
        out_specs=[pl.BlockSpec((tm, d), lambda i: (i, 0)),
                   pl.BlockSpec((tm, d), lambda i: (i, 0)),
                   pl.BlockSpec((tm, LANES), lambda i: (i, 0))],
        out_shape=[jax.ShapeDtypeStruct((n, d), F32), jax.ShapeDtypeStruct((n, d), F32),
                   jax.ShapeDtypeStruct((n, LANES), F32)],
        compiler_params=_cparams(("arbitrary",)),
        name="outproj_router",
    )(*ys, *ws, h, gain.reshape(1, d), w_router, b_router)


def _router_weights(w_coarse, b_coarse, w_fine, b_fine):
    d = w_coarse.shape[0]
    wf = jnp.transpose(w_fine, (1, 0, 2)).reshape(d, N_EXPERTS)
    w = jnp.concatenate([w_coarse, wf], axis=1).astype(F32)
    b = jnp.concatenate([b_coarse, b_fine.reshape(N_EXPERTS)]).astype(F32)
    pad = LANES - w.shape[1]
    return jnp.pad(w, ((0, 0), (0, pad))), jnp.pad(b, (0, pad)).reshape(1, LANES)


def _moe_plan(route, tm):
    n = route.shape[0]
    a = n * TOP_K
    n_blocks = a // MOE_BLOCK + N_EXPERTS
    e_flat = route[:, :TOP_K].astype(jnp.int32).reshape(a)
    seg = MOE_BLOCK
    onehot = (e_flat[:, None] == jnp.arange(N_EXPERTS, dtype=jnp.int32)[None, :]).astype(F32)
    onehot = onehot.reshape(a // seg, seg, N_EXPERTS)
    tri = jnp.tril(jnp.ones((seg, seg), F32))
    within = jnp.einsum("ij,bjk->bik", tri, onehot)
    tot = within[:, -1, :]
    offs = jnp.cumsum(tot, axis=0) - tot
    rank = (jnp.sum(onehot * (within + offs[:, None, :]), axis=-1) - 1.0).astype(jnp.int32).reshape(a)
    counts = (offs[-1] + tot[-1]).astype(jnp.int32)
    padded = (counts + MOE_BLOCK - 1) // MOE_BLOCK * MOE_BLOCK
    pad_end = jnp.cumsum(padded)
    pad_start = pad_end - padded
    start_of = jnp.sum(jnp.where(e_flat[:, None] == jnp.arange(N_EXPERTS, dtype=jnp.int32)[None, :],
                                 pad_start[None, :], 0), axis=1)
    dest = (start_of + rank).astype(jnp.int32)
    block_start = jnp.arange(n_blocks, dtype=jnp.int32) * MOE_BLOCK
    block_expert = jnp.minimum(jnp.sum((pad_end[None, :] <= block_start[:, None]).astype(jnp.int32), axis=1),
                               N_EXPERTS - 1).astype(jnp.int32)
    n_used = (pad_end[-1] // MOE_BLOCK).astype(jnp.int32).reshape(1)
    pad_lo = (pad_start + counts).astype(jnp.int32)
    return dict(dest3=dest.reshape(n // tm, 1, TOP_K * tm), block_expert=block_expert, n_used=n_used,
                pad_lo=pad_lo, pad_hi=pad_end.astype(jnp.int32), n_blocks=n_blocks)


def _dispatch_kernel(padlo_ref, padhi_ref, nused_ref, hf_ref, dest_ref, xs_hbm, zblk, sem, zsem):
    i = pl.program_id(0)
    tm = hf_ref.shape[0]

    @pl.when(i == 0)
    def _():
        zblk[...] = jnp.zeros_like(zblk)

        def row_copy(rw):
            return pltpu.make_async_copy(zblk.at[pl.ds(0, 1), :], xs_hbm.at[pl.ds(rw, 1), :], zsem)

        def group_copy(g):
            return pltpu.make_async_copy(zblk.at[pl.ds(0, TAIL), :],
                                         xs_hbm.at[pl.ds(pl.multiple_of(g * TAIL, TAIL), TAIL), :], zsem)

        def run(lo, hi, make, wait):
            def body(k, c2):
                cp = make(k)
                cp.wait() if wait else cp.start()
                return c2
            lax.fori_loop(lo, hi, body, 0)

        def per_expert(e, carry):
            lo, hi = padlo_ref[e], padhi_ref[e]
            mid = jnp.minimum((lo + TAIL - 1) // TAIL * TAIL, hi)
            for wait in (False, True):
                run(lo, mid, row_copy, wait)
                run(mid // TAIL, hi // TAIL, group_copy, wait)
            return carry
        lax.fori_loop(0, N_EXPERTS, per_expert, 0)

        n_blocks = xs_hbm.shape[0] // MOE_BLOCK

        def block_copy(b):
            return pltpu.make_async_copy(
                zblk, xs_hbm.at[pl.ds(pl.multiple_of(b * MOE_BLOCK, MOE_BLOCK), MOE_BLOCK), :], zsem)

        def zero_block(b, carry):
            block_copy(b).start()
            return carry
        lax.fori_loop(nused_ref[0], n_blocks, zero_block, 0)

        def wait_block(b, carry):
            block_copy(b).wait()
            return carry
        lax.fori_loop(nused_ref[0], n_blocks, wait_block, 0)

    for t in range(tm):
        for c in range(TOP_K):
            pltpu.make_async_copy(hf_ref.at[pl.ds(t, 1), :],
                                  xs_hbm.at[pl.ds(dest_ref[0, 0, TOP_K * t + c], 1), :], sem).start()
    for c in range(TOP_K):
        pltpu.make_async_copy(hf_ref, xs_hbm.at[pl.ds(0, tm), :], sem).wait()


def _dispatch(hf, plan, tm):
    n, d = hf.shape
    slots = plan["n_blocks"] * MOE_BLOCK
    grid_spec = pltpu.PrefetchScalarGridSpec(
        num_scalar_prefetch=3,
        grid=(n // tm,),
        in_specs=[pl.BlockSpec((tm, d), lambda i, lo, hi, nu: (i, 0)),
                  pl.BlockSpec((1, 1, TOP_K * tm), lambda i, lo, hi, nu: (i, 0, 0), memory_space=pltpu.SMEM)],
        out_specs=pl.BlockSpec(memory_space=pl.ANY),
        scratch_shapes=[pltpu.VMEM((MOE_BLOCK, d), F32), pltpu.SemaphoreType.DMA(()),
                        pltpu.SemaphoreType.DMA(())])
    return pl.pallas_call(
        _dispatch_kernel,
        grid_spec=grid_spec,
        out_shape=jax.ShapeDtypeStruct((slots, d), F32),
        compiler_params=_cparams(("arbitrary",)),
        name="moe_dispatch",
    )(plan["pad_lo"], plan["pad_hi"], plan["n_used"], hf, plan["dest3"])


def _experts_kernel(bexp_ref, nused_ref, x_ref, wg_ref, wu_ref, wd_ref, y_ref, wg_s, wu_s, wd_s):
    i = pl.program_id(0)
    active = i < nused_ref[0]

    @pl.when(active & ((i == 0) | (bexp_ref[i] != bexp_ref[jnp.maximum(i - 1, 0)])))
    def _():
        wg_s[...] = wg_ref[0].astype(BF16)
        wu_s[...] = wu_ref[0].astype(BF16)
        wd_s[...] = wd_ref[0].astype(BF16)

    @pl.when(active)
    def _():
        x = x_ref[...].astype(BF16)
        hg = jnp.dot(x, wg_s[...], preferred_element_type=F32)
        hu = jnp.dot(x, wu_s[...], preferred_element_type=F32)
        hb = (_silu(hg) * hu).astype(BF16)
        y_ref[...] = jnp.dot(hb, wd_s[...], preferred_element_type=F32)

    @pl.when(i >= nused_ref[0])
    def _():
        y_ref[...] = jnp.zeros_like(y_ref)


def _experts(xs, plan, layer, wg, wu, wd):
    slots, d = xs.shape
    n_blocks = plan["n_blocks"]
    used = lambda i, nu: jnp.minimum(i, nu[0] - 1)
    w_idx = lambda i, be, nu: (layer, be[used(i, nu)], 0, 0)
    grid_spec = pltpu.PrefetchScalarGridSpec(
        num_scalar_prefetch=2,
        grid=(n_blocks,),
        in_specs=[pl.BlockSpec((MOE_BLOCK, d), lambda i, be, nu: (used(i, nu), 0)),
                  pl.BlockSpec((None, 1, d, EXPERT_HIDDEN), w_idx),
                  pl.BlockSpec((None, 1, d, EXPERT_HIDDEN), w_idx),
                  pl.BlockSpec((None, 1, EXPERT_HIDDEN, d), w_idx)],
        out_specs=pl.BlockSpec((MOE_BLOCK, d), lambda i, be, nu: (i, 0)),
        scratch_shapes=[pltpu.VMEM((d, EXPERT_HIDDEN), BF16), pltpu.VMEM((d, EXPERT_HIDDEN), BF16),
                        pltpu.VMEM((EXPERT_HIDDEN, d), BF16)])
    return pl.pallas_call(
        _experts_kernel,
        grid_spec=grid_spec,
        out_shape=jax.ShapeDtypeStruct((slots, d), F32),
        compiler_params=_cparams(("arbitrary",)),
        name="moe_experts",
    )(plan["block_expert"], plan["n_used"], xs, wg, wu, wd)


def _swa_kernel(q_ref, kvp_ref, kvc_ref, qg_ref, kg_ref, slope_ref, sink_ref, o_ref):
    jb = pl.program_id(1)
    blk = CHUNK
    qi = _iota2((blk, blk), 0)
    kj = _iota2((blk, blk), 1)
    from_prev = kj > qi
    deltaf = jnp.where(from_prev, qi + blk - kj, qi - kj).astype(F32)
    no_prev = jnp.where(from_prev, jnp.where(jb > 0, 0.0, NEG_BIG), 0.0)
    scale = HEAD_DIM ** -0.5

    bd = (_iota2((LANES, LANES), 0) // HEAD_DIM) == (_iota2((LANES, LANES), 1) // HEAD_DIM)
    bd_ones = jnp.where(bd, 1.0, 0.0).astype(BF16)

    def head_rms(x, gain):
        sq = x * x
        s1 = sq.astype(BF16)
        s2 = (sq - s1.astype(F32)).astype(BF16)
        ms = (jnp.dot(s1, bd_ones, preferred_element_type=F32)
              + jnp.dot(s2, bd_ones, preferred_element_type=F32)) * (1.0 / HEAD_DIM)
        return x * lax.rsqrt(ms + NORM_EPS) * gain

    n_kv_blk = KV_DIM // LANES
    kv = jnp.concatenate([kvp_ref[...], kvc_ref[...]], axis=0)
    lane_kv = _iota2((kv.shape[0], LANES), 1)
    kv_half = (lane_kv < HEAD_DIM, lane_kv >= HEAD_DIM)
    lane_q = _iota2((blk, LANES), 1)
    q_half = (lane_q < HEAD_DIM, lane_q >= HEAD_DIM)
    kn = [head_rms(kv[:, j * LANES:(j + 1) * LANES], kg_ref[...]) for j in range(n_kv_blk)]
    vb = [kv[:, KV_DIM + j * LANES:KV_DIM + (j + 1) * LANES] for j in range(n_kv_blk)]
    kn_sw = [pltpu.roll(x, HEAD_DIM, 1) for x in kn]
    vb_sw = [pltpu.roll(x, HEAD_DIM, 1) for x in vb]
    v_same = [jnp.where(kv_half[g % 2], vb[g // 2], 0.0) for g in range(KV_HEADS)]
    v_swap = [jnp.where(kv_half[1 - g % 2], vb_sw[g // 2], 0.0) for g in range(KV_HEADS)]
    qn = [head_rms(q_ref[:, j * LANES:(j + 1) * LANES], qg_ref[...]) for j in range(Q_DIM // LANES)]

    rep = lambda x: jnp.concatenate([x] * Q_PER_KV, axis=0)
    from_prev4, delta4 = rep(from_prev), rep(deltaf)
    neg4 = [rep(no_prev)] + [None] * (SWA_QBLOCKS - 1)
    col = lambda ref, hs: jnp.concatenate([jnp.broadcast_to(ref[:, h:h + 1], (blk, 1)) for h in hs], axis=0)
    order = lambda g: [g * Q_PER_KV + g % 2, g * Q_PER_KV + g % 2 + 2,
                       g * Q_PER_KV + 1 - g % 2, g * Q_PER_KV + 3 - g % 2]
    units = [(u, g) for u in range(SWA_QBLOCKS) for g in range(KV_HEADS)]
    keys = lambda x, u: x[u * blk:(u + 2) * blk]
    qrow = lambda x, u: x[u * blk:(u + 1) * blk]

    qms = [[jnp.where(q_half[h % 2], qrow(qn[h // 2], u), 0.0) for h in order(g)] for u, g in units]
    scs = [jnp.concatenate([_dot_nt(jnp.concatenate(qm[:2], axis=0), keys(kn[g // 2], u)),
                            _dot_nt(jnp.concatenate(qm[2:], axis=0), keys(kn_sw[g // 2], u))], axis=0)
           for qm, (u, g) in zip(qms, units)]
    sinks = [col(sink_ref, order(g)) for u, g in units]
    ss = []
    for sc, (u, g) in zip(scs, units):
        s = jnp.where(from_prev4, sc[:, :blk], sc[:, blk:]) * scale - col(slope_ref, order(g)) * delta4
        ss.append(s if neg4[u] is None else s + neg4[u])
    ms = [jnp.maximum(jnp.max(s, axis=-1, keepdims=True), sk) for s, sk in zip(ss, sinks)]
    ps = [jnp.exp(s - m) for s, m in zip(ss, ms)]
    ones = jnp.ones((blk, LANES), BF16)
    invs = [1.0 / (_dot(p, ones) + jnp.exp(sk - m)) for p, sk, m in zip(ps, sinks, ms)]
    pcats = [jnp.concatenate([jnp.where(from_prev4, p, 0.0), jnp.where(from_prev4, 0.0, p)], axis=1) for p in ps]
    outs = [jnp.concatenate([_dot(pc[:2 * blk], keys(v_same[g], u)), _dot(pc[2 * blk:], keys(v_swap[g], u))],
                            axis=0) * inv
            for pc, inv, (u, g) in zip(pcats, invs, units)]
    for o, (u, g) in zip(outs, units):
        hs = order(g)
        for j in sorted({h // 2 for h in hs}):
            pair = sum(o[idx * blk:(idx + 1) * blk] for idx, h in enumerate(hs) if h // 2 == j)
            o_ref[u * blk:(u + 1) * blk, j * LANES:(j + 1) * LANES] = pair.astype(o_ref.dtype)


def _swa(proj, batch, seq, q_gain, k_gain, sinks):
    nbk = seq // CHUNK
    qrows = SWA_QBLOCKS * CHUNK
    nsteps = seq // qrows
    n = batch * seq
    slopes = (2.0 ** (-8.0 * jnp.arange(1, ATT_HEADS + 1, dtype=F32) / ATT_HEADS))
    pad = lambda v: jnp.pad(v.astype(F32), (0, LANES - v.shape[0])).reshape(1, LANES)
    pair = lambda v: jnp.tile(v.astype(F32), LANES // HEAD_DIM).reshape(1, LANES)
    full = lambda shape: pl.BlockSpec(shape, lambda b, j: (0, 0))
    kvw = 2 * KV_DIM
    return pl.pallas_call(
        _swa_kernel,
        grid=(batch, nsteps),
        in_specs=[pl.BlockSpec((qrows, Q_DIM), lambda b, j: (b * nsteps + j, 0)),
                  pl.BlockSpec((CHUNK, kvw),
                               lambda b, j: (b * nbk + jnp.maximum(SWA_QBLOCKS * j - 1, 0), Q_DIM // kvw)),
                  pl.BlockSpec((qrows, kvw), lambda b, j: (b * nsteps + j, Q_DIM // kvw)),
                  full((1, LANES)), full((1, LANES)), full((1, LANES)), full((1, LANES))],
        out_specs=pl.BlockSpec((qrows, Q_DIM), lambda b, j: (b * nsteps + j, 0)),
        out_shape=jax.ShapeDtypeStruct((n, Q_DIM), BF16),
        compiler_params=_cparams(("arbitrary", "arbitrary")),
        name="swa",
    )(proj, proj, proj, pair(q_gain), pair(k_gain), pad(slopes), pad(sinks))


def _even_in_weight(w):
    rw = SSD_COLS
    cols = jnp.concatenate([w[:, :SSD_INNER + SSD_CONV_DIM], w[:, rw:],
                            w[:, SSD_INNER + SSD_CONV_DIM:SSD_COLS]], axis=1)
    return jnp.pad(cols, ((0, 0), (0, EVEN_COLS_PAD - cols.shape[1]))).astype(BF16)


def kernel(x, ln_mix, ln_ffn, e_w_in, e_w_out, ssd_conv_w, ssd_conv_b, ssd_dt_bias, ssd_a_log, ssd_d, ssd_norm,
           rwkv_mu, rwkv_w0, rwkv_w2, rwkv_a0, rwkv_a2, rwkv_g2, rwkv_k_k, rwkv_k_a, rwkv_r_k, rwkv_ln_w,
           rwkv_ln_b, o_w_in, o_w_out, attn_q_norm, attn_k_norm, attn_sinks, moe_w_coarse, moe_b_coarse,
           moe_w_fine, moe_b_fine, moe_w_gate, moe_w_up, moe_w_down):
    batch, seq, d = x.shape
    n = batch * seq
    tm = min(512, n)
    tm_mm = min(1024, n)
    h = x.reshape(n, d)

    def moe(layer, hf, route):
        plan = _moe_plan(route, tm)
        xs = _dispatch(hf, plan, tm)
        ys = _experts(xs, plan, layer, moe_w_gate, moe_w_up, moe_w_down)
        return ys, plan["dest3"]

    def router_w(layer):
        return _router_weights(moe_w_coarse[layer], moe_b_coarse[layer], moe_w_fine[layer], moe_b_fine[layer])

    proj = _norm_proj(h, ln_mix[0], _even_in_weight(e_w_in[0]), tm_mm, EVEN_COLS_PAD // 3)
    y_ssd, y_rwkv = _mixers(
        proj, batch, seq,
        (ssd_conv_w[0], ssd_conv_b[0], ssd_dt_bias[0], ssd_a_log[0], ssd_d[0], ssd_norm[0]),
        (rwkv_mu[0], rwkv_w0[0], rwkv_w2[0], rwkv_a0[0], rwkv_a2[0], rwkv_g2[0], rwkv_k_k[0], rwkv_k_a[0],
         rwkv_r_k[0].reshape(-1), rwkv_ln_w[0], rwkv_ln_b[0]))
    w_out = e_w_out[0].astype(BF16)
    wr, br = router_w(0)
    h, hf, route = _outproj_router([y_ssd, y_rwkv], [w_out[:SSD_INNER], w_out[SSD_INNER:]], h, ln_ffn[0],
                                   wr, br, tm_mm)
    ys, dest3 = moe(0, hf, route)

    h, proj = _combine_norm_proj(h, ys, route, dest3, ln_mix[1], o_w_in[0].astype(BF16), tm)
    att = _swa(proj, batch, seq, attn_q_norm[0], attn_k_norm[0], attn_sinks[0])
    wr, br = router_w(1)
    h, hf, route = _outproj_router([att], [o_w_out[0].astype(BF16)], h, ln_ffn[1], wr, br, tm_mm)
    ys, dest3 = moe(1, hf, route)
    out = _combine(h, ys, route, dest3, tm)
    return out.reshape(batch, seq, d)
```

```python
import functools
import math

import jax
import jax.numpy as jnp
from jax import lax
from jax.experimental import pallas as pl
from jax.experimental.pallas import tpu as pltpu

F32 = jnp.float32
BF16 = jnp.bfloat16

D_MODEL = 1024
SSD_HEADS = 16
SSD_HEAD_DIM = 64
SSD_INNER = 1024
SSD_GROUPS = 4
SSD_STATE = 128
SSD_CONV = 4
SSD_CONV_DIM = 2048
SSD_COLS = 3088
RWKV_HEADS = 16
RWKV_HEAD_DIM = 64
RWKV_DIM = 1024
DECAY_LORA = 64
AAA_LORA = 64
GATE_LORA = 128
RWKV_GN_EPS = 64e-5
ATT_HEADS = 16
KV_HEADS = 4
Q_PER_KV = 4
HEAD_DIM = 64
Q_DIM = 1024
KV_DIM = 256
WINDOW = 128
EXPERT_GROUPS = 4
EXPERTS_PER_GROUP = 8
N_EXPERTS = 32
TOP_K = 2
EXPERT_HIDDEN = 512
MOE_BLOCK = 512
NORM_EPS = 1e-6

LANES = 128
CHUNK = 128
TAIL = 8
DMA_UNROLL = 8
SWA_QBLOCKS = 8
EVEN_COLS_PAD = 6528
VMEM_LIMIT = 56 * 1024 * 1024
NEG_BIG = -1e30
assert WINDOW == CHUNK


def _cparams(sem):
    return pltpu.CompilerParams(dimension_semantics=sem, vmem_limit_bytes=VMEM_LIMIT)


def _dot(a, b):
    return jnp.dot(a.astype(BF16), b.astype(BF16), preferred_element_type=F32)


def _dot_nt(a, b):
    return lax.dot_general(a.astype(BF16), b.astype(BF16), (((1,), (1,)), ((), ())),
                           preferred_element_type=F32)


def _dot_tn(a, b):
    return lax.dot_general(a.astype(BF16), b.astype(BF16), (((0,), (0,)), ((), ())),
                           preferred_element_type=F32)


def _split3(x):
    x1 = x.astype(BF16)
    r1 = x - x1.astype(F32)
    x2 = r1.astype(BF16)
    x3 = (r1 - x2.astype(F32)).astype(BF16)
    return x1, x2, x3


def _dot_exact_lhs(m_bf16, x):
    x1, x2, x3 = _split3(x)
    f = lambda p: jnp.dot(m_bf16, p, preferred_element_type=F32)
    return f(x1) + f(x2) + f(x3)


def _dot_exact_rhs(x, m_bf16):
    x1, x2, x3 = _split3(x)
    f = lambda p: jnp.dot(p, m_bf16, preferred_element_type=F32)
    return f(x1) + f(x2) + f(x3)


def _dot_f32(x, w):
    x1, x2, _ = _split3(x)
    w1, w2, _ = _split3(w)
    f = lambda p, q: jnp.dot(p, q, preferred_element_type=F32)
    return f(x1, w1) + (f(x1, w2) + f(x2, w1))


def _sigmoid(x):
    return 0.5 + 0.5 * jnp.tanh(0.5 * x)


def _silu(x):
    return x * _sigmoid(x)


def _softplus(x):
    return jnp.maximum(x, 0.0) + jnp.log(1.0 + jnp.exp(-jnp.abs(x)))


def _rms(x, gain):
    return x * lax.rsqrt(jnp.mean(x * x, axis=-1, keepdims=True) + NORM_EPS) * gain


def _iota2(shape, dim):
    return lax.broadcasted_iota(jnp.int32, shape, dim)


def _tri_incl_bf16(n):
    return jnp.where(_iota2((n, n), 0) >= _iota2((n, n), 1), 1.0, 0.0).astype(BF16)


def _norm_proj_kernel(h_ref, g_ref, w_ref, o_ref):
    o_ref[...] = _dot(_rms(h_ref[...], g_ref[...]), w_ref[...])


def _norm_proj(h, gain, w_bf16, tm, tn):
    n, d = h.shape
    c = w_bf16.shape[1]
    return pl.pallas_call(
        _norm_proj_kernel,
        grid=(c // tn, n // tm),
        in_specs=[pl.BlockSpec((tm, d), lambda j, i: (i, 0)),
                  pl.BlockSpec((1, d), lambda j, i: (0, 0)),
                  pl.BlockSpec((d, tn), lambda j, i: (0, j))],
        out_specs=pl.BlockSpec((tm, tn), lambda j, i: (i, j)),
        out_shape=jax.ShapeDtypeStruct((n, c), F32),
        compiler_params=_cparams(("arbitrary", "arbitrary")),
        name="norm_proj",
    )(h, gain.reshape(1, d), w_bf16)


def _gathered_combine(h_ref, route_ref, dest_ref, destn_ref, ys_hbm, yg, gsem):
    i = pl.program_id(0)
    nt = pl.num_programs(0)
    slot = lax.rem(i, 2)
    tm = h_ref.shape[0]

    def row_copy(idx_ref, buf_slot, t, c):
        return pltpu.make_async_copy(ys_hbm.at[pl.ds(idx_ref[0, 0, TOP_K * t + c], 1), :],
                                     yg.at[buf_slot, c, pl.ds(t, 1), :], gsem.at[buf_slot])

    def wait_tile(buf_slot):
        for c in range(TOP_K):
            pltpu.make_async_copy(ys_hbm.at[pl.ds(0, tm), :], yg.at[buf_slot, c], gsem.at[buf_slot]).wait()

    @pl.when(i == 0)
    def _():
        def body(g, carry):
            for u in range(DMA_UNROLL):
                for c in range(TOP_K):
                    row_copy(dest_ref, 0, g * DMA_UNROLL + u, c).start()
            return carry
        lax.fori_loop(0, tm // DMA_UNROLL, body, 0)

    wait_tile(slot)
    for t in range(tm):
        for c in range(TOP_K):
            row_copy(destn_ref, 1 - slot, t, c).start()
    r = route_ref[...]
    x = h_ref[...] + r[:, 2:3] * yg[slot, 0] + r[:, 3:4] * yg[slot, 1]

    def finalize():
        @pl.when(i == nt - 1)
        def _():
            wait_tile(1 - slot)
    return x, finalize


def _combine_norm_proj_kernel(h_ref, route_ref, dest_ref, destn_ref, g_ref, w_ref, ys_hbm, hnew_ref, o_ref,
                              yg, gsem):
    x, finalize = _gathered_combine(h_ref, route_ref, dest_ref, destn_ref, ys_hbm, yg, gsem)
    hnew_ref[...] = x
    o_ref[...] = _dot(_rms(x, g_ref[...]), w_ref[...])
    finalize()


def _combine_kernel(h_ref, route_ref, dest_ref, destn_ref, ys_hbm, o_ref, yg, gsem):
    x, finalize = _gathered_combine(h_ref, route_ref, dest_ref, destn_ref, ys_hbm, yg, gsem)
    o_ref[...] = x
    finalize()


def _combine_specs(n, d, tm):
    nt = n // tm
    smem = lambda fn: pl.BlockSpec((1, 1, TOP_K * tm), fn, memory_space=pltpu.SMEM)
    in_specs = [pl.BlockSpec((tm, d), lambda i: (i, 0)),
                pl.BlockSpec((tm, LANES), lambda i: (i, 0)),
                smem(lambda i: (i, 0, 0)),
                smem(lambda i: (jnp.minimum(i + 1, nt - 1), 0, 0))]
    scratch = [pltpu.VMEM((2, TOP_K, tm, d), F32), pltpu.SemaphoreType.DMA((2,))]
    return in_specs, scratch


def _combine_norm_proj(h, ys, route, dest3, gain, w_bf16, tm):
    n, d = h.shape
    c = w_bf16.shape[1]
    in_specs, scratch = _combine_specs(n, d, tm)
    return pl.pallas_call(
        _combine_norm_proj_kernel,
        grid=(n // tm,),
        in_specs=in_specs + [pl.BlockSpec((1, d), lambda i: (0, 0)),
                             pl.BlockSpec((d, c), lambda i: (0, 0)),
                             pl.BlockSpec(memory_space=pl.ANY)],
        out_specs=[pl.BlockSpec((tm, d), lambda i: (i, 0)),
                   pl.BlockSpec((tm, c), lambda i: (i, 0))],
        out_shape=[jax.ShapeDtypeStruct((n, d), F32), jax.ShapeDtypeStruct((n, c), F32)],
        scratch_shapes=scratch,
        compiler_params=_cparams(("arbitrary",)),
        name="combine_norm_proj",
    )(h, route, dest3, dest3, gain.reshape(1, d), w_bf16, ys)


def _combine(h, ys, route, dest3, tm):
    n, d = h.shape
    in_specs, scratch = _combine_specs(n, d, tm)
    return pl.pallas_call(
        _combine_kernel,
        grid=(n // tm,),
        in_specs=in_specs + [pl.BlockSpec(memory_space=pl.ANY)],
        out_specs=pl.BlockSpec((tm, d), lambda i: (i, 0)),
        out_shape=jax.ShapeDtypeStruct((n, d), F32),
        scratch_shapes=scratch,
        compiler_params=_cparams(("arbitrary",)),
        name="combine",
    )(h, route, dest3, dest3, ys)


def _shifted_taps(buf, u_ref, n_taps):
    buf[TAIL:TAIL + CHUNK, :] = u_ref[...]
    taps = [buf[TAIL - j:TAIL - j + CHUNK, :] for j in range(n_taps)]
    return taps


def _carry_tail(buf):
    buf[0:TAIL, :] = buf[CHUNK:CHUNK + TAIL, :]


def _ssd_stages(z_ref, x_ref, bc_ref, dt_ref, cwx_ref, cbx_ref, cwbc_ref, cbbc_ref, dtb_ref, alog_ref,
                dskip_ref, nw_ref, hexp_ref, o_ref, xbuf, bcbuf, state):
    st = {}
    gw = SSD_INNER // SSD_GROUPS

    def init():
        @pl.when(pl.program_id(1) == 0)
        def _():
            xbuf[0:TAIL, :] = jnp.zeros((TAIL, SSD_INNER), F32)
            bcbuf[0:TAIL, :] = jnp.zeros((TAIL, SSD_INNER), F32)
            state[...] = jnp.zeros_like(state)

    def conv(buf, u_ref, w_ref, b_ref):
        taps = _shifted_taps(buf, u_ref, SSD_CONV)
        acc = b_ref[...] + taps[0] * w_ref[3:4, :]
        for j in range(1, SSD_CONV):
            acc = acc + taps[j] * w_ref[3 - j:4 - j, :]
        _carry_tail(buf)
        return _silu(acc)

    def convs():
        st["xs"] = conv(xbuf, x_ref, cwx_ref, cbx_ref)
        st["bc"] = conv(bcbuf, bc_ref, cwbc_ref, cbbc_ref)

    def decays():
        lane = _iota2((CHUNK, LANES), 1)
        dt = _softplus(dt_ref[...] + dtb_ref[...])
        adt = jnp.where(lane < SSD_HEADS, -jnp.exp(alog_ref[...]) * dt, 0.0)
        cum = _dot_exact_lhs(_tri_incl_bf16(CHUNK), adt)
        hexp = hexp_ref[...]
        cum_full = _dot_exact_rhs(cum, hexp)
        tot_full = cum_full[CHUNK - 1:CHUNK, :]
        xd = st["xs"] * _dot_exact_rhs(dt, hexp)
        st.update(cum=cum, cum_t=cum.T, tot_full=tot_full, xd=xd, xds=xd * jnp.exp(tot_full - cum_full),
                  eac=jnp.exp(cum_full), lane_lo=lane < SSD_HEAD_DIM,
                  causal=_iota2((CHUNK, CHUNK), 0) >= _iota2((CHUNK, CHUNK), 1), y_parts=[])

    def group(g):
        bc, cum, cum_t, xd = st["bc"], st["cum"], st["cum_t"], st["xd"]
        bg = bc[:, g * SSD_STATE:(g + 1) * SSD_STATE]
        cg = bc[:, (SSD_GROUPS + g) * SSD_STATE:(SSD_GROUPS + g + 1) * SSD_STATE]
        cb = _dot_nt(cg, bg)
        s_prev = state[:, g * gw:(g + 1) * gw]
        y_off = _dot(cg, s_prev) * st["eac"][:, g * gw:(g + 1) * gw]
        s_new = _dot(bg.T, st["xds"][:, g * gw:(g + 1) * gw])
        state[:, g * gw:(g + 1) * gw] = jnp.exp(st["tot_full"][:, g * gw:(g + 1) * gw]) * s_prev + s_new
        for pr in range(2):
            lo = g * gw + pr * LANES
            xd_pair = xd[:, lo:lo + LANES]
            yd = jnp.zeros((CHUNK, LANES), F32)
            for k in range(2):
                h = (lo // SSD_HEAD_DIM) + k
                diff = cum[:, h:h + 1] - cum_t[h:h + 1, :]
                decay = jnp.exp(jnp.where(st["causal"], diff, NEG_BIG))
                keep = st["lane_lo"] if k == 0 else jnp.logical_not(st["lane_lo"])
                yd = yd + _dot(cb * decay, jnp.where(keep, xd_pair, 0.0))
            st["y_parts"].append(yd + y_off[:, pr * LANES:(pr + 1) * LANES])

    def finish():
        y = jnp.concatenate(st["y_parts"], axis=1) + dskip_ref[...] * st["xs"]
        y = y * _silu(z_ref[...])
        outs = []
        for g in range(SSD_GROUPS):
            yg = y[:, g * gw:(g + 1) * gw]
            outs.append(yg * lax.rsqrt(jnp.mean(yg * yg, axis=-1, keepdims=True) + 1e-5))
        o_ref[...] = (jnp.concatenate(outs, axis=1) * nw_ref[...]).astype(o_ref.dtype)

    return [init, convs, decays] + [functools.partial(group, g) for g in range(SSD_GROUPS)] + [finish]


def _head_expand(n_heads, head_dim):
    h = jnp.arange(LANES)[:, None]
    l = jnp.arange(n_heads * head_dim)[None, :]
    return (l // head_dim == h).astype(BF16)


def _ssd_operands(proj, nc, conv_w, conv_b, dt_bias, a_log, d_skip, norm_w):
    pad16 = lambda v: jnp.pad(v.astype(F32), (0, LANES - v.shape[0])).reshape(1, LANES)
    row_spec = lambda w, blk: pl.BlockSpec((CHUNK, w), lambda b, c: (b * nc + c, blk))
    full = lambda shape: pl.BlockSpec(shape, lambda b, c: (0, 0))
    in_specs = [row_spec(SSD_INNER, 0),
                row_spec(SSD_INNER, 1),
                row_spec(SSD_INNER, 2),
                row_spec(LANES, 50),
                full((SSD_CONV, SSD_INNER)), full((1, SSD_INNER)),
                full((SSD_CONV, SSD_INNER)), full((1, SSD_INNER)),
                full((1, LANES)), full((1, LANES)),
                full((1, SSD_INNER)), full((1, SSD_INNER)),
                full((LANES, SSD_INNER))]
    operands = [proj, proj, proj, proj,
                conv_w[:, :SSD_INNER], conv_b[:SSD_INNER].reshape(1, -1),
                conv_w[:, SSD_INNER:], conv_b[SSD_INNER:].reshape(1, -1),
                pad16(dt_bias), pad16(a_log),
                jnp.repeat(d_skip.astype(F32), SSD_HEAD_DIM).reshape(1, -1), norm_w.reshape(1, -1),
                _head_expand(SSD_HEADS, SSD_HEAD_DIM)]
    scratch = [pltpu.VMEM((TAIL + CHUNK, SSD_INNER), F32),
               pltpu.VMEM((TAIL + CHUNK, SSD_INNER), F32),
               pltpu.VMEM((SSD_STATE, SSD_INNER), F32)]
    return in_specs, operands, scratch


def _rwkv_kernel(r_ref, k_ref, v_ref, lo_ref, mur_ref, muk_ref, muv_ref, mulo_ref, w0_ref, w2_ref, a0_ref,
                 a2_ref, g2_ref, kk_ref, ka_ref, rk_ref, lnw_ref, lnb_ref, o_ref,
                 rbuf, kbuf, vbuf, lobuf, state, side_work=()):
    side = iter(side_work)
    run_side = lambda: next(side, lambda: None)()
    run_side()
    c = pl.program_id(1)

    @pl.when(c == 0)
    def _():
        rbuf[0:TAIL, :] = jnp.zeros((TAIL, RWKV_DIM), F32)
        kbuf[0:TAIL, :] = jnp.zeros((TAIL, RWKV_DIM), F32)
        vbuf[0:TAIL, :] = jnp.zeros((TAIL, RWKV_DIM), F32)
        lobuf[0:TAIL, :] = jnp.zeros((TAIL, 2 * LANES), F32)
        state[...] = jnp.zeros_like(state)

    def shift(buf, u_ref, mu_ref):
        cur, prev = _shifted_taps(buf, u_ref, 2)
        _carry_tail(buf)
        return cur + mu_ref[...] * (prev - cur)

    r = shift(rbuf, r_ref, mur_ref)
    k = shift(kbuf, k_ref, muk_ref)
    v = shift(vbuf, v_ref, muv_ref)
    lo = shift(lobuf, lo_ref, mulo_ref)
    wa = lo[:, :LANES]
    logw = -math.exp(-0.5) * _sigmoid(w0_ref[...] + _dot(jnp.tanh(wa), w2_ref[...]))
    a = _sigmoid(a0_ref[...] + _dot(wa, a2_ref[...]))
    g = _dot(_sigmoid(lo[:, LANES:]), g2_ref[...])

    lane = _iota2((CHUNK, LANES), 1)
    head0 = lane < RWKV_HEAD_DIM
    bd = (_iota2((LANES, LANES), 0) // RWKV_HEAD_DIM) == (_iota2((LANES, LANES), 1) // RWKV_HEAD_DIM)
    bd_ones = jnp.where(bd, 1.0, 0.0).astype(BF16)

    def head_sum(x):
        return jnp.dot(x.astype(BF16), bd_ones, preferred_element_type=F32)

    tri = _tri_incl_bf16(CHUNK)
    cum = _dot_exact_lhs(tri, logw)
    cume = cum - logw
    cmid = cum[CHUNK // 2 - 1:CHUNK // 2, :]
    cend = cum[CHUNK - 1:CHUNK, :]
    e_in_mid = jnp.exp(cum - cmid)
    e_ex_mid = jnp.exp(cume - cmid)
    e_mid_in = jnp.exp(cmid - cum)
    e_ex = jnp.exp(cume)
    e_in = jnp.exp(cum)
    e_end = jnp.exp(cend - cum)
    e_tot = jnp.exp(cend)

    row = _iota2((CHUNK, CHUNK), 0)
    col = _iota2((CHUNK, CHUNK), 1)
    strict = row > col
    incl = row >= col
    zeros = jnp.zeros((CHUNK, LANES), F32)

    n_blocks = RWKV_DIM // LANES
    blk = []
    for p in range(n_blocks):
        sl = slice(p * LANES, (p + 1) * LANES)
        rp, vp, ap = r[:, sl], v[:, sl], a[:, sl]
        kkp = k[:, sl] * kk_ref[:, sl]
        kkp = kkp * lax.rsqrt(jnp.maximum(head_sum(kkp * kkp), 1e-24))
        kp = k[:, sl] * (1.0 + (ap - 1.0) * ka_ref[:, sl])
        aap = -kkp
        bp = kkp * ap
        a_mid = aap * e_ex_mid[:, sl]
        r_mid = rp * e_in_mid[:, sl]
        bf = lambda x: x.astype(BF16)
        lhs = jnp.concatenate([bf(jnp.where(head0, a_mid, 0.0)), bf(jnp.where(head0, 0.0, a_mid)),
                               bf(jnp.where(head0, r_mid, 0.0)), bf(jnp.where(head0, 0.0, r_mid))], axis=0)
        rhs = jnp.concatenate([bf(bp * e_mid_in[:, sl]), bf(kp * e_mid_in[:, sl])], axis=0)
        a_abs = aap * e_ex[:, sl]
        blk.append(dict(sl=sl, rp=rp, vp=vp, kp=kp, r_abs=rp * e_in[:, sl],
                        a_abs=(bf(jnp.where(head0, a_abs, 0.0)), bf(jnp.where(head0, 0.0, a_abs))),
                        vm=(bf(jnp.where(head0, vp, 0.0)), bf(jnp.where(head0, 0.0, vp))),
                        b_end=bf(bp * e_end[:, sl]), k_end=bf(kp * e_end[:, sl]), prod=_dot_nt(lhs, rhs)))
    run_side()

    heads = []
    for p in range(n_blocks):
        prod = blk[p]["prod"]
        for hh in range(2):
            heads.append(dict(
                vm=blk[p]["vm"][hh], a_abs=blk[p]["a_abs"][hh],
                a_ab=jnp.where(strict, prod[hh * CHUNK:(hh + 1) * CHUNK, :CHUNK], 0.0),
                a_ak=jnp.where(strict, prod[hh * CHUNK:(hh + 1) * CHUNK, CHUNK:], 0.0).astype(BF16),
                m_rb=jnp.where(incl, prod[(2 + hh) * CHUNK:(3 + hh) * CHUNK, :CHUNK], 0.0).astype(BF16),
                m_rk=jnp.where(incl, prod[(2 + hh) * CHUNK:(3 + hh) * CHUNK, CHUNK:], 0.0).astype(BF16)))

    eye = jnp.where(row == col, 1.0, 0.0)
    n_levels = int(math.log2(CHUNK)) - 1
    ts = [eye + hd["a_ab"] for hd in heads]
    xs = [hd["a_ab"].astype(BF16) for hd in heads]
    avs = [_dot(hd["a_ak"], hd["vm"]).astype(BF16) for hd in heads]
    run_side()
    xs = [jnp.dot(x, x, preferred_element_type=F32).astype(BF16) for x in xs]
    for _ in range(n_levels - 1):
        run_side()
        zs = [jnp.dot(x, jnp.concatenate([x, t.astype(BF16)], axis=1), preferred_element_type=F32)
              for x, t in zip(xs, ts)]
        xs = [z[:, :CHUNK].astype(BF16) for z in zs]
        ts = [t + z[:, CHUNK:] for t, z in zip(ts, zs)]
    ts = [t + jnp.dot(x, t.astype(BF16), preferred_element_type=F32) for x, t in zip(xs, ts)]
    for _ in side:
        _()

    wmats = [_dot(t, jnp.concatenate([hd["a_abs"], av], axis=1)).astype(BF16)
             for t, hd, av in zip(ts, heads, avs)]
    zeros_bf = zeros.astype(BF16)
    outs = [_dot(jnp.concatenate([hd["m_rb"], hd["m_rk"]], axis=1),
                 jnp.concatenate([jnp.concatenate([wm[:, LANES:], wm[:, :LANES]], axis=1),
                                  jnp.concatenate([hd["vm"], zeros_bf], axis=1)], axis=0))
            for wm, hd in zip(wmats, heads)]

    zts = []
    for p in range(n_blocks):
        w0h, w1h = wmats[2 * p], wmats[2 * p + 1]
        ui = w0h[:, LANES:] + w1h[:, LANES:]
        a_eff = w0h[:, :LANES] + w1h[:, :LANES]
        zts.append(_dot_tn(jnp.concatenate([jnp.concatenate([ui, a_eff], axis=1),
                                            jnp.concatenate([blk[p]["vp"].astype(BF16), zeros_bf], axis=1)],
                                           axis=0),
                           jnp.concatenate([blk[p]["b_end"], blk[p]["k_end"]], axis=0)))

    ys = []
    for p in range(n_blocks):
        s0 = state[p]
        o0, o1 = outs[2 * p], outs[2 * p + 1]
        r_eff = blk[p]["r_abs"] + o0[:, LANES:] + o1[:, LANES:]
        ys.append(_dot_nt(r_eff, s0) + o0[:, :LANES] + o1[:, :LANES])
        h_intra = jnp.where(bd, zts[p][:LANES, :], 0.0)
        g_corr = jnp.where(bd, zts[p][LANES:, :], 0.0)
        state[p] = s0 * e_tot[:, blk[p]["sl"]] + _dot(s0, g_corr) + h_intra

    means = [head_sum(y) * (1.0 / RWKV_HEAD_DIM) for y in ys]
    devs = [y - m for y, m in zip(ys, means)]
    vars_ = [head_sum(dv * dv) * (1.0 / RWKV_HEAD_DIM) for dv in devs]
    for p in range(n_blocks):
        sl = blk[p]["sl"]
        yn = devs[p] * lax.rsqrt(vars_[p] + RWKV_GN_EPS) * lnw_ref[:, sl] + lnb_ref[:, sl]
        bonus = head_sum(blk[p]["rp"] * blk[p]["kp"] * rk_ref[:, sl])
        o_ref[:, sl] = ((yn + bonus * blk[p]["vp"]) * g[:, sl]).astype(o_ref.dtype)


def _rwkv_operands(proj, nc, mu, w0, w2, a0, a2, g2, k_k, k_a, r_k, ln_w, ln_b):
    d = RWKV_DIM
    row_spec = lambda w, blk: pl.BlockSpec((CHUNK, w), lambda b, c: (b * nc + c, blk))
    full = lambda shape: pl.BlockSpec(shape, lambda b, c: (0,) * len(shape))
    vec = lambda x: x.astype(F32).reshape(1, -1)
    w2p = jnp.concatenate([w2, jnp.zeros((AAA_LORA, d), w2.dtype)], axis=0).astype(BF16)
    a2p = jnp.concatenate([jnp.zeros((DECAY_LORA, d), a2.dtype), a2], axis=0).astype(BF16)
    in_specs = [row_spec(d, 3), row_spec(d, 4), row_spec(d, 5), row_spec(2 * LANES, 24),
                full((1, d)), full((1, d)), full((1, d)), full((1, 2 * LANES)),
                full((1, d)), full((LANES, d)), full((1, d)), full((LANES, d)), full((LANES, d)),
                full((1, d)), full((1, d)), full((1, d)), full((1, d)), full((1, d))]
    operands = [proj, proj, proj, proj,
                vec(mu[:d]), vec(mu[d:2 * d]), vec(mu[2 * d:3 * d]), vec(mu[3 * d:]),
                vec(w0), w2p, vec(a0), a2p, g2.astype(BF16),
                vec(k_k), vec(k_a), vec(r_k), vec(ln_w), vec(ln_b)]
    scratch = [pltpu.VMEM((TAIL + CHUNK, d), F32), pltpu.VMEM((TAIL + CHUNK, d), F32),
               pltpu.VMEM((TAIL + CHUNK, d), F32), pltpu.VMEM((TAIL + CHUNK, 2 * LANES), F32),
               pltpu.VMEM((d // LANES, LANES, LANES), F32)]
    return in_specs, operands, scratch


def _mixers_kernel(*refs, n_ssd_in, n_rwkv_in, n_ssd_scratch):
    ssd_in = refs[:n_ssd_in]
    rwkv_in = refs[n_ssd_in:n_ssd_in + n_rwkv_in]
    o_ssd, o_rwkv = refs[n_ssd_in + n_rwkv_in:n_ssd_in + n_rwkv_in + 2]
    scratch = refs[n_ssd_in + n_rwkv_in + 2:]
    _rwkv_kernel(*rwkv_in, o_rwkv, *scratch[n_ssd_scratch:],
                 side_work=_ssd_stages(*ssd_in, o_ssd, *scratch[:n_ssd_scratch]))


def _mixers(proj, batch, seq, ssd_params, rwkv_params):
    nc = seq // CHUNK
    n = batch * seq
    s_specs, s_ops, s_scratch = _ssd_operands(proj, nc, *ssd_params)
    r_specs, r_ops, r_scratch = _rwkv_operands(proj, nc, *rwkv_params)
    out_spec = lambda w: pl.BlockSpec((CHUNK, w), lambda b, c: (b * nc + c, 0))
    return pl.pallas_call(
        functools.partial(_mixers_kernel, n_ssd_in=len(s_specs), n_rwkv_in=len(r_specs),
                          n_ssd_scratch=len(s_scratch)),
        grid=(batch, nc),
        in_specs=s_specs + r_specs,
        out_specs=[out_spec(SSD_INNER), out_spec(RWKV_DIM)],
        out_shape=[jax.ShapeDtypeStruct((n, SSD_INNER), BF16), jax.ShapeDtypeStruct((n, RWKV_DIM), BF16)],
        scratch_shapes=s_scratch + r_scratch,
        compiler_params=_cparams(("arbitrary", "arbitrary")),
        name="mixers",
    )(*s_ops, *r_ops)


def _route(logits):
    lane = _iota2(logits.shape, 1)
    lanef = lane.astype(F32)
    big = float(LANES)

    def first_max(x):
        m = jnp.max(x, axis=-1, keepdims=True)
        idx = jnp.min(jnp.where(x == m, lanef, big), axis=-1, keepdims=True)
        return m, idx

    cl = jnp.where(lane < EXPERT_GROUPS, logits, NEG_BIG)
    cmax, grp = first_max(cl)
    p_group = 1.0 / jnp.sum(jnp.exp(cl - cmax), axis=-1, keepdims=True)
    lo = EXPERT_GROUPS + grp * EXPERTS_PER_GROUP
    fl = jnp.where((lanef >= lo) & (lanef < lo + EXPERTS_PER_GROUP), logits, NEG_BIG)
    m0, i0 = first_max(fl)
    m1, i1 = first_max(jnp.where(lanef == i0, NEG_BIG, fl))
    e1 = jnp.exp(m1 - m0)
    g0 = p_group / (1.0 + e1)
    g1 = p_group * e1 / (1.0 + e1)
    return jnp.where(lane == 0, i0 - EXPERT_GROUPS,
                     jnp.where(lane == 1, i1 - EXPERT_GROUPS,
                               jnp.where(lane == 2, g0, jnp.where(lane == 3, g1, 0.0))))


def _outproj_router_kernel(*refs, n_in):
    ys = refs[:n_in]
    ws = refs[n_in:2 * n_in]
    h_ref, g_ref, wr_ref, br_ref, hnew_ref, hf_ref, route_ref = refs[2 * n_in:]
    acc = h_ref[...]
    for y_ref, w_ref in zip(ys, ws):
        acc = acc + jnp.dot(y_ref[...], w_ref[...], preferred_element_type=F32)
    hnew_ref[...] = acc
    hf = _rms(acc, g_ref[...])
    hf_ref[...] = hf
    route_ref[...] = _route(_dot_f32(hf, wr_ref[...]) + br_ref[...])


def _outproj_router(ys, ws, h, gain, w_router, b_router, tm):
    n, d = h.shape
    n_in = len(ys)
    in_specs = ([pl.BlockSpec((tm, y.shape[1]), lambda i: (i, 0)) for y in ys]
                + [pl.BlockSpec(w.shape, lambda i: (0, 0)) for w in ws]
                + [pl.BlockSpec((tm, d), lambda i: (i, 0)),
                   pl.BlockSpec((1, d), lambda i: (0, 0)),
                   pl.BlockSpec((d, LANES), lambda i: (0, 0)),
                   pl.BlockSpec((1, LANES), lambda i: (0, 0))])
    return pl.pallas_call(
        functools.partial(_outproj_router_kernel, n_in=n_in),
        grid=(n // tm,),
        in_specs=in_specs,
        out_specs=[pl.BlockSpec((tm, d), lambda i: (i, 0)),
                   pl.BlockSpec((tm, d), lambda i: (i, 0)),
                   pl.BlockSpec((tm, LANES), lambda i: (i, 0))],
        out_shape=[jax.ShapeDtypeStruct((n, d), F32), jax.ShapeDtypeStruct((n, d), F32),
                   jax.ShapeDtypeStruct((n, LANES), F32)],
        compiler_params=_cparams(("arbitrary",)),
        name="outproj_router",
    )(*ys, *ws, h, gain.reshape(1, d), w_router, b_router)


def _router_weights(w_coarse, b_coarse, w_fine, b_fine):
    d = w_coarse.shape[0]
    wf = jnp.transpose(w_fine, (1, 0, 2)).reshape(d, N_EXPERTS)
    w = jnp.concatenate([w_coarse, wf], axis=1).astype(F32)
    b = jnp.concatenate([b_coarse, b_fine.reshape(N_EXPERTS)]).astype(F32)
    pad = LANES - w.shape[1]
    return jnp.pad(w, ((0, 0), (0, pad))), jnp.pad(b, (0, pad)).reshape(1, LANES)


def _moe_plan(route, tm):
    n = route.shape[0]
    a = n * TOP_K
    n_blocks = a // MOE_BLOCK + N_EXPERTS
    e_flat = route[:, :TOP_K].astype(jnp.int32).reshape(a)
    seg = MOE_BLOCK
    onehot = (e_flat[:, None] == jnp.arange(N_EXPERTS, dtype=jnp.int32)[None, :]).astype(F32)
    onehot = onehot.reshape(a // seg, seg, N_EXPERTS)
    tri = jnp.tril(jnp.ones((seg, seg), F32))
    within = jnp.einsum("ij,bjk->bik", tri, onehot)
    tot = within[:, -1, :]
    offs = jnp.cumsum(tot, axis=0) - tot
    rank = (jnp.sum(onehot * (within + offs[:, None, :]), axis=-1) - 1.0).astype(jnp.int32).reshape(a)
    counts = (offs[-1] + tot[-1]).astype(jnp.int32)
    padded = (counts + MOE_BLOCK - 1) // MOE_BLOCK * MOE_BLOCK
    pad_end = jnp.cumsum(padded)
    pad_start = pad_end - padded
    start_of = jnp.sum(jnp.where(e_flat[:, None] == jnp.arange(N_EXPERTS, dtype=jnp.int32)[None, :],
                                 pad_start[None, :], 0), axis=1)
    dest = (start_of + rank).astype(jnp.int32)
    block_start = jnp.arange(n_blocks, dtype=jnp.int32) * MOE_BLOCK
    block_expert = jnp.minimum(jnp.sum((pad_end[None, :] <= block_start[:, None]).astype(jnp.int32), axis=1),
                               N_EXPERTS - 1).astype(jnp.int32)
    n_used = (pad_end[-1] // MOE_BLOCK).astype(jnp.int32).reshape(1)
    pad_lo = (pad_start + counts).astype(jnp.int32)
    return dict(dest3=dest.reshape(n // tm, 1, TOP_K * tm), block_expert=block_expert, n_used=n_used,
                pad_lo=pad_lo, pad_hi=pad_end.astype(jnp.int32), n_blocks=n_blocks)


def _dispatch_kernel(padlo_ref, padhi_ref, nused_ref, hf_ref, dest_ref, xs_hbm, zblk, sem, zsem):
    i = pl.program_id(0)
    tm = hf_ref.shape[0]

    @pl.when(i == 0)
    def _():
        zblk[...] = jnp.zeros_like(zblk)

        def row_copy(rw):
            return pltpu.make_async_copy(zblk.at[pl.ds(0, 1), :], xs_hbm.at[pl.ds(rw, 1), :], zsem)

        def group_copy(g):
            return pltpu.make_async_copy(zblk.at[pl.ds(0, TAIL), :],
                                         xs_hbm.at[pl.ds(pl.multiple_of(g * TAIL, TAIL), TAIL), :], zsem)

        def run(lo, hi, make, wait):
            def body(k, c2):
                cp = make(k)
                cp.wait() if wait else cp.start()
                return c2
            lax.fori_loop(lo, hi, body, 0)

        def per_expert(e, carry):
            lo, hi = padlo_ref[e], padhi_ref[e]
            mid = jnp.minimum((lo + TAIL - 1) // TAIL * TAIL, hi)
            for wait in (False, True):
                run(lo, mid, row_copy, wait)
                run(mid // TAIL, hi // TAIL, group_copy, wait)
            return carry
        lax.fori_loop(0, N_EXPERTS, per_expert, 0)

        n_blocks = xs_hbm.shape[0] // MOE_BLOCK

        def block_copy(b):
            return pltpu.make_async_copy(
                zblk, xs_hbm.at[pl.ds(pl.multiple_of(b * MOE_BLOCK, MOE_BLOCK), MOE_BLOCK), :], zsem)

        def zero_block(b, carry):
            block_copy(b).start()
            return carry
        lax.fori_loop(nused_ref[0], n_blocks, zero_block, 0)

        def wait_block(b, carry):
            block_copy(b).wait()
            return carry
        lax.fori_loop(nused_ref[0], n_blocks, wait_block, 0)

    for t in range(tm):
        for c in range(TOP_K):
            pltpu.make_async_copy(hf_ref.at[pl.ds(t, 1), :],
                                  xs_hbm.at[pl.ds(dest_ref[0, 0, TOP_K * t + c], 1), :], sem).start()
    for c in range(TOP_K):
        pltpu.make_async_copy(hf_ref, xs_hbm.at[pl.ds(0, tm), :], sem).wait()


def _dispatch(hf, plan, tm):
    n, d = hf.shape
    slots = plan["n_blocks"] * MOE_BLOCK
    grid_spec = pltpu.PrefetchScalarGridSpec(
        num_scalar_prefetch=3,
        grid=(n // tm,),
        in_specs=[pl.BlockSpec((tm, d), lambda i, lo, hi, nu: (i, 0)),
                  pl.BlockSpec((1, 1, TOP_K * tm), lambda i, lo, hi, nu: (i, 0, 0), memory_space=pltpu.SMEM)],
        out_specs=pl.BlockSpec(memory_space=pl.ANY),
        scratch_shapes=[pltpu.VMEM((MOE_BLOCK, d), F32), pltpu.SemaphoreType.DMA(()),
                        pltpu.SemaphoreType.DMA(())])
    return pl.pallas_call(
        _dispatch_kernel,
        grid_spec=grid_spec,
        out_shape=jax.ShapeDtypeStruct((slots, d), F32),
        compiler_params=_cparams(("arbitrary",)),
        name="moe_dispatch",
    )(plan["pad_lo"], plan["pad_hi"], plan["n_used"], hf, plan["dest3"])


def _experts_kernel(bexp_ref, nused_ref, x_ref, wg_ref, wu_ref, wd_ref, y_ref, wg_s, wu_s, wd_s):
    i = pl.program_id(0)
    active = i < nused_ref[0]

    @pl.when(active & ((i == 0) | (bexp_ref[i] != bexp_ref[jnp.maximum(i - 1, 0)])))
    def _():
        wg_s[...] = wg_ref[0].astype(BF16)
        wu_s[...] = wu_ref[0].astype(BF16)
        wd_s[...] = wd_ref[0].astype(BF16)

    @pl.when(active)
    def _():
        x = x_ref[...].astype(BF16)
        hg = jnp.dot(x, wg_s[...], preferred_element_type=F32)
        hu = jnp.dot(x, wu_s[...], preferred_element_type=F32)
        hb = (_silu(hg) * hu).astype(BF16)
        y_ref[...] = jnp.dot(hb, wd_s[...], preferred_element_type=F32)

    @pl.when(i >= nused_ref[0])
    def _():
        y_ref[...] = jnp.zeros_like(y_ref)


def _experts(xs, plan, layer, wg, wu, wd):
    slots, d = xs.shape
    n_blocks = plan["n_blocks"]
    used = lambda i, nu: jnp.minimum(i, nu[0] - 1)
    w_idx = lambda i, be, nu: (layer, be[used(i, nu)], 0, 0)
    grid_spec = pltpu.PrefetchScalarGridSpec(
        num_scalar_prefetch=2,
        grid=(n_blocks,),
        in_specs=[pl.BlockSpec((MOE_BLOCK, d), lambda i, be, nu: (used(i, nu), 0)),
                  pl.BlockSpec((None, 1, d, EXPERT_HIDDEN), w_idx),
                  pl.BlockSpec((None, 1, d, EXPERT_HIDDEN), w_idx),
                  pl.BlockSpec((None, 1, EXPERT_HIDDEN, d), w_idx)],
        out_specs=pl.BlockSpec((MOE_BLOCK, d), lambda i, be, nu: (i, 0)),
        scratch_shapes=[pltpu.VMEM((d, EXPERT_HIDDEN), BF16), pltpu.VMEM((d, EXPERT_HIDDEN), BF16),
                        pltpu.VMEM((EXPERT_HIDDEN, d), BF16)])
    return pl.pallas_call(
        _experts_kernel,
        grid_spec=grid_spec,
        out_shape=jax.ShapeDtypeStruct((slots, d), F32),
        compiler_params=_cparams(("arbitrary",)),
        name="moe_experts",
    )(plan["block_expert"], plan["n_used"], xs, wg, wu, wd)


def _swa_kernel(q_ref, kvp_ref, kvc_ref, qg_ref, kg_ref, slope_ref, sink_ref, o_ref):
    jb = pl.program_id(1)
    blk = CHUNK
    qi = _iota2((blk, blk), 0)
    kj = _iota2((blk, blk), 1)
    from_prev = kj > qi
    deltaf = jnp.where(from_prev, qi + blk - kj, qi - kj).astype(F32)
    no_prev = jnp.where(from_prev, jnp.where(jb > 0, 0.0, NEG_BIG), 0.0)
    scale = HEAD_DIM ** -0.5

    bd = (_iota2((LANES, LANES), 0) // HEAD_DIM) == (_iota2((LANES, LANES), 1) // HEAD_DIM)
    bd_ones = jnp.where(bd, 1.0, 0.0).astype(BF16)

    def head_rms(x, gain):
        sq = x * x
        s1 = sq.astype(BF16)
        s2 = (sq - s1.astype(F32)).astype(BF16)
        ms = (jnp.dot(s1, bd_ones, preferred_element_type=F32)
              + jnp.dot(s2, bd_ones, preferred_element_type=F32)) * (1.0 / HEAD_DIM)
        return x * lax.rsqrt(ms + NORM_EPS) * gain

    n_kv_blk = KV_DIM // LANES
    kv = jnp.concatenate([kvp_ref[...], kvc_ref[...]], axis=0)
    lane_kv = _iota2((kv.shape[0], LANES), 1)
    kv_half = (lane_kv < HEAD_DIM, lane_kv >= HEAD_DIM)
    lane_q = _iota2((blk, LANES), 1)
    q_half = (lane_q < HEAD_DIM, lane_q >= HEAD_DIM)
    kn = [head_rms(kv[:, j * LANES:(j + 1) * LANES], kg_ref[...]) for j in range(n_kv_blk)]
    vb = [kv[:, KV_DIM + j * LANES:KV_DIM + (j + 1) * LANES] for j in range(n_kv_blk)]
    kn_sw = [pltpu.roll(x, HEAD_DIM, 1) for x in kn]
    vb_sw = [pltpu.roll(x, HEAD_DIM, 1) for x in vb]
    v_same = [jnp.where(kv_half[g % 2], vb[g // 2], 0.0) for g in range(KV_HEADS)]
    v_swap = [jnp.where(kv_half[1 - g % 2], vb_sw[g // 2], 0.0) for g in range(KV_HEADS)]
    qn = [head_rms(q_ref[:, j * LANES:(j + 1) * LANES], qg_ref[...]) for j in range(Q_DIM // LANES)]

    rep = lambda x: jnp.concatenate([x] * Q_PER_KV, axis=0)
    from_prev4, delta4 = rep(from_prev), rep(deltaf)
    neg4 = [rep(no_prev)] + [None] * (SWA_QBLOCKS - 1)
    tile = lambda ref, hs: jnp.concatenate([jnp.broadcast_to(ref[:, h:h + 1], (blk, LANES)) for h in hs], axis=0)
    order = lambda g: [g * Q_PER_KV + g % 2, g * Q_PER_KV + g % 2 + 2,
                       g * Q_PER_KV + 1 - g % 2, g * Q_PER_KV + 3 - g % 2]
    units = [(u, g) for u in range(SWA_QBLOCKS) for g in range(KV_HEADS)]
    keys = lambda x, u: x[u * blk:(u + 2) * blk]
    qrow = lambda x, u: x[u * blk:(u + 1) * blk]

    qms = [[jnp.where(q_half[h % 2], qrow(qn[h // 2], u), 0.0) for h in order(g)] for u, g in units]
    scs = [jnp.concatenate([_dot_nt(jnp.concatenate(qm[:2], axis=0), keys(kn[g // 2], u)),
                            _dot_nt(jnp.concatenate(qm[2:], axis=0), keys(kn_sw[g // 2], u))], axis=0)
           for qm, (u, g) in zip(qms, units)]
    sinks = [tile(sink_ref, order(g)) for u, g in units]
    ss = []
    for sc, (u, g) in zip(scs, units):
        s = jnp.where(from_prev4, sc[:, :blk], sc[:, blk:]) * scale - tile(slope_ref, order(g)) * delta4
        ss.append(s if neg4[u] is None else s + neg4[u])
    ms = [jnp.maximum(jnp.broadcast_to(jnp.max(s, axis=-1, keepdims=True), s.shape), sk)
          for s, sk in zip(ss, sinks)]
    ps = [jnp.exp(s - m) for s, m in zip(ss, ms)]
    ones = jnp.ones((blk, LANES), BF16)
    invs = [1.0 / (_dot(p, ones) + jnp.exp(sk - m)) for p, sk, m in zip(ps, sinks, ms)]
    pcats = [jnp.concatenate([jnp.where(from_prev4, p, 0.0), jnp.where(from_prev4, 0.0, p)], axis=1) for p in ps]
    outs = [jnp.concatenate([_dot(pc[:2 * blk], keys(v_same[g], u)), _dot(pc[2 * blk:], keys(v_swap[g], u))],
                            axis=0) * inv
            for pc, inv, (u, g) in zip(pcats, invs, units)]
    for o, (u, g) in zip(outs, units):
        hs = order(g)
        for j in sorted({h // 2 for h in hs}):
            pair = sum(o[idx * blk:(idx + 1) * blk] for idx, h in enumerate(hs) if h // 2 == j)
            o_ref[u * blk:(u + 1) * blk, j * LANES:(j + 1) * LANES] = pair.astype(o_ref.dtype)


def _swa(proj, batch, seq, q_gain, k_gain, sinks):
    nbk = seq // CHUNK
    qrows = SWA_QBLOCKS * CHUNK
    nsteps = seq // qrows
    n = batch * seq
    slopes = (2.0 ** (-8.0 * jnp.arange(1, ATT_HEADS + 1, dtype=F32) / ATT_HEADS))
    pad = lambda v: jnp.pad(v.astype(F32), (0, LANES - v.shape[0])).reshape(1, LANES)
    pair = lambda v: jnp.tile(v.astype(F32), LANES // HEAD_DIM).reshape(1, LANES)
    full = lambda shape: pl.BlockSpec(shape, lambda b, j: (0, 0))
    kvw = 2 * KV_DIM
    return pl.pallas_call(
        _swa_kernel,
        grid=(batch, nsteps),
        in_specs=[pl.BlockSpec((qrows, Q_DIM), lambda b, j: (b * nsteps + j, 0)),
                  pl.BlockSpec((CHUNK, kvw),
                               lambda b, j: (b * nbk + jnp.maximum(SWA_QBLOCKS * j - 1, 0), Q_DIM // kvw)),
                  pl.BlockSpec((qrows, kvw), lambda b, j: (b * nsteps + j, Q_DIM // kvw)),
                  full((1, LANES)), full((1, LANES)), full((1, LANES)), full((1, LANES))],
        out_specs=pl.BlockSpec((qrows, Q_DIM), lambda b, j: (b * nsteps + j, 0)),
        out_shape=jax.ShapeDtypeStruct((n, Q_DIM), BF16),
        compiler_params=_cparams(("arbitrary", "arbitrary")),
        name="swa",
    )(proj, proj, proj, pair(q_gain), pair(k_gain), pad(slopes), pad(sinks))


def _even_in_weight(w):
    rw = SSD_COLS
    cols = jnp.concatenate([w[:, :SSD_INNER + SSD_CONV_DIM], w[:, rw:],
                            w[:, SSD_INNER + SSD_CONV_DIM:SSD_COLS]], axis=1)
    return jnp.pad(cols, ((0, 0), (0, EVEN_COLS_PAD - cols.shape[1]))).astype(BF16)


def kernel(x, ln_mix, ln_ffn, e_w_in, e_w_out, ssd_conv_w, ssd_conv_b, ssd_dt_bias, ssd_a_log, ssd_d, ssd_norm,
           rwkv_mu, rwkv_w0, rwkv_w2, rwkv_a0, rwkv_a2, rwkv_g2, rwkv_k_k, rwkv_k_a, rwkv_r_k, rwkv_ln_w,
           rwkv_ln_b, o_w_in, o_w_out, attn_q_norm, attn_k_norm, attn_sinks, moe_w_coarse, moe_b_coarse,
           moe_w_fine, moe_b_fine, moe_w_gate, moe_w_up, moe_w_down):
    batch, seq, d = x.shape
    n = batch * seq
    tm = min(512, n)
    tm_mm = min(1024, n)
    h = x.reshape(n, d)

    def moe(layer, hf, route):
        plan = _moe_plan(route, tm)
        xs = _dispatch(hf, plan, tm)
        ys = _experts(xs, plan, layer, moe_w_gate, moe_w_up, moe_w_down)
        return ys, plan["dest3"]

    def router_w(layer):
        return _router_weights(moe_w_coarse[layer], moe_b_coarse[layer], moe_w_fine[layer], moe_b_fine[layer])

    proj = _norm_proj(h, ln_mix[0], _even_in_weight(e_w_in[0]), tm_mm, EVEN_COLS_PAD // 3)
    y_ssd, y_rwkv = _mixers(
        proj, batch, seq,
        (ssd_conv_w[0], ssd_conv_b[0], ssd_dt_bias[0], ssd_a_log[0], ssd_d[0], ssd_norm[0]),
        (rwkv_mu[0], rwkv_w0[0], rwkv_w2[0], rwkv_a0[0], rwkv_a2[0], rwkv_g2[0], rwkv_k_k[0], rwkv_k_a[0],
         rwkv_r_k[0].reshape(-1), rwkv_ln_w[0], rwkv_ln_b[0]))
    w_out = e_w_out[0].astype(BF16)
    wr, br = router_w(0)
    h, hf, route = _outproj_router([y_ssd, y_rwkv], [w_out[:SSD_INNER], w_out[SSD_INNER:]], h, ln_ffn[0],
                                   wr, br, tm_mm)
    ys, dest3 = moe(0, hf, route)

    h, proj = _combine_norm_proj(h, ys, route, dest3, ln_mix[1], o_w_in[0].astype(BF16), tm)
    att = _swa(proj, batch, seq, attn_q_norm[0], attn_k_norm[0], attn_sinks[0])
    wr, br = router_w(1)
    h, hf, route = _outproj_router([att], [o_w_out[0].astype(BF16)], h, ln_ffn[1], wr, br, tm_mm)
    ys, dest3 = moe(1, hf, route)
    out = _combine(h, ys, route, dest3, tm)
    return out.reshape(batch, seq, d)
```
